```python
import jax, jax.numpy as jnp
from jax import lax
import numpy as np

D_MODEL = 2048
BATCH = 16
SEQ = 2048
DEPTH = 4

HEAD_DIM = 128
N_HEADS_NA = 6
N_HEADS_DIL = 6
N_HEADS_MEM = 4
W_NA = N_HEADS_NA * HEAD_DIM
W_DIL = N_HEADS_DIL * HEAD_DIM
W_MEM = N_HEADS_MEM * HEAD_DIM
MIX_WIDTH = W_NA + W_DIL + W_MEM
IN_SPLITS = (W_NA,) * 4 + (W_DIL,) * 4 + (W_MEM,) * 2
IN_COLS = sum(IN_SPLITS)
N_MEM = 256
GRID_W = 64
NA_WIN_ROWS = 8
NA_WIN_COLS = 16
NA_QCOL_BLOCK = 16
NA_KCOL_BLOCK = 32
DIL_CONFIGS = ((128, 1), (512, 4), (2048, 16))
ROPE_THETA = 10000.0
EPS = 1e-6
NEG = -1e30
SCALE = HEAD_DIM ** -0.5

kernel_name = "hybrid_natten_dilated_memory_encoder"


def rms_norm(x, g):
    xf = x.astype(jnp.float32)
    y = xf * lax.rsqrt(jnp.mean(xf * xf, axis=-1, keepdims=True) + EPS)
    return (y * g.astype(jnp.float32)).astype(x.dtype)


def rope(x, pos):
    half = HEAD_DIM // 2
    inv = ROPE_THETA ** (-jnp.arange(half, dtype=jnp.float32) / half)
    ang = pos.astype(jnp.float32)[:, None] * inv[None, :]
    cos = jnp.cos(ang)[None, :, None, :]
    sin = jnp.sin(ang)[None, :, None, :]
    xf = x.astype(jnp.float32)
    x1, x2 = xf[..., :half], xf[..., half:]
    return jnp.concatenate([x1 * cos - x2 * sin, x2 * cos + x1 * sin], axis=-1).astype(x.dtype)


def neighbourhood_attention(q, k, v, rpb):
    B, T, H, hd = q.shape
    rows = T // GRID_W
    win_r = min(NA_WIN_ROWS, rows)
    n_cb = GRID_W // NA_QCOL_BLOCK

    def grid(a):
        return a.reshape(B, rows, GRID_W, H, hd).transpose(1, 0, 3, 2, 4)

    qg, kg, vg = grid(q), grid(k), grid(v)
    qcol = np.arange(GRID_W).reshape(n_cb, NA_QCOL_BLOCK)
    kstart = np.clip(np.arange(n_cb) * NA_QCOL_BLOCK - NA_WIN_COLS // 2, 0, GRID_W - NA_KCOL_BLOCK)
    kcol = kstart[:, None] + np.arange(NA_KCOL_BLOCK)[None, :]
    cstart = np.clip(qcol - NA_WIN_COLS // 2, 0, GRID_W - NA_WIN_COLS)
    col_ok = (kcol[:, None, :] >= cstart[:, :, None]) & (kcol[:, None, :] < cstart[:, :, None] + NA_WIN_COLS)
    dc_idx = np.clip(kcol[:, None, :] - qcol[:, :, None], -(NA_WIN_COLS - 1), NA_WIN_COLS - 1) + NA_WIN_COLS - 1
    rpb32 = rpb.astype(jnp.float32)

    def one_row(r):
        rs = jnp.clip(r - win_r // 2, 0, rows - win_r)
        k_rows = lax.dynamic_slice_in_dim(kg, rs, win_r, axis=0)
        v_rows = lax.dynamic_slice_in_dim(vg, rs, win_r, axis=0)
        k_blk = k_rows[:, :, :, kcol]
        v_blk = v_rows[:, :, :, kcol]
        q_blk = qg[r].reshape(B, H, n_cb, NA_QCOL_BLOCK, hd)
        s = jnp.einsum('bhcqd,wbhcjd->bhcqwj', q_blk, k_blk).astype(jnp.float32) * SCALE
        dr_idx = rs + jnp.arange(win_r) - r + NA_WIN_ROWS - 1
        bias = rpb32[:, dr_idx[None, None, :, None], dc_idx[:, :, None, :]]
        s = jnp.where(col_ok[:, :, None, :], s + bias[None], NEG)
        p = jax.nn.softmax(s.reshape(B, H, n_cb, NA_QCOL_BLOCK, win_r * NA_KCOL_BLOCK), axis=-1)
        p = p.reshape(s.shape).astype(v.dtype)
        o = jnp.einsum('bhcqwj,wbhcjd->bhcqd', p, v_blk)
        return o.reshape(B, H, GRID_W, hd)

    out = lax.map(one_row, jnp.arange(rows))
    return out.transpose(1, 0, 3, 2, 4).reshape(B, T, H, hd)


def banded_attention(q, k, v, half):
    lead = q.shape[:-2]
    L, hd = q.shape[-2], q.shape[-1]
    nb = -(-L // half)
    Lp = nb * half
    nl = len(lead)
    qb = jnp.pad(q, [(0, 0)] * nl + [(0, Lp - L), (0, 0)]).reshape(*lead, nb, half, hd)

    def key_blocks(a):
        c = jnp.pad(a, [(0, 0)] * nl + [(half, Lp - L + half), (0, 0)]).reshape(*lead, nb + 2, half, hd)
        return jnp.concatenate([c[..., :-2, :, :], c[..., 1:-1, :, :], c[..., 2:, :, :]], axis=-2)

    kb, vb = key_blocks(k), key_blocks(v)
    qpos = np.arange(Lp).reshape(nb, half)
    kpos = (np.arange(nb)[:, None] - 1) * half + np.arange(3 * half)[None, :]
    ok = (np.abs(kpos[:, None, :] - qpos[:, :, None]) <= half) & (kpos[:, None, :] >= 0) & (kpos[:, None, :] < L)
    s = jnp.einsum('...nqd,...nkd->...nqk', qb, kb).astype(jnp.float32) * SCALE
    s = jnp.where(ok, s, NEG)
    lse = jax.nn.logsumexp(s, axis=-1)
    p = jnp.exp(s - lse[..., None]).astype(v.dtype)
    o = jnp.einsum('...nqk,...nkd->...nqd', p, vb)
    return o.reshape(*lead, Lp, hd)[..., :L, :], lse.reshape(*lead, Lp)[..., :L]


def dilated_attention(q, k, v):
    B, T, H, hd = q.shape
    outs, lses = [], []
    for window, dil in DIL_CONFIGS:
        half = (window // 2) // dil
        L = T // dil

        def stream(a):
            return a.reshape(B, L, dil, H, hd).transpose(0, 2, 3, 1, 4)

        o, lse = banded_attention(stream(q), stream(k), stream(v), half)
        outs.append(o.transpose(0, 3, 1, 2, 4).reshape(B, T, H, hd))
        lses.append(lse.transpose(0, 3, 1, 2).reshape(B, T, H))
    w = jax.nn.softmax(jnp.stack(lses), axis=0)
    out = jnp.sum(w[..., None] * jnp.stack(outs).astype(jnp.float32), axis=0)
    return out.astype(q.dtype)


def memory_attention(q, mk, mv):
    s = jnp.einsum('bthd,bmhd->bhtm', q, mk).astype(jnp.float32) * SCALE
    p = jax.nn.softmax(s, axis=-1).astype(mv.dtype)
    return jnp.einsum('bhtm,bmhd->bthd', p, mv)


def setup_inputs(seed: int = 0) -> dict:
    key = jax.random.key(seed)
    ks = jax.random.split(key, 10)
    f32 = jnp.float32
    x = jax.random.normal(ks[0], (BATCH, SEQ, D_MODEL), f32)
    mem = jax.random.normal(ks[1], (BATCH, N_MEM, D_MODEL), f32)
    norm_g = 1.0 + 0.02 * jax.random.normal(ks[2], (DEPTH, D_MODEL), f32)
    w_in = jax.random.normal(ks[3], (DEPTH, D_MODEL, IN_COLS), f32) * D_MODEL ** -0.5
    na_rpb = 0.02 * jax.random.normal(ks[4], (DEPTH, N_HEADS_NA, 2 * NA_WIN_ROWS - 1, 2 * NA_WIN_COLS - 1), f32)
    mem_norm_g = 1.0 + 0.02 * jax.random.normal(ks[5], (D_MODEL,), f32)
    w_mem_kv = jax.random.normal(ks[6], (DEPTH, D_MODEL, 2 * W_MEM), f32) * D_MODEL ** -0.5
    w_out = jax.random.normal(ks[7], (DEPTH, MIX_WIDTH, D_MODEL), f32) * MIX_WIDTH ** -0.5
    final_g = 1.0 + 0.02 * jax.random.normal(ks[8], (D_MODEL,), f32)
    return {"x": x, "mem": mem, "norm_g": norm_g, "w_in": w_in, "na_rpb": na_rpb,
            "mem_norm_g": mem_norm_g, "w_mem_kv": w_mem_kv, "w_out": w_out, "final_g": final_g}


def reference(x, mem, norm_g, w_in, na_rpb, mem_norm_g, w_mem_kv, w_out, final_g):
    B, T, _ = x.shape
    pos = jnp.arange(T)
    offsets = np.cumsum(IN_SPLITS)[:-1].tolist()
    mem_n = rms_norm(mem, mem_norm_g)

    def heads(a, n):
        return a.reshape(a.shape[0], a.shape[1], n, HEAD_DIM)

    for l in range(DEPTH):
        h = rms_norm(x, norm_g[l])
        z = h @ w_in[l]
        na_q, na_k, na_v, na_g, dl_q, dl_k, dl_v, dl_g, m_q, m_g = jnp.split(z, offsets, axis=-1)
        o_na = neighbourhood_attention(heads(na_q, N_HEADS_NA), heads(na_k, N_HEADS_NA),
                                       heads(na_v, N_HEADS_NA), na_rpb[l]).reshape(B, T, W_NA)
        o_dl = dilated_attention(rope(heads(dl_q, N_HEADS_DIL), pos), rope(heads(dl_k, N_HEADS_DIL), pos),
                                 heads(dl_v, N_HEADS_DIL)).reshape(B, T, W_DIL)
        mk, mv = jnp.split(mem_n @ w_mem_kv[l], 2, axis=-1)
        o_m = memory_attention(heads(m_q, N_HEADS_MEM), heads(mk, N_HEADS_MEM),
                               heads(mv, N_HEADS_MEM)).reshape(B, T, W_MEM)
        y = jnp.concatenate([o_na * jax.nn.silu(na_g), o_dl * jax.nn.silu(dl_g), o_m * jax.nn.silu(m_g)], axis=-1)
        x = x + y @ w_out[l]
    return rms_norm(x, final_g)
```

```python
import functools

import numpy as np
import jax
import jax.numpy as jnp
from jax import lax
from jax.experimental import pallas as pl
from jax.experimental.pallas import tpu as pltpu

D_MODEL = 2048
HEAD_DIM = 128
N_HEADS_NA = 6
N_HEADS_DIL = 6
N_HEADS_MEM = 4
W_NA = N_HEADS_NA * HEAD_DIM
W_DIL = N_HEADS_DIL * HEAD_DIM
W_MEM = N_HEADS_MEM * HEAD_DIM
IN_COLS = 4 * W_NA + 4 * W_DIL + 2 * W_MEM
N_MEM = 256
GRID_W = 64
NA_WIN_ROWS = 8
NA_WIN_COLS = 16
DIL_CONFIGS = ((128, 1), (512, 4), (2048, 16))
ROPE_THETA = 10000.0
EPS = 1e-6
NEG = -1e30
SCALE = HEAD_DIM ** -0.5

OFF_NA_Q, OFF_NA_K, OFF_NA_V, OFF_NA_G = 0, W_NA, 2 * W_NA, 3 * W_NA
OFF_DL_Q = 4 * W_NA
OFF_DL_K, OFF_DL_V, OFF_DL_G = OFF_DL_Q + W_DIL, OFF_DL_Q + 2 * W_DIL, OFF_DL_Q + 3 * W_DIL
OFF_M_Q = OFF_DL_Q + 4 * W_DIL
OFF_M_G = OFF_M_Q + W_MEM

IN_SEGMENTS = (
    (OFF_NA_Q, W_NA, "scale"), (OFF_NA_K, W_NA, "plain"), (OFF_NA_V, W_NA, "plain"), (OFF_NA_G, W_NA, "silu"),
    (OFF_DL_Q, W_DIL, "rope_scale"), (OFF_DL_K, W_DIL, "rope"), (OFF_DL_V, W_DIL, "plain"), (OFF_DL_G, W_DIL, "silu"),
    (OFF_M_Q, W_MEM, "scale"), (OFF_M_G, W_MEM, "silu"),
)

V7X_VMEM_LIMIT = 56 * 1024 * 1024

PROJ_TM = 256
NA_ROWS_PER_GROUP = 4
NA_Q = NA_ROWS_PER_GROUP * GRID_W
NA_KEY_ROWS = NA_ROWS_PER_GROUP + NA_WIN_ROWS
NA_K = NA_KEY_ROWS * GRID_W
DL_Q = 256
MEM_Q = 256


def _rms_norm_f32(x, g):
    return x * lax.rsqrt(jnp.mean(x * x, axis=-1, keepdims=True) + EPS) * g


def _dot_nt(a, b):
    return lax.dot_general(a, b, (((1,), (1,)), ((), ())), preferred_element_type=jnp.float32)


def _in_proj_kernel(x_ref, g_ref, w_ref, cos_ref, sin_ref, o_ref, *, segments):
    h = _rms_norm_f32(x_ref[...], g_ref[...]).astype(jnp.bfloat16)
    for off, width, kind in segments:
        if kind in ("rope", "rope_scale"):
            for c in range(off, off + width, HEAD_DIM):
                a = jnp.dot(h, w_ref[:, c:c + HEAD_DIM], preferred_element_type=jnp.float32)
                a = a * cos_ref[...] + pltpu.roll(a, HEAD_DIM // 2, axis=1) * sin_ref[...]
                if kind == "rope_scale":
                    a = a * SCALE
                o_ref[:, c:c + HEAD_DIM] = a.astype(o_ref.dtype)
        else:
            a = jnp.dot(h, w_ref[:, off:off + width], preferred_element_type=jnp.float32)
            if kind == "scale":
                a = a * SCALE
            elif kind == "silu":
                a = a * (1.0 / (1.0 + jnp.exp(-a)))
            o_ref[:, off:off + width] = a.astype(o_ref.dtype)


def _in_proj(x, g, w_bf16, cos2, sin2, segments, seq):
    n, d = x.shape
    cols = w_bf16.shape[1]
    tm = PROJ_TM
    blocks_per_seq = seq // tm
    return pl.pallas_call(
        functools.partial(_in_proj_kernel, segments=segments),
        grid=(n // tm,),
        in_specs=[
            pl.BlockSpec((tm, d), lambda i: (i, 0)),
            pl.BlockSpec((1, d), lambda i: (0, 0)),
            pl.BlockSpec((d, cols), lambda i: (0, 0), pipeline_mode=pl.Buffered(1)),
            pl.BlockSpec((tm, HEAD_DIM), lambda i: (i % blocks_per_seq, 0)),
            pl.BlockSpec((tm, HEAD_DIM), lambda i: (i % blocks_per_seq, 0)),
        ],
        out_specs=pl.BlockSpec((tm, cols), lambda i: (i, 0)),
        out_shape=jax.ShapeDtypeStruct((n, cols), jnp.bfloat16),
        compiler_params=pltpu.CompilerParams(
            dimension_semantics=("parallel",), vmem_limit_bytes=V7X_VMEM_LIMIT),
        name="in_proj",
    )(x, g.reshape(1, d), w_bf16, cos2, sin2)


def _out_proj_kernel(x_ref, yna_ref, ydl_ref, ym_ref, w_ref, g_ref, o_ref, *, final_norm):
    acc = jnp.dot(yna_ref[...], w_ref[0:W_NA, :], preferred_element_type=jnp.float32)
    acc += jnp.dot(ydl_ref[...], w_ref[W_NA:W_NA + W_DIL, :], preferred_element_type=jnp.float32)
    acc += jnp.dot(ym_ref[...], w_ref[W_NA + W_DIL:, :], preferred_element_type=jnp.float32)
    x = x_ref[...] + acc
    if final_norm:
        x = _rms_norm_f32(x, g_ref[...])
    o_ref[...] = x


def _out_proj(x, y_na, y_dl, y_m, w_bf16, final_g, final_norm):
    n, d = x.shape
    tm = PROJ_TM
    return pl.pallas_call(
        functools.partial(_out_proj_kernel, final_norm=final_norm),
        grid=(n // tm,),
        in_specs=[
            pl.BlockSpec((tm, d), lambda i: (i, 0)),
            pl.BlockSpec((tm, W_NA), lambda i: (i, 0)),
            pl.BlockSpec((tm, W_DIL), lambda i: (i, 0)),
            pl.BlockSpec((tm, W_MEM), lambda i: (i, 0)),
            pl.BlockSpec((d, d), lambda i: (0, 0), pipeline_mode=pl.Buffered(1)),
            pl.BlockSpec((1, d), lambda i: (0, 0)),
        ],
        out_specs=pl.BlockSpec((tm, d), lambda i: (i, 0)),
        out_shape=jax.ShapeDtypeStruct((n, d), jnp.float32),
        compiler_params=pltpu.CompilerParams(
            dimension_semantics=("parallel",), vmem_limit_bytes=V7X_VMEM_LIMIT),
        name="out_proj",
    )(x, y_na, y_dl, y_m, w_bf16, final_g.reshape(1, d))


def _softmax_pv(s, v, gate):
    m = jnp.max(s, axis=-1, keepdims=True)
    p = jnp.exp(s - m)
    l = jnp.sum(p, axis=-1, keepdims=True)
    o = jnp.dot(p.astype(jnp.bfloat16), v, preferred_element_type=jnp.float32)
    return (o / l) * gate.astype(jnp.float32)


def _na_kernel(q_ref, k_ref, v_ref, gate_ref, bias_ref, o_ref):
    n_groups = q_ref.shape[0] // NA_Q
    rows = q_ref.shape[0] // GRID_W

    def body(g, carry):
        q0 = pl.multiple_of(g * NA_Q, NA_Q)
        base = jnp.clip(g * NA_ROWS_PER_GROUP - NA_WIN_ROWS // 2, 0, rows - NA_KEY_ROWS)
        k0 = pl.multiple_of(base * GRID_W, GRID_W)
        pat = jnp.where(g == 0, 0, jnp.where(g == n_groups - 1, 2, 1))
        s = _dot_nt(q_ref[pl.ds(q0, NA_Q), :], k_ref[pl.ds(k0, NA_K), :]) + bias_ref[pat]
        o = _softmax_pv(s, v_ref[pl.ds(k0, NA_K), :], gate_ref[pl.ds(q0, NA_Q), :])
        o_ref[pl.ds(q0, NA_Q), :] = o.astype(o_ref.dtype)
        return carry

    lax.fori_loop(0, n_groups, body, 0)


def _na_bias_tiles(rpb, rows):
    n_groups = rows // NA_ROWS_PER_GROUP
    tiles_dr, tiles_dc, tiles_ok = [], [], []
    i = np.arange(NA_Q)
    j = np.arange(NA_K)
    for g in (0, 1, n_groups - 1):
        base = int(np.clip(g * NA_ROWS_PER_GROUP - NA_WIN_ROWS // 2, 0, rows - NA_KEY_ROWS))
        r = g * NA_ROWS_PER_GROUP + i // GRID_W
        c = i % GRID_W
        rs = np.clip(r - NA_WIN_ROWS // 2, 0, rows - NA_WIN_ROWS)
        cs = np.clip(c - NA_WIN_COLS // 2, 0, GRID_W - NA_WIN_COLS)
        kr = base + j // GRID_W
        kc = j % GRID_W
        ok = ((kr[None, :] >= rs[:, None]) & (kr[None, :] < rs[:, None] + NA_WIN_ROWS)
              & (kc[None, :] >= cs[:, None]) & (kc[None, :] < cs[:, None] + NA_WIN_COLS))
        dr = np.clip(kr[None, :] - r[:, None] + NA_WIN_ROWS - 1, 0, 2 * NA_WIN_ROWS - 2)
        dc = np.clip(kc[None, :] - c[:, None], -(NA_WIN_COLS - 1), NA_WIN_COLS - 1) + NA_WIN_COLS - 1
        tiles_dr.append(dr)
        tiles_dc.append(dc)
        tiles_ok.append(ok)
    dr = np.stack(tiles_dr)
    dc = np.stack(tiles_dc)
    ok = np.stack(tiles_ok)
    return jnp.where(ok[None], rpb.astype(jnp.float32)[:, dr, dc], NEG)


def _na_attention(z, bias, batch, seq):
    n = z.shape[0]
    qb, kb, vb, gb = (OFF_NA_Q // HEAD_DIM, OFF_NA_K // HEAD_DIM, OFF_NA_V // HEAD_DIM, OFF_NA_G // HEAD_DIM)
    return pl.pallas_call(
        _na_kernel,
        grid=(N_HEADS_NA, batch),
        in_specs=[
            pl.BlockSpec((seq, HEAD_DIM), lambda h, b: (b, qb + h)),
            pl.BlockSpec((seq, HEAD_DIM), lambda h, b: (b, kb + h)),
            pl.BlockSpec((seq, HEAD_DIM), lambda h, b: (b, vb + h)),
            pl.BlockSpec((seq, HEAD_DIM), lambda h, b: (b, gb + h)),
            pl.BlockSpec((None, 3, NA_Q, NA_K), lambda h, b: (h, 0, 0, 0)),
        ],
        out_specs=pl.BlockSpec((seq, HEAD_DIM), lambda h, b: (b, h)),
        out_shape=jax.ShapeDtypeStruct((n, W_NA), jnp.bfloat16),
        compiler_params=pltpu.CompilerParams(
            dimension_semantics=("parallel", "parallel"), vmem_limit_bytes=V7X_VMEM_LIMIT),
        name="na_attention",
    )(z, z, z, z, bias)


def _dl_kernel(q_ref, k_ref, v_ref, gate_ref, bias_ref, o_ref):
    seq = q_ref.shape[0]
    n_blocks = seq // DL_Q

    def body(i, carry):
        q0 = pl.multiple_of(i * DL_Q, DL_Q)
        b0 = pl.multiple_of(seq - DL_Q - q0, DL_Q)
        s = _dot_nt(q_ref[pl.ds(q0, DL_Q), :], k_ref[...]) + bias_ref[:, pl.ds(b0, seq)]
        o = _softmax_pv(s, v_ref[...], gate_ref[pl.ds(q0, DL_Q), :])
        o_ref[pl.ds(q0, DL_Q), :] = o.astype(o_ref.dtype)
        return carry

    lax.fori_loop(0, n_blocks, body, 0)


def _dl_bias(seq):
    r = np.arange(DL_Q)[:, None]
    c = np.arange(2 * seq - DL_Q)[None, :]
    delta = c - (seq - DL_Q) - r
    mult = np.zeros(delta.shape, np.float64)
    for window, dil in DIL_CONFIGS:
        mult += (delta % dil == 0) & (np.abs(delta) <= window // 2)
    with np.errstate(divide="ignore"):
        logm = np.where(mult > 0, np.log(np.maximum(mult, 1.0)), NEG)
    return jnp.asarray(logm, jnp.float32)


def _dl_attention(z, bias, batch, seq):
    n = z.shape[0]
    qb, kb, vb, gb = (OFF_DL_Q // HEAD_DIM, OFF_DL_K // HEAD_DIM, OFF_DL_V // HEAD_DIM, OFF_DL_G // HEAD_DIM)
    return pl.pallas_call(
        _dl_kernel,
        grid=(batch, N_HEADS_DIL),
        in_specs=[
            pl.BlockSpec((seq, HEAD_DIM), lambda b, h: (b, qb + h)),
            pl.BlockSpec((seq, HEAD_DIM), lambda b, h: (b, kb + h)),
            pl.BlockSpec((seq, HEAD_DIM), lambda b, h: (b, vb + h)),
            pl.BlockSpec((seq, HEAD_DIM), lambda b, h: (b, gb + h)),
            pl.BlockSpec(bias.shape, lambda b, h: (0, 0)),
        ],
        out_specs=pl.BlockSpec((seq, HEAD_DIM), lambda b, h: (b, h)),
        out_shape=jax.ShapeDtypeStruct((n, W_DIL), jnp.bfloat16),
        compiler_params=pltpu.CompilerParams(
            dimension_semantics=("parallel", "parallel"), vmem_limit_bytes=V7X_VMEM_LIMIT),
        name="dl_attention",
    )(z, z, z, z, bias)


def _mem_kernel(q_ref, k_ref, v_ref, gate_ref, o_ref):
    n_blocks = q_ref.shape[0] // MEM_Q

    def body(i, carry):
        q0 = pl.multiple_of(i * MEM_Q, MEM_Q)
        s = _dot_nt(q_ref[pl.ds(q0, MEM_Q), :], k_ref[...])
        o = _softmax_pv(s, v_ref[...], gate_ref[pl.ds(q0, MEM_Q), :])
        o_ref[pl.ds(q0, MEM_Q), :] = o.astype(o_ref.dtype)
        return carry

    lax.fori_loop(0, n_blocks, body, 0)


def _mem_attention(z, mkv, layer, batch, seq):
    n = z.shape[0]
    qb, gb = OFF_M_Q // HEAD_DIM, OFF_M_G // HEAD_DIM
    kb = layer * 2 * N_HEADS_MEM
    vb = kb + N_HEADS_MEM
    return pl.pallas_call(
        _mem_kernel,
        grid=(batch, N_HEADS_MEM),
        in_specs=[
            pl.BlockSpec((seq, HEAD_DIM), lambda b, h: (b, qb + h)),
            pl.BlockSpec((N_MEM, HEAD_DIM), lambda b, h: (b, kb + h)),
            pl.BlockSpec((N_MEM, HEAD_DIM), lambda b, h: (b, vb + h)),
            pl.BlockSpec((seq, HEAD_DIM), lambda b, h: (b, gb + h)),
        ],
        out_specs=pl.BlockSpec((seq, HEAD_DIM), lambda b, h: (b, h)),
        out_shape=jax.ShapeDtypeStruct((n, W_MEM), jnp.bfloat16),
        compiler_params=pltpu.CompilerParams(dimension_semantics=("parallel", "parallel")),
        name="mem_attention",
    )(z, mkv, mkv, z)


def _rope_tables(seq):
    half = HEAD_DIM // 2
    inv = ROPE_THETA ** (-jnp.arange(half, dtype=jnp.float32) / half)
    ang = jnp.arange(seq).astype(jnp.float32)[:, None] * inv[None, :]
    cos, sin = jnp.cos(ang), jnp.sin(ang)
    return jnp.concatenate([cos, cos], axis=-1), jnp.concatenate([-sin, sin], axis=-1)


def kernel(x, mem, norm_g, w_in, na_rpb, mem_norm_g, w_mem_kv, w_out, final_g):
    batch, seq, d = x.shape
    depth = w_in.shape[0]
    assert d == D_MODEL and w_in.shape[2] == IN_COLS and seq % GRID_W == 0
    assert seq % PROJ_TM == 0 and seq % NA_Q == 0 and seq % DL_Q == 0 and seq % MEM_Q == 0

    cos2, sin2 = _rope_tables(seq)
    dl_bias = _dl_bias(seq)
    w_in_b = w_in.astype(jnp.bfloat16)
    w_out_b = w_out.astype(jnp.bfloat16)
    w_mem_b = jnp.transpose(w_mem_kv, (1, 0, 2)).reshape(d, depth * 2 * W_MEM).astype(jnp.bfloat16)
    mkv = _in_proj(mem.reshape(batch * N_MEM, d), mem_norm_g, w_mem_b, cos2, sin2,
                   ((0, depth * 2 * W_MEM, "plain"),), N_MEM)

    xf = x.reshape(batch * seq, d)
    for l in range(depth):
        z = _in_proj(xf, norm_g[l], w_in_b[l], cos2, sin2, IN_SEGMENTS, seq)
        na_bias = _na_bias_tiles(na_rpb[l], seq // GRID_W)
        y_na = _na_attention(z, na_bias, batch, seq)
        y_dl = _dl_attention(z, dl_bias, batch, seq)
        y_m = _mem_attention(z, mkv, l, batch, seq)
        xf = _out_proj(xf, y_na, y_dl, y_m, w_out_b[l], final_g, l == depth - 1)
    return xf.reshape(batch, seq, d)
```

```python
import functools

import numpy as np
import jax
import jax.numpy as jnp
from jax import lax
from jax.experimental import pallas as pl
from jax.experimental.pallas import tpu as pltpu

D_MODEL = 2048
HEAD_DIM = 128
N_HEADS_NA = 6
N_HEADS_DIL = 6
N_HEADS_MEM = 4
W_NA = N_HEADS_NA * HEAD_DIM
W_DIL = N_HEADS_DIL * HEAD_DIM
W_MEM = N_HEADS_MEM * HEAD_DIM
IN_COLS = 4 * W_NA + 4 * W_DIL + 2 * W_MEM
N_MEM = 256
GRID_W = 64
NA_WIN_ROWS = 8
NA_WIN_COLS = 16
DIL_CONFIGS = ((128, 1), (512, 4), (2048, 16))
ROPE_THETA = 10000.0
EPS = 1e-6
NEG = -1e30
SCALE = HEAD_DIM ** -0.5

OFF_NA_Q, OFF_NA_K, OFF_NA_V, OFF_NA_G = 0, W_NA, 2 * W_NA, 3 * W_NA
OFF_DL_Q = 4 * W_NA
OFF_DL_K, OFF_DL_V, OFF_DL_G = OFF_DL_Q + W_DIL, OFF_DL_Q + 2 * W_DIL, OFF_DL_Q + 3 * W_DIL
OFF_M_Q = OFF_DL_Q + 4 * W_DIL
OFF_M_G = OFF_M_Q + W_MEM

IN_SEGMENTS = (
    (OFF_NA_Q, W_NA, "scale", None), (OFF_NA_K, W_NA, "plain", None),
    (OFF_NA_V, W_NA, "plain", None), (OFF_NA_G, W_NA, "silu", None),
    (OFF_DL_Q, W_DIL, "rope_scale", 0), (OFF_DL_K, W_DIL, "rope", W_DIL),
    (OFF_DL_V, W_DIL, "plain", 2 * W_DIL), (OFF_DL_G, W_DIL, "silu", None),
    (OFF_M_Q, W_MEM, "scale", None), (OFF_M_G, W_MEM, "silu", None),
)

V7X_VMEM_LIMIT = 56 * 1024 * 1024

PROJ_TM = 256
NA_ROWS_PER_GROUP = 4
NA_Q = NA_ROWS_PER_GROUP * GRID_W
NA_KEY_ROWS = NA_ROWS_PER_GROUP + NA_WIN_ROWS
NA_K = NA_KEY_ROWS * GRID_W
MEM_Q = 256

DL_STREAMS = 4
DL_HALF = 64
DL_Q = 128
DL_KA = DL_Q + 2 * DL_HALF


def _rms_norm_f32(x, g):
    return x * lax.rsqrt(jnp.mean(x * x, axis=-1, keepdims=True) + EPS) * g


def _dot_nt(a, b):
    return lax.dot_general(a, b, (((1,), (1,)), ((), ())), preferred_element_type=jnp.float32)


def _in_proj_kernel(x_ref, g_ref, w_ref, cos_ref, sin_ref, o_ref, *rest, segments):
    if rest:
        s_ref, scr = rest
    tm = x_ref.shape[0]
    h = _rms_norm_f32(x_ref[...], g_ref[...]).astype(jnp.bfloat16)

    def emit(a, c0, width, s0):
        o_ref[:, c0:c0 + width] = a.astype(o_ref.dtype)
        if s0 is not None:
            for j in range(0, width, HEAD_DIM):
                scr[j // HEAD_DIM] = a[:, j:j + HEAD_DIM]
                for r in range(DL_STREAMS):
                    rows = scr[j // HEAD_DIM, pl.ds(r, tm // DL_STREAMS, stride=DL_STREAMS), :]
                    s_ref[r, :, s0 + j:s0 + j + HEAD_DIM] = rows.astype(s_ref.dtype)

    for off, width, kind, s_off in segments:
        if kind in ("rope", "rope_scale"):
            for j in range(0, width, HEAD_DIM):
                a = jnp.dot(h, w_ref[:, off + j:off + j + HEAD_DIM], preferred_element_type=jnp.float32)
                a = a * cos_ref[...] + pltpu.roll(a, HEAD_DIM // 2, axis=1) * sin_ref[...]
                if kind == "rope_scale":
                    a = a * SCALE
                emit(a, off + j, HEAD_DIM, None if s_off is None else s_off + j)
        else:
            a = jnp.dot(h, w_ref[:, off:off + width], preferred_element_type=jnp.float32)
            if kind == "scale":
                a = a * SCALE
            elif kind == "silu":
                a = a * (1.0 / (1.0 + jnp.exp(-a)))
            emit(a, off, width, s_off)


def _in_proj(x, g, w_bf16, cos2, sin2, segments, batch, seq):
    n, d = x.shape
    cols = w_bf16.shape[1]
    tm = PROJ_TM
    bps = seq // tm
    streamed = any(s[3] is not None for s in segments)
    out_specs = [pl.BlockSpec((tm, cols), lambda i: (i, 0))]
    out_shape = [jax.ShapeDtypeStruct((n, cols), jnp.bfloat16)]
    scratch = []
    if streamed:
        out_specs.append(pl.BlockSpec((None, DL_STREAMS, tm // DL_STREAMS, 3 * W_DIL),
                                      lambda i: (i // bps, 0, i % bps, 0)))
        out_shape.append(jax.ShapeDtypeStruct((batch, DL_STREAMS, seq // DL_STREAMS, 3 * W_DIL), jnp.bfloat16))
        scratch.append(pltpu.VMEM((W_DIL // HEAD_DIM, tm, HEAD_DIM), jnp.float32))
    res = pl.pallas_call(
        functools.partial(_in_proj_kernel, segments=segments),
        grid=(n // tm,),
        in_specs=[
            pl.BlockSpec((tm, d), lambda i: (i, 0)),
            pl.BlockSpec((1, d), lambda i: (0, 0)),
            pl.BlockSpec((d, cols), lambda i: (0, 0), pipeline_mode=pl.Buffered(1)),
            pl.BlockSpec((tm, HEAD_DIM), lambda i: (i % bps, 0)),
            pl.BlockSpec((tm, HEAD_DIM), lambda i: (i % bps, 0)),
        ],
        out_specs=out_specs,
        out_shape=out_shape,
        scratch_shapes=scratch,
        compiler_params=pltpu.CompilerParams(
            dimension_semantics=("parallel",), vmem_limit_bytes=V7X_VMEM_LIMIT),
        name="in_proj",
    )(x, g.reshape(1, d), w_bf16, cos2, sin2)
    return res if streamed else res[0]


def _out_proj_kernel(x_ref, yna_ref, ydl_ref, ym_ref, w_ref, g_ref, o_ref, *, final_norm):
    acc = jnp.dot(yna_ref[...], w_ref[0:W_NA, :], preferred_element_type=jnp.float32)
    acc += jnp.dot(ydl_ref[...], w_ref[W_NA:W_NA + W_DIL, :], preferred_element_type=jnp.float32)
    acc += jnp.dot(ym_ref[...], w_ref[W_NA + W_DIL:, :], preferred_element_type=jnp.float32)
    x = x_ref[...] + acc
    if final_norm:
        x = _rms_norm_f32(x, g_ref[...])
    o_ref[...] = x


def _out_proj(x, y_na, y_dl, y_m, w_bf16, final_g, final_norm):
    n, d = x.shape
    tm = PROJ_TM
    return pl.pallas_call(
        functools.partial(_out_proj_kernel, final_norm=final_norm),
        grid=(n // tm,),
        in_specs=[
            pl.BlockSpec((tm, d), lambda i: (i, 0)),
            pl.BlockSpec((tm, W_NA), lambda i: (i, 0)),
            pl.BlockSpec((tm, W_DIL), lambda i: (i, 0)),
            pl.BlockSpec((tm, W_MEM), lambda i: (i, 0)),
            pl.BlockSpec((d, d), lambda i: (0, 0), pipeline_mode=pl.Buffered(1)),
            pl.BlockSpec((1, d), lambda i: (0, 0)),
        ],
        out_specs=pl.BlockSpec((tm, d), lambda i: (i, 0)),
        out_shape=jax.ShapeDtypeStruct((n, d), jnp.float32),
        compiler_params=pltpu.CompilerParams(
            dimension_semantics=("parallel",), vmem_limit_bytes=V7X_VMEM_LIMIT),
        name="out_proj",
    )(x, y_na, y_dl, y_m, w_bf16, final_g.reshape(1, d))


def _softmax_pv(s, v, gate):
    m = jnp.max(s, axis=-1, keepdims=True)
    p = jnp.exp(s - m)
    l = jnp.sum(p, axis=-1, keepdims=True)
    o = jnp.dot(p.astype(jnp.bfloat16), v, preferred_element_type=jnp.float32)
    return (o / l) * gate.astype(jnp.float32)


def _na_kernel(q_ref, k_ref, v_ref, gate_ref, bias_ref, o_ref):
    n_groups = q_ref.shape[0] // NA_Q
    rows = q_ref.shape[0] // GRID_W

    def body(g, carry):
        q0 = pl.multiple_of(g * NA_Q, NA_Q)
        base = jnp.clip(g * NA_ROWS_PER_GROUP - NA_WIN_ROWS // 2, 0, rows - NA_KEY_ROWS)
        k0 = pl.multiple_of(base * GRID_W, GRID_W)
        pat = jnp.where(g == 0, 0, jnp.where(g == n_groups - 1, 2, 1))
        s = _dot_nt(q_ref[pl.ds(q0, NA_Q), :], k_ref[pl.ds(k0, NA_K), :]) + bias_ref[pat]
        o = _softmax_pv(s, v_ref[pl.ds(k0, NA_K), :], gate_ref[pl.ds(q0, NA_Q), :])
        o_ref[pl.ds(q0, NA_Q), :] = o.astype(o_ref.dtype)
        return carry

    lax.fori_loop(0, n_groups, body, 0, unroll=2)


def _na_bias_tiles(rpb, rows):
    n_groups = rows // NA_ROWS_PER_GROUP
    c = np.arange(GRID_W)
    cs = np.clip(c - NA_WIN_COLS // 2, 0, GRID_W - NA_WIN_COLS)
    col_ok = (c[None, :] >= cs[:, None]) & (c[None, :] < cs[:, None] + NA_WIN_COLS)
    dc = np.clip(c[None, :] - c[:, None], -(NA_WIN_COLS - 1), NA_WIN_COLS - 1) + NA_WIN_COLS - 1
    onehot = (dc[None] == np.arange(2 * NA_WIN_COLS - 1)[:, None, None]).astype(np.float32)
    toep = jnp.einsum("lhaj,jqk->lhaqk", rpb.astype(jnp.float32), jnp.asarray(onehot),
                      precision=lax.Precision.HIGHEST)
    toep = jnp.where(col_ok, toep, NEG)
    neg_blk = jnp.full(toep.shape[:2] + (GRID_W, GRID_W), NEG, jnp.float32)
    tiles = []
    for g in (0, 1, n_groups - 1):
        base = int(np.clip(g * NA_ROWS_PER_GROUP - NA_WIN_ROWS // 2, 0, rows - NA_KEY_ROWS))
        blk_rows = []
        for qi in range(NA_ROWS_PER_GROUP):
            r = g * NA_ROWS_PER_GROUP + qi
            rs = int(np.clip(r - NA_WIN_ROWS // 2, 0, rows - NA_WIN_ROWS))
            blks = []
            for kj in range(NA_KEY_ROWS):
                kr = base + kj
                blks.append(toep[:, :, kr - r + NA_WIN_ROWS - 1] if rs <= kr < rs + NA_WIN_ROWS else neg_blk)
            blk_rows.append(jnp.concatenate(blks, axis=-1))
        tiles.append(jnp.concatenate(blk_rows, axis=-2))
    return jnp.stack(tiles, axis=2)


def _na_attention(z, bias, batch, seq):
    n = z.shape[0]
    qb, kb, vb, gb = (OFF_NA_Q // HEAD_DIM, OFF_NA_K // HEAD_DIM, OFF_NA_V // HEAD_DIM, OFF_NA_G // HEAD_DIM)
    return pl.pallas_call(
        _na_kernel,
        grid=(N_HEADS_NA, batch),
        in_specs=[
            pl.BlockSpec((seq, HEAD_DIM), lambda h, b: (b, qb + h)),
            pl.BlockSpec((seq, HEAD_DIM), lambda h, b: (b, kb + h)),
            pl.BlockSpec((seq, HEAD_DIM), lambda h, b: (b, vb + h)),
            pl.BlockSpec((seq, HEAD_DIM), lambda h, b: (b, gb + h)),
            pl.BlockSpec((None, 3, NA_Q, NA_K), lambda h, b: (h, 0, 0, 0)),
        ],
        out_specs=pl.BlockSpec((seq, HEAD_DIM), lambda h, b: (b, h)),
        out_shape=jax.ShapeDtypeStruct((n, W_NA), jnp.bfloat16),
        compiler_params=pltpu.CompilerParams(
            dimension_semantics=("parallel", "parallel"), vmem_limit_bytes=V7X_VMEM_LIMIT),
        name="na_attention",
    )(z, z, z, z, bias)


def _dl_kernel(qn_ref, kn_ref, vn_ref, gate_ref, qs_ref, ks_ref, vs_ref, bias_a_ref, bias_b_ref, o_ref,
               m_scr, l_scr, acc_scr):
    seq = qn_ref.shape[0]
    slen = seq // DL_STREAMS

    for r in range(DL_STREAMS):
        k_r = ks_ref[r]
        v_r = vs_ref[r]
        for i in range(slen // DL_Q):
            b0 = slen - DL_Q - i * DL_Q
            s = _dot_nt(qs_ref[r, i * DL_Q:(i + 1) * DL_Q, :], k_r) + bias_b_ref[:, b0:b0 + slen]
            m = jnp.max(s, axis=-1, keepdims=True)
            p = jnp.exp(s - m)
            l = jnp.sum(p, axis=-1, keepdims=True)
            acc = jnp.dot(p.astype(jnp.bfloat16), v_r, preferred_element_type=jnp.float32)
            rows = pl.ds(r + DL_STREAMS * DL_Q * i, DL_Q, stride=DL_STREAMS)
            m_scr[rows, :] = jnp.broadcast_to(m, (DL_Q, HEAD_DIM))
            l_scr[rows, :] = jnp.broadcast_to(l, (DL_Q, HEAD_DIM))
            acc_scr[rows, :] = acc

    def body(i, carry):
        q0 = pl.multiple_of(i * DL_Q, DL_Q)
        k0 = pl.multiple_of(jnp.clip(q0 - DL_HALF, 0, seq - DL_KA), DL_HALF)
        pat = (q0 - k0) // DL_HALF
        s = _dot_nt(qn_ref[pl.ds(q0, DL_Q), :], kn_ref[pl.ds(k0, DL_KA), :]) + bias_a_ref[pat]
        m_b = m_scr[pl.ds(q0, DL_Q), :]
        m = jnp.maximum(jnp.max(s, axis=-1, keepdims=True), m_b)
        p = jnp.exp(s - jnp.concatenate([m] * (DL_KA // HEAD_DIM), axis=1))
        alpha = jnp.exp(m_b - m)
        l = alpha * l_scr[pl.ds(q0, DL_Q), :] + jnp.sum(p, axis=-1, keepdims=True)
        acc = alpha * acc_scr[pl.ds(q0, DL_Q), :] + jnp.dot(
            p.astype(jnp.bfloat16), vn_ref[pl.ds(k0, DL_KA), :], preferred_element_type=jnp.float32)
        o = (acc / l) * gate_ref[pl.ds(q0, DL_Q), :].astype(jnp.float32)
        o_ref[pl.ds(q0, DL_Q), :] = o.astype(o_ref.dtype)
        return carry

    lax.fori_loop(0, seq // DL_Q, body, 0, unroll=4)


def _dl_bias_tables(seq):
    (w1, d1), (w4, d4), (w16, d16) = DIL_CONFIGS
    assert d1 == 1 and d4 == DL_STREAMS and d16 % d4 == 0
    assert (w1 // 2) // d1 == DL_HALF and (w4 // 2) // d4 == DL_HALF and (w16 // 2) // d16 == DL_HALF
    r = np.arange(DL_Q)[:, None]
    j = np.arange(DL_KA)[None, :]
    bias_a = np.stack([np.where(np.abs(j - off - r) <= DL_HALF, 0.0, NEG) for off in (0, DL_HALF, 2 * DL_HALF)])
    slen = seq // DL_STREAMS
    c = np.arange(2 * slen - DL_Q)[None, :]
    delta = c - (slen - DL_Q) - r
    step = d16 // d4
    mult = (np.abs(delta) <= DL_HALF).astype(np.float64) + ((delta % step == 0) & (np.abs(delta) <= DL_HALF * step))
    bias_b = np.where(mult > 0, np.log(np.maximum(mult, 1.0)), NEG)
    return jnp.asarray(bias_a, jnp.float32), jnp.asarray(bias_b, jnp.float32)


def _dl_attention(z, zs, bias_a, bias_b, batch, seq):
    n = z.shape[0]
    qb, kb, vb, gb = (OFF_DL_Q // HEAD_DIM, OFF_DL_K // HEAD_DIM, OFF_DL_V // HEAD_DIM, OFF_DL_G // HEAD_DIM)
    slen = seq // DL_STREAMS
    stream_spec = lambda c0: pl.BlockSpec((None, DL_STREAMS, slen, HEAD_DIM), lambda b, h: (b, 0, 0, c0 + h))
    return pl.pallas_call(
        _dl_kernel,
        grid=(batch, N_HEADS_DIL),
        in_specs=[
            pl.BlockSpec((seq, HEAD_DIM), lambda b, h: (b, qb + h)),
            pl.BlockSpec((seq, HEAD_DIM), lambda b, h: (b, kb + h)),
            pl.BlockSpec((seq, HEAD_DIM), lambda b, h: (b, vb + h)),
            pl.BlockSpec((seq, HEAD_DIM), lambda b, h: (b, gb + h)),
            stream_spec(0), stream_spec(N_HEADS_DIL), stream_spec(2 * N_HEADS_DIL),
            pl.BlockSpec(bias_a.shape, lambda b, h: (0, 0, 0)),
            pl.BlockSpec(bias_b.shape, lambda b, h: (0, 0)),
        ],
        out_specs=pl.BlockSpec((seq, HEAD_DIM), lambda b, h: (b, h)),
        out_shape=jax.ShapeDtypeStruct((n, W_DIL), jnp.bfloat16),
        scratch_shapes=[pltpu.VMEM((seq, HEAD_DIM), jnp.float32)] * 3,
        compiler_params=pltpu.CompilerParams(
            dimension_semantics=("parallel", "parallel"), vmem_limit_bytes=V7X_VMEM_LIMIT),
        name="dl_attention",
    )(z, z, z, z, zs, zs, zs, bias_a, bias_b)


def _mem_kernel(q_ref, k_ref, v_ref, gate_ref, o_ref):
    n_blocks = q_ref.shape[0] // MEM_Q

    def body(i, carry):
        q0 = pl.multiple_of(i * MEM_Q, MEM_Q)
        s = _dot_nt(q_ref[pl.ds(q0, MEM_Q), :], k_ref[...])
        o = _softmax_pv(s, v_ref[...], gate_ref[pl.ds(q0, MEM_Q), :])
        o_ref[pl.ds(q0, MEM_Q), :] = o.astype(o_ref.dtype)
        return carry

    lax.fori_loop(0, n_blocks, body, 0, unroll=True)


def _mem_attention(z, mkv, layer, batch, seq):
    n = z.shape[0]
    qb, gb = OFF_M_Q // HEAD_DIM, OFF_M_G // HEAD_DIM
    kb = layer * 2 * N_HEADS_MEM
    vb = kb + N_HEADS_MEM
    return pl.pallas_call(
        _mem_kernel,
        grid=(batch, N_HEADS_MEM),
        in_specs=[
            pl.BlockSpec((seq, HEAD_DIM), lambda b, h: (b, qb + h)),
            pl.BlockSpec((N_MEM, HEAD_DIM), lambda b, h: (b, kb + h)),
            pl.BlockSpec((N_MEM, HEAD_DIM), lambda b, h: (b, vb + h)),
            pl.BlockSpec((seq, HEAD_DIM), lambda b, h: (b, gb + h)),
        ],
        out_specs=pl.BlockSpec((seq, HEAD_DIM), lambda b, h: (b, h)),
        out_shape=jax.ShapeDtypeStruct((n, W_MEM), jnp.bfloat16),
        compiler_params=pltpu.CompilerParams(dimension_semantics=("parallel", "parallel")),
        name="mem_attention",
    )(z, mkv, mkv, z)


def _rope_tables(seq):
    half = HEAD_DIM // 2
    inv = ROPE_THETA ** (-jnp.arange(half, dtype=jnp.float32) / half)
    ang = jnp.arange(seq).astype(jnp.float32)[:, None] * inv[None, :]
    cos, sin = jnp.cos(ang), jnp.sin(ang)
    return jnp.concatenate([cos, cos], axis=-1), jnp.concatenate([-sin, sin], axis=-1)


def kernel(x, mem, norm_g, w_in, na_rpb, mem_norm_g, w_mem_kv, w_out, final_g):
    batch, seq, d = x.shape
    depth = w_in.shape[0]
    assert d == D_MODEL and w_in.shape[2] == IN_COLS and seq % GRID_W == 0
    assert seq % PROJ_TM == 0 and seq % NA_Q == 0 and seq % MEM_Q == 0
    assert seq % (DL_STREAMS * DL_Q) == 0 and PROJ_TM % DL_STREAMS == 0

    cos2, sin2 = _rope_tables(seq)
    dl_bias_a, dl_bias_b = _dl_bias_tables(seq)
    na_bias = _na_bias_tiles(na_rpb, seq // GRID_W)
    w_in_b = w_in.astype(jnp.bfloat16)
    w_out_b = w_out.astype(jnp.bfloat16)
    w_mem_b = jnp.transpose(w_mem_kv, (1, 0, 2)).reshape(d, depth * 2 * W_MEM).astype(jnp.bfloat16)
    mkv = _in_proj(mem.reshape(batch * N_MEM, d), mem_norm_g, w_mem_b, cos2, sin2,
                   ((0, depth * 2 * W_MEM, "plain", None),), batch, N_MEM)

    xf = x.reshape(batch * seq, d)
    for l in range(depth):
        z, zs = _in_proj(xf, norm_g[l], w_in_b[l], cos2, sin2, IN_SEGMENTS, batch, seq)
        y_na = _na_attention(z, na_bias[l], batch, seq)
        y_dl = _dl_attention(z, zs, dl_bias_a, dl_bias_b, batch, seq)
        y_m = _mem_attention(z, mkv, l, batch, seq)
        xf = _out_proj(xf, y_na, y_dl, y_m, w_out_b[l], final_g, l == depth - 1)
    return xf.reshape(batch, seq, d)
```

```python
import functools

import numpy as np
import jax
import jax.numpy as jnp
from jax import lax
from jax.experimental import pallas as pl
from jax.experimental.pallas import tpu as pltpu

D_MODEL = 2048
HEAD_DIM = 128
N_HEADS_NA = 6
N_HEADS_DIL = 6
N_HEADS_MEM = 4
W_NA = N_HEADS_NA * HEAD_DIM
W_DIL = N_HEADS_DIL * HEAD_DIM
W_MEM = N_HEADS_MEM * HEAD_DIM
IN_COLS = 4 * W_NA + 4 * W_DIL + 2 * W_MEM
N_MEM = 256
GRID_W = 64
NA_WIN_ROWS = 8
NA_WIN_COLS = 16
DIL_CONFIGS = ((128, 1), (512, 4), (2048, 16))
ROPE_THETA = 10000.0
EPS = 1e-6
NEG = -1e30
SCALE = HEAD_DIM ** -0.5

OFF_NA_Q, OFF_NA_K, OFF_NA_V, OFF_NA_G = 0, W_NA, 2 * W_NA, 3 * W_NA
OFF_DL_Q = 4 * W_NA
OFF_DL_K, OFF_DL_V, OFF_DL_G = OFF_DL_Q + W_DIL, OFF_DL_Q + 2 * W_DIL, OFF_DL_Q + 3 * W_DIL
OFF_M_Q = OFF_DL_Q + 4 * W_DIL
OFF_M_G = OFF_M_Q + W_MEM

IN_SEGMENTS = (
    (OFF_NA_Q, W_NA, "scale", None), (OFF_NA_K, W_NA, "plain", None),
    (OFF_NA_V, W_NA, "plain", None), (OFF_NA_G, W_NA, "silu", None),
    (OFF_DL_Q, W_DIL, "rope_scale", 0), (OFF_DL_K, W_DIL, "rope", W_DIL),
    (OFF_DL_V, W_DIL, "plain", 2 * W_DIL), (OFF_DL_G, W_DIL, "silu", None),
    (OFF_M_Q, W_MEM, "scale", None), (OFF_M_G, W_MEM, "silu", None),
)

V7X_VMEM_LIMIT = 56 * 1024 * 1024

PROJ_TM = 256
NA_ROWS_PER_GROUP = 4
NA_Q = NA_ROWS_PER_GROUP * GRID_W
NA_KEY_ROWS = NA_ROWS_PER_GROUP + NA_WIN_ROWS
NA_K = NA_KEY_ROWS * GRID_W
MEM_Q = 512

DL_STREAMS = 4
DL_HALF = 64
DL_Q = 128
DL_KA = DL_Q + 2 * DL_HALF
DL_DEPTH_A = 3


def _rms_norm_f32(x, g):
    return x * lax.rsqrt(jnp.mean(x * x, axis=-1, keepdims=True) + EPS) * g


def _dot_nt(a, b):
    return lax.dot_general(a, b, (((1,), (1,)), ((), ())), preferred_element_type=jnp.float32)


def _in_proj_kernel(x_ref, g_ref, w_ref, cos_ref, sin_ref, o_ref, *rest, segments):
    if rest:
        s_ref, scr = rest
    tm = x_ref.shape[0]
    h = _rms_norm_f32(x_ref[...], g_ref[...]).astype(jnp.bfloat16)

    def emit(a, c0, width, s0):
        o_ref[:, c0:c0 + width] = a.astype(o_ref.dtype)
        if s0 is not None:
            for j in range(0, width, HEAD_DIM):
                scr[j // HEAD_DIM] = a[:, j:j + HEAD_DIM]
                for r in range(DL_STREAMS):
                    rows = scr[j // HEAD_DIM, pl.ds(r, tm // DL_STREAMS, stride=DL_STREAMS), :]
                    s_ref[r, :, s0 + j:s0 + j + HEAD_DIM] = rows.astype(s_ref.dtype)

    for off, width, kind, s_off in segments:
        a = jnp.dot(h, w_ref[:, off:off + width], preferred_element_type=jnp.float32)
        if kind in ("rope", "rope_scale"):
            for j in range(0, width, HEAD_DIM):
                aj = a[:, j:j + HEAD_DIM]
                aj = aj * cos_ref[...] + pltpu.roll(aj, HEAD_DIM // 2, axis=1) * sin_ref[...]
                if kind == "rope_scale":
                    aj = aj * SCALE
                emit(aj, off + j, HEAD_DIM, None if s_off is None else s_off + j)
        else:
            if kind == "scale":
                a = a * SCALE
            elif kind == "silu":
                a = a * (1.0 / (1.0 + jnp.exp(-a)))
            emit(a, off, width, s_off)


def _in_proj(x, g, w_bf16, cos2, sin2, segments, batch, seq):
    n, d = x.shape
    cols = w_bf16.shape[1]
    tm = PROJ_TM
    bps = seq // tm
    streamed = any(s[3] is not None for s in segments)
    out_specs = [pl.BlockSpec((tm, cols), lambda i: (i, 0))]
    out_shape = [jax.ShapeDtypeStruct((n, cols), jnp.bfloat16)]
    scratch = []
    if streamed:
        out_specs.append(pl.BlockSpec((None, DL_STREAMS, tm // DL_STREAMS, 3 * W_DIL),
                                      lambda i: (i // bps, 0, i % bps, 0)))
        out_shape.append(jax.ShapeDtypeStruct((batch, DL_STREAMS, seq // DL_STREAMS, 3 * W_DIL), jnp.bfloat16))
        scratch.append(pltpu.VMEM((W_DIL // HEAD_DIM, tm, HEAD_DIM), jnp.float32))
    res = pl.pallas_call(
        functools.partial(_in_proj_kernel, segments=segments),
        grid=(n // tm,),
        in_specs=[
            pl.BlockSpec((tm, d), lambda i: (i, 0)),
            pl.BlockSpec((1, d), lambda i: (0, 0)),
            pl.BlockSpec((d, cols), lambda i: (0, 0), pipeline_mode=pl.Buffered(1)),
            pl.BlockSpec((tm, HEAD_DIM), lambda i: (i % bps, 0)),
            pl.BlockSpec((tm, HEAD_DIM), lambda i: (i % bps, 0)),
        ],
        out_specs=out_specs,
        out_shape=out_shape,
        scratch_shapes=scratch,
        compiler_params=pltpu.CompilerParams(
            dimension_semantics=("parallel",), vmem_limit_bytes=V7X_VMEM_LIMIT),
        name="in_proj",
    )(x, g.reshape(1, d), w_bf16, cos2, sin2)
    return res if streamed else res[0]


def _out_proj_kernel(x_ref, yna_ref, ydl_ref, ym_ref, w_ref, g_ref, o_ref, *, final_norm):
    acc = jnp.dot(yna_ref[...], w_ref[0:W_NA, :], preferred_element_type=jnp.float32)
    acc += jnp.dot(ydl_ref[...], w_ref[W_NA:W_NA + W_DIL, :], preferred_element_type=jnp.float32)
    acc += jnp.dot(ym_ref[...], w_ref[W_NA + W_DIL:, :], preferred_element_type=jnp.float32)
    x = x_ref[...] + acc
    if final_norm:
        x = _rms_norm_f32(x, g_ref[...])
    o_ref[...] = x


def _out_proj(x, y_na, y_dl, y_m, w_bf16, final_g, final_norm):
    n, d = x.shape
    tm = PROJ_TM
    return pl.pallas_call(
        functools.partial(_out_proj_kernel, final_norm=final_norm),
        grid=(n // tm,),
        in_specs=[
            pl.BlockSpec((tm, d), lambda i: (i, 0)),
            pl.BlockSpec((tm, W_NA), lambda i: (i, 0)),
            pl.BlockSpec((tm, W_DIL), lambda i: (i, 0)),
            pl.BlockSpec((tm, W_MEM), lambda i: (i, 0)),
            pl.BlockSpec((d, d), lambda i: (0, 0), pipeline_mode=pl.Buffered(1)),
            pl.BlockSpec((1, d), lambda i: (0, 0)),
        ],
        out_specs=pl.BlockSpec((tm, d), lambda i: (i, 0)),
        out_shape=jax.ShapeDtypeStruct((n, d), jnp.float32),
        compiler_params=pltpu.CompilerParams(
            dimension_semantics=("parallel",), vmem_limit_bytes=V7X_VMEM_LIMIT),
        name="out_proj",
    )(x, y_na, y_dl, y_m, w_bf16, final_g.reshape(1, d))


def _pipelined_attention(n_blocks, depth, scores_fn, pv_fn, finish_fn, prior_fn=None, stats_fn=None):
    queue = [scores_fn(i) for i in range(min(depth, n_blocks))]
    pending = None
    for i in range(n_blocks):
        s = queue.pop(0)
        if i + depth < n_blocks:
            queue.append(scores_fn(i + depth))
        m = jnp.max(s, axis=-1, keepdims=True)
        if stats_fn is not None:
            stats_fn(i, m)
        if prior_fn is None:
            p = jnp.exp(s - m)
            l = jnp.sum(p, axis=-1, keepdims=True)
            acc = pv_fn(i, p.astype(jnp.bfloat16))
        else:
            m_b, l_b, acc_b = prior_fn(i)
            m = jnp.maximum(m, m_b)
            p = jnp.exp(s - jnp.concatenate([m] * (s.shape[1] // m.shape[1]), axis=1))
            alpha = jnp.exp(m_b - m)
            l = alpha * l_b + jnp.sum(p, axis=-1, keepdims=True)
            acc = alpha * acc_b + pv_fn(i, p.astype(jnp.bfloat16))
        if pending is not None:
            finish_fn(*pending)
        pending = (i, l, acc)
    finish_fn(*pending)


def _na_kernel(q_ref, k_ref, v_ref, gate_ref, bias_ref, o_ref):
    n_groups = q_ref.shape[0] // NA_Q
    rows = q_ref.shape[0] // GRID_W

    def key_start(g):
        return int(np.clip(g * NA_ROWS_PER_GROUP - NA_WIN_ROWS // 2, 0, rows - NA_KEY_ROWS)) * GRID_W

    def scores(g):
        pat = 0 if g == 0 else (2 if g == n_groups - 1 else 1)
        k0 = key_start(g)
        return _dot_nt(q_ref[g * NA_Q:(g + 1) * NA_Q, :], k_ref[k0:k0 + NA_K, :]) + bias_ref[pat]

    def pv(g, p):
        k0 = key_start(g)
        return jnp.dot(p, v_ref[k0:k0 + NA_K, :], preferred_element_type=jnp.float32)

    def finish(g, l, acc):
        o = (acc / l) * gate_ref[g * NA_Q:(g + 1) * NA_Q, :].astype(jnp.float32)
        o_ref[g * NA_Q:(g + 1) * NA_Q, :] = o.astype(o_ref.dtype)

    _pipelined_attention(n_groups, 1, scores, pv, finish)


def _na_bias_tiles(rpb, rows):
    n_groups = rows // NA_ROWS_PER_GROUP
    c = np.arange(GRID_W)
    cs = np.clip(c - NA_WIN_COLS // 2, 0, GRID_W - NA_WIN_COLS)
    col_ok = (c[None, :] >= cs[:, None]) & (c[None, :] < cs[:, None] + NA_WIN_COLS)
    dc = np.clip(c[None, :] - c[:, None], -(NA_WIN_COLS - 1), NA_WIN_COLS - 1) + NA_WIN_COLS - 1
    onehot = (dc[None] == np.arange(2 * NA_WIN_COLS - 1)[:, None, None]).astype(np.float32)
    toep = jnp.einsum("lhaj,jqk->lhaqk", rpb.astype(jnp.float32), jnp.asarray(onehot),
                      precision=lax.Precision.HIGHEST)
    toep = jnp.where(col_ok, toep, NEG)
    neg_blk = jnp.full(toep.shape[:2] + (GRID_W, GRID_W), NEG, jnp.float32)
    tiles = []
    for g in (0, 1, n_groups - 1):
        base = int(np.clip(g * NA_ROWS_PER_GROUP - NA_WIN_ROWS // 2, 0, rows - NA_KEY_ROWS))
        blk_rows = []
        for qi in range(NA_ROWS_PER_GROUP):
            r = g * NA_ROWS_PER_GROUP + qi
            rs = int(np.clip(r - NA_WIN_ROWS // 2, 0, rows - NA_WIN_ROWS))
            blks = []
            for kj in range(NA_KEY_ROWS):
                kr = base + kj
                blks.append(toep[:, :, kr - r + NA_WIN_ROWS - 1] if rs <= kr < rs + NA_WIN_ROWS else neg_blk)
            blk_rows.append(jnp.concatenate(blks, axis=-1))
        tiles.append(jnp.concatenate(blk_rows, axis=-2))
    return jnp.stack(tiles, axis=2)


def _na_attention(z, bias, batch, seq):
    n = z.shape[0]
    qb, kb, vb, gb = (OFF_NA_Q // HEAD_DIM, OFF_NA_K // HEAD_DIM, OFF_NA_V // HEAD_DIM, OFF_NA_G // HEAD_DIM)
    return pl.pallas_call(
        _na_kernel,
        grid=(N_HEADS_NA, batch),
        in_specs=[
            pl.BlockSpec((seq, HEAD_DIM), lambda h, b: (b, qb + h)),
            pl.BlockSpec((seq, HEAD_DIM), lambda h, b: (b, kb + h)),
            pl.BlockSpec((seq, HEAD_DIM), lambda h, b: (b, vb + h)),
            pl.BlockSpec((seq, HEAD_DIM), lambda h, b: (b, gb + h)),
            pl.BlockSpec((None, 3, NA_Q, NA_K), lambda h, b: (h, 0, 0, 0)),
        ],
        out_specs=pl.BlockSpec((seq, HEAD_DIM), lambda h, b: (b, h)),
        out_shape=jax.ShapeDtypeStruct((n, W_NA), jnp.bfloat16),
        compiler_params=pltpu.CompilerParams(
            dimension_semantics=("parallel", "parallel"), vmem_limit_bytes=V7X_VMEM_LIMIT),
        name="na_attention",
    )(z, z, z, z, bias)


def _dl_kernel(qn_ref, kn_ref, vn_ref, gate_ref, qs_ref, ks_ref, vs_ref, bias_a_ref, bias_b_ref, o_ref,
               m_scr, l_scr, acc_scr):
    seq = qn_ref.shape[0]
    slen = seq // DL_STREAMS

    def stream_rows(r):
        return pl.ds(r, slen, stride=DL_STREAMS)

    def scores_b(r):
        return _dot_nt(qs_ref[r], ks_ref[r]) + bias_b_ref[...]

    def pv_b(r, p):
        return jnp.dot(p, vs_ref[r], preferred_element_type=jnp.float32)

    def stats_b(r, m):
        m_scr[stream_rows(r), :] = jnp.broadcast_to(m, (slen, HEAD_DIM))

    def finish_b(r, l, acc):
        l_scr[stream_rows(r), :] = jnp.broadcast_to(l, (slen, HEAD_DIM))
        acc_scr[stream_rows(r), :] = acc

    _pipelined_attention(DL_STREAMS, 1, scores_b, pv_b, finish_b, stats_fn=stats_b)

    def key_start(i):
        return int(np.clip(i * DL_Q - DL_HALF, 0, seq - DL_KA))

    def scores_a(i):
        k0 = key_start(i)
        pat = (i * DL_Q - k0) // DL_HALF
        return _dot_nt(qn_ref[i * DL_Q:(i + 1) * DL_Q, :], kn_ref[k0:k0 + DL_KA, :]) + bias_a_ref[pat]

    def pv_a(i, p):
        k0 = key_start(i)
        return jnp.dot(p, vn_ref[k0:k0 + DL_KA, :], preferred_element_type=jnp.float32)

    def prior_a(i):
        rows = slice(i * DL_Q, (i + 1) * DL_Q)
        return m_scr[rows, :], l_scr[rows, :], acc_scr[rows, :]

    def finish_a(i, l, acc):
        rows = slice(i * DL_Q, (i + 1) * DL_Q)
        o = (acc / l) * gate_ref[rows, :].astype(jnp.float32)
        o_ref[rows, :] = o.astype(o_ref.dtype)

    _pipelined_attention(seq // DL_Q, DL_DEPTH_A, scores_a, pv_a, finish_a, prior_fn=prior_a)


def _dl_bias_tables(seq):
    (w1, d1), (w4, d4), (w16, d16) = DIL_CONFIGS
    assert d1 == 1 and d4 == DL_STREAMS and d16 % d4 == 0
    assert (w1 // 2) // d1 == DL_HALF and (w4 // 2) // d4 == DL_HALF and (w16 // 2) // d16 == DL_HALF
    r = np.arange(DL_Q)[:, None]
    j = np.arange(DL_KA)[None, :]
    bias_a = np.stack([np.where(np.abs(j - off - r) <= DL_HALF, 0.0, NEG) for off in (0, DL_HALF, 2 * DL_HALF)])
    slen = seq // DL_STREAMS
    delta = np.arange(slen)[None, :] - np.arange(slen)[:, None]
    step = d16 // d4
    mult = (np.abs(delta) <= DL_HALF).astype(np.float64) + ((delta % step == 0) & (np.abs(delta) <= DL_HALF * step))
    bias_b = np.where(mult > 0, np.log(np.maximum(mult, 1.0)), NEG)
    return jnp.asarray(bias_a, jnp.float32), jnp.asarray(bias_b, jnp.float32)


def _dl_attention(z, zs, bias_a, bias_b, batch, seq):
    n = z.shape[0]
    qb, kb, vb, gb = (OFF_DL_Q // HEAD_DIM, OFF_DL_K // HEAD_DIM, OFF_DL_V // HEAD_DIM, OFF_DL_G // HEAD_DIM)
    slen = seq // DL_STREAMS
    stream_spec = lambda c0: pl.BlockSpec((None, DL_STREAMS, slen, HEAD_DIM), lambda b, h: (b, 0, 0, c0 + h))
    return pl.pallas_call(
        _dl_kernel,
        grid=(batch, N_HEADS_DIL),
        in_specs=[
            pl.BlockSpec((seq, HEAD_DIM), lambda b, h: (b, qb + h)),
            pl.BlockSpec((seq, HEAD_DIM), lambda b, h: (b, kb + h)),
            pl.BlockSpec((seq, HEAD_DIM), lambda b, h: (b, vb + h)),
            pl.BlockSpec((seq, HEAD_DIM), lambda b, h: (b, gb + h)),
            stream_spec(0), stream_spec(N_HEADS_DIL), stream_spec(2 * N_HEADS_DIL),
            pl.BlockSpec(bias_a.shape, lambda b, h: (0, 0, 0)),
            pl.BlockSpec(bias_b.shape, lambda b, h: (0, 0)),
        ],
        out_specs=pl.BlockSpec((seq, HEAD_DIM), lambda b, h: (b, h)),
        out_shape=jax.ShapeDtypeStruct((n, W_DIL), jnp.bfloat16),
        scratch_shapes=[pltpu.VMEM((seq, HEAD_DIM), jnp.float32)] * 3,
        compiler_params=pltpu.CompilerParams(
            dimension_semantics=("parallel", "parallel"), vmem_limit_bytes=V7X_VMEM_LIMIT),
        name="dl_attention",
    )(z, z, z, z, zs, zs, zs, bias_a, bias_b)


def _mem_kernel(q_ref, k_ref, v_ref, gate_ref, o_ref):
    def scores(i):
        return _dot_nt(q_ref[i * MEM_Q:(i + 1) * MEM_Q, :], k_ref[...])

    def pv(i, p):
        return jnp.dot(p, v_ref[...], preferred_element_type=jnp.float32)

    def finish(i, l, acc):
        o = (acc / l) * gate_ref[i * MEM_Q:(i + 1) * MEM_Q, :].astype(jnp.float32)
        o_ref[i * MEM_Q:(i + 1) * MEM_Q, :] = o.astype(o_ref.dtype)

    _pipelined_attention(q_ref.shape[0] // MEM_Q, 1, scores, pv, finish)


def _mem_attention(z, mkv, layer, batch, seq):
    n = z.shape[0]
    qb, gb = OFF_M_Q // HEAD_DIM, OFF_M_G // HEAD_DIM
    kb = layer * 2 * N_HEADS_MEM
    vb = kb + N_HEADS_MEM
    return pl.pallas_call(
        _mem_kernel,
        grid=(batch, N_HEADS_MEM),
        in_specs=[
            pl.BlockSpec((seq, HEAD_DIM), lambda b, h: (b, qb + h)),
            pl.BlockSpec((N_MEM, HEAD_DIM), lambda b, h: (b, kb + h)),
            pl.BlockSpec((N_MEM, HEAD_DIM), lambda b, h: (b, vb + h)),
            pl.BlockSpec((seq, HEAD_DIM), lambda b, h: (b, gb + h)),
        ],
        out_specs=pl.BlockSpec((seq, HEAD_DIM), lambda b, h: (b, h)),
        out_shape=jax.ShapeDtypeStruct((n, W_MEM), jnp.bfloat16),
        compiler_params=pltpu.CompilerParams(dimension_semantics=("parallel", "parallel")),
        name="mem_attention",
    )(z, mkv, mkv, z)


def _rope_tables(seq):
    half = HEAD_DIM // 2
    inv = ROPE_THETA ** (-jnp.arange(half, dtype=jnp.float32) / half)
    ang = jnp.arange(seq).astype(jnp.float32)[:, None] * inv[None, :]
    cos, sin = jnp.cos(ang), jnp.sin(ang)
    return jnp.concatenate([cos, cos], axis=-1), jnp.concatenate([-sin, sin], axis=-1)


def kernel(x, mem, norm_g, w_in, na_rpb, mem_norm_g, w_mem_kv, w_out, final_g):
    batch, seq, d = x.shape
    depth = w_in.shape[0]
    assert d == D_MODEL and w_in.shape[2] == IN_COLS and seq % GRID_W == 0
    assert seq % PROJ_TM == 0 and seq % NA_Q == 0 and seq % MEM_Q == 0
    assert seq % (DL_STREAMS * DL_Q) == 0 and PROJ_TM % DL_STREAMS == 0

    cos2, sin2 = _rope_tables(seq)
    dl_bias_a, dl_bias_b = _dl_bias_tables(seq)
    na_bias = _na_bias_tiles(na_rpb, seq // GRID_W)
    w_in_b = w_in.astype(jnp.bfloat16)
    w_out_b = w_out.astype(jnp.bfloat16)
    w_mem_b = jnp.transpose(w_mem_kv, (1, 0, 2)).reshape(d, depth * 2 * W_MEM).astype(jnp.bfloat16)
    mkv = _in_proj(mem.reshape(batch * N_MEM, d), mem_norm_g, w_mem_b, cos2, sin2,
                   ((0, depth * 2 * W_MEM, "plain", None),), batch, N_MEM)

    xf = x.reshape(batch * seq, d)
    for l in range(depth):
        z, zs = _in_proj(xf, norm_g[l], w_in_b[l], cos2, sin2, IN_SEGMENTS, batch, seq)
        y_na = _na_attention(z, na_bias[l], batch, seq)
        y_dl = _dl_attention(z, zs, dl_bias_a, dl_bias_b, batch, seq)
        y_m = _mem_attention(z, mkv, l, batch, seq)
        xf = _out_proj(xf, y_na, y_dl, y_m, w_out_b[l], final_g, l == depth - 1)
    return xf.reshape(batch, seq, d)
```

```python
import functools

import numpy as np
import jax
import jax.numpy as jnp
from jax import lax
from jax.experimental import pallas as pl
from jax.experimental.pallas import tpu as pltpu

D_MODEL = 2048
HEAD_DIM = 128
N_HEADS_NA = 6
N_HEADS_DIL = 6
N_HEADS_MEM = 4
W_NA = N_HEADS_NA * HEAD_DIM
W_DIL = N_HEADS_DIL * HEAD_DIM
W_MEM = N_HEADS_MEM * HEAD_DIM
IN_COLS = 4 * W_NA + 4 * W_DIL + 2 * W_MEM
N_MEM = 256
GRID_W = 64
NA_WIN_ROWS = 8
NA_WIN_COLS = 16
DIL_CONFIGS = ((128, 1), (512, 4), (2048, 16))
ROPE_THETA = 10000.0
EPS = 1e-6
NEG = -1e30
SCALE = HEAD_DIM ** -0.5
LOG2E = float(np.log2(np.e))
Q_SCALE = SCALE * LOG2E

OFF_NA_Q, OFF_NA_K, OFF_NA_V, OFF_NA_G = 0, W_NA, 2 * W_NA, 3 * W_NA
OFF_DL_Q = 4 * W_NA
OFF_DL_K, OFF_DL_V, OFF_DL_G = OFF_DL_Q + W_DIL, OFF_DL_Q + 2 * W_DIL, OFF_DL_Q + 3 * W_DIL
OFF_M_Q = OFF_DL_Q + 4 * W_DIL
OFF_M_G = OFF_M_Q + W_MEM

IN_SEGMENTS = (
    (OFF_NA_Q, W_NA, "scale", None), (OFF_NA_K, W_NA, "plain", None),
    (OFF_NA_V, W_NA, "plain", None), (OFF_NA_G, W_NA, "silu", None),
    (OFF_DL_Q, W_DIL, "rope_scale", 0), (OFF_DL_K, W_DIL, "rope", W_DIL),
    (OFF_DL_V, W_DIL, "plain", 2 * W_DIL), (OFF_DL_G, W_DIL, "silu", None),
    (OFF_M_Q, W_MEM, "scale", None), (OFF_M_G, W_MEM, "silu", None),
)

V7X_VMEM_LIMIT = 56 * 1024 * 1024

PROJ_TM = 256
NA_ROWS_PER_GROUP = 4
NA_Q = NA_ROWS_PER_GROUP * GRID_W
NA_KEY_ROWS = NA_ROWS_PER_GROUP + NA_WIN_ROWS
NA_K = NA_KEY_ROWS * GRID_W
MEM_Q = 512

DL_STREAMS = 4
DL_HALF = 64
DL_Q = 128
DL_KA = DL_Q + 2 * DL_HALF
DL_DEPTH_A = 4


def _rms_norm_f32(x, g):
    return x * lax.rsqrt(jnp.mean(x * x, axis=-1, keepdims=True) + EPS) * g


def _dot_nt(a, b):
    return lax.dot_general(a, b, (((1,), (1,)), ((), ())), preferred_element_type=jnp.float32)


def _in_proj_kernel(x_ref, g_ref, w_ref, cos_ref, sin_ref, o_ref, *rest, segments):
    if rest:
        s_ref, scr = rest
    tm = x_ref.shape[0]
    h = _rms_norm_f32(x_ref[...], g_ref[...]).astype(jnp.bfloat16)

    def emit(a, c0, width, s0):
        o_ref[:, c0:c0 + width] = a.astype(o_ref.dtype)
        if s0 is not None:
            for j in range(0, width, HEAD_DIM):
                scr[j // HEAD_DIM] = a[:, j:j + HEAD_DIM]
                for r in range(DL_STREAMS):
                    rows = scr[j // HEAD_DIM, pl.ds(r, tm // DL_STREAMS, stride=DL_STREAMS), :]
                    s_ref[r, :, s0 + j:s0 + j + HEAD_DIM] = rows.astype(s_ref.dtype)

    for off, width, kind, s_off in segments:
        a = jnp.dot(h, w_ref[:, off:off + width], preferred_element_type=jnp.float32)
        if kind in ("rope", "rope_scale"):
            for j in range(0, width, HEAD_DIM):
                aj = a[:, j:j + HEAD_DIM]
                aj = aj * cos_ref[...] + pltpu.roll(aj, HEAD_DIM // 2, axis=1) * sin_ref[...]
                if kind == "rope_scale":
                    aj = aj * Q_SCALE
                emit(aj, off + j, HEAD_DIM, None if s_off is None else s_off + j)
        else:
            if kind == "scale":
                a = a * Q_SCALE
            elif kind == "silu":
                a = a * (1.0 / (1.0 + jnp.exp(-a)))
            emit(a, off, width, s_off)


def _in_proj(x, g, w_bf16, cos2, sin2, segments, batch, seq):
    n, d = x.shape
    cols = w_bf16.shape[1]
    tm = PROJ_TM
    bps = seq // tm
    streamed = any(s[3] is not None for s in segments)
    out_specs = [pl.BlockSpec((tm, cols), lambda i: (i, 0))]
    out_shape = [jax.ShapeDtypeStruct((n, cols), jnp.bfloat16)]
    scratch = []
    if streamed:
        out_specs.append(pl.BlockSpec((None, DL_STREAMS, tm // DL_STREAMS, 3 * W_DIL),
                                      lambda i: (i // bps, 0, i % bps, 0)))
        out_shape.append(jax.ShapeDtypeStruct((batch, DL_STREAMS, seq // DL_STREAMS, 3 * W_DIL), jnp.bfloat16))
        scratch.append(pltpu.VMEM((W_DIL // HEAD_DIM, tm, HEAD_DIM), jnp.float32))
    res = pl.pallas_call(
        functools.partial(_in_proj_kernel, segments=segments),
        grid=(n // tm,),
        in_specs=[
            pl.BlockSpec((tm, d), lambda i: (i, 0)),
            pl.BlockSpec((1, d), lambda i: (0, 0)),
            pl.BlockSpec((d, cols), lambda i: (0, 0), pipeline_mode=pl.Buffered(1)),
            pl.BlockSpec((tm, HEAD_DIM), lambda i: (i % bps, 0)),
            pl.BlockSpec((tm, HEAD_DIM), lambda i: (i % bps, 0)),
        ],
        out_specs=out_specs,
        out_shape=out_shape,
        scratch_shapes=scratch,
        compiler_params=pltpu.CompilerParams(
            dimension_semantics=("parallel",), vmem_limit_bytes=V7X_VMEM_LIMIT),
        name="in_proj",
    )(x, g.reshape(1, d), w_bf16, cos2, sin2)
    return res if streamed else res[0]


def _out_proj_kernel(x_ref, yna_ref, ydl_ref, ym_ref, w_ref, g_ref, o_ref, *, final_norm):
    acc = jnp.dot(yna_ref[...], w_ref[0:W_NA, :], preferred_element_type=jnp.float32)
    acc += jnp.dot(ydl_ref[...], w_ref[W_NA:W_NA + W_DIL, :], preferred_element_type=jnp.float32)
    acc += jnp.dot(ym_ref[...], w_ref[W_NA + W_DIL:, :], preferred_element_type=jnp.float32)
    x = x_ref[...] + acc
    if final_norm:
        x = _rms_norm_f32(x, g_ref[...])
    o_ref[...] = x


def _out_proj(x, y_na, y_dl, y_m, w_bf16, final_g, final_norm):
    n, d = x.shape
    tm = PROJ_TM
    return pl.pallas_call(
        functools.partial(_out_proj_kernel, final_norm=final_norm),
        grid=(n // tm,),
        in_specs=[
            pl.BlockSpec((tm, d), lambda i: (i, 0)),
            pl.BlockSpec((tm, W_NA), lambda i: (i, 0)),
            pl.BlockSpec((tm, W_DIL), lambda i: (i, 0)),
            pl.BlockSpec((tm, W_MEM), lambda i: (i, 0)),
            pl.BlockSpec((d, d), lambda i: (0, 0), pipeline_mode=pl.Buffered(1)),
            pl.BlockSpec((1, d), lambda i: (0, 0)),
        ],
        out_specs=pl.BlockSpec((tm, d), lambda i: (i, 0)),
        out_shape=jax.ShapeDtypeStruct((n, d), jnp.float32),
        compiler_params=pltpu.CompilerParams(
            dimension_semantics=("parallel",), vmem_limit_bytes=V7X_VMEM_LIMIT),
        name="out_proj",
    )(x, y_na, y_dl, y_m, w_bf16, final_g.reshape(1, d))


def _pipelined_attention(n_blocks, depth, scores_fn, pv_fn, finish_fn, prior_fn=None, stats_fn=None):
    queue = [scores_fn(i) for i in range(min(depth, n_blocks))]
    pending = None
    for i in range(n_blocks):
        s = queue.pop(0)
        if i + depth < n_blocks:
            queue.append(scores_fn(i + depth))
        m = jnp.max(s, axis=-1, keepdims=True)
        if stats_fn is not None:
            stats_fn(i, m)
        if prior_fn is None:
            p = jnp.exp2(s - m)
            l = jnp.sum(p, axis=-1, keepdims=True)
            acc = pv_fn(i, p.astype(jnp.bfloat16))
        else:
            m_b, l_b, acc_b = prior_fn(i)
            m = jnp.maximum(m, m_b)
            p = jnp.exp2(s - jnp.concatenate([m] * (s.shape[1] // m.shape[1]), axis=1))
            alpha = jnp.exp2(m_b - m)
            l = alpha * l_b + jnp.sum(p, axis=-1, keepdims=True)
            acc = alpha * acc_b + pv_fn(i, p.astype(jnp.bfloat16))
        if pending is not None:
            finish_fn(*pending)
        pending = (i, l, acc)
    finish_fn(*pending)


def _na_kernel(q_ref, k_ref, v_ref, gate_ref, toep_ref, o_ref, bias_ref):
    n_groups = q_ref.shape[0] // NA_Q
    rows = q_ref.shape[0] // GRID_W

    def key_start(g):
        return int(np.clip(g * NA_ROWS_PER_GROUP - NA_WIN_ROWS // 2, 0, rows - NA_KEY_ROWS)) * GRID_W

    @pl.when(pl.program_id(1) == 0)
    def _():
        outside = jnp.full((GRID_W, GRID_W), NEG, jnp.float32)
        for pat, g in enumerate((0, 1, n_groups - 1)):
            base = key_start(g) // GRID_W
            for qi in range(NA_ROWS_PER_GROUP):
                r = g * NA_ROWS_PER_GROUP + qi
                rs = int(np.clip(r - NA_WIN_ROWS // 2, 0, rows - NA_WIN_ROWS))
                for kj in range(0, NA_KEY_ROWS, 2):
                    pair = [toep_ref[base + kk - r + NA_WIN_ROWS - 1] if rs <= base + kk < rs + NA_WIN_ROWS
                            else outside for kk in (kj, kj + 1)]
                    bias_ref[pat, qi * GRID_W:(qi + 1) * GRID_W, kj * GRID_W:(kj + 2) * GRID_W] = (
                        jnp.concatenate(pair, axis=1))

    def scores(g):
        pat = 0 if g == 0 else (2 if g == n_groups - 1 else 1)
        k0 = key_start(g)
        return _dot_nt(q_ref[g * NA_Q:(g + 1) * NA_Q, :], k_ref[k0:k0 + NA_K, :]) + bias_ref[pat]

    def pv(g, p):
        k0 = key_start(g)
        return jnp.dot(p, v_ref[k0:k0 + NA_K, :], preferred_element_type=jnp.float32)

    def finish(g, l, acc):
        o = (acc / l) * gate_ref[g * NA_Q:(g + 1) * NA_Q, :].astype(jnp.float32)
        o_ref[g * NA_Q:(g + 1) * NA_Q, :] = o.astype(o_ref.dtype)

    _pipelined_attention(n_groups, 2, scores, pv, finish)


def _na_toeplitz(rpb):
    c = np.arange(GRID_W)
    cs = np.clip(c - NA_WIN_COLS // 2, 0, GRID_W - NA_WIN_COLS)
    col_ok = (c[None, :] >= cs[:, None]) & (c[None, :] < cs[:, None] + NA_WIN_COLS)
    dc = np.clip(c[None, :] - c[:, None], -(NA_WIN_COLS - 1), NA_WIN_COLS - 1) + NA_WIN_COLS - 1
    onehot = (dc[None] == np.arange(2 * NA_WIN_COLS - 1)[:, None, None]).astype(np.float32)
    toep = jnp.einsum("lhaj,jqk->lhaqk", rpb.astype(jnp.float32), jnp.asarray(onehot),
                      precision=lax.Precision.HIGHEST)
    return jnp.where(col_ok, toep * LOG2E, NEG)


def _na_attention(z, toep, batch, seq):
    n = z.shape[0]
    qb, kb, vb, gb = (OFF_NA_Q // HEAD_DIM, OFF_NA_K // HEAD_DIM, OFF_NA_V // HEAD_DIM, OFF_NA_G // HEAD_DIM)
    return pl.pallas_call(
        _na_kernel,
        grid=(N_HEADS_NA, batch),
        in_specs=[
            pl.BlockSpec((seq, HEAD_DIM), lambda h, b: (b, qb + h)),
            pl.BlockSpec((seq, HEAD_DIM), lambda h, b: (b, kb + h)),
            pl.BlockSpec((seq, HEAD_DIM), lambda h, b: (b, vb + h)),
            pl.BlockSpec((seq, HEAD_DIM), lambda h, b: (b, gb + h)),
            pl.BlockSpec((None,) + toep.shape[1:], lambda h, b: (h, 0, 0, 0)),
        ],
        out_specs=pl.BlockSpec((seq, HEAD_DIM), lambda h, b: (b, h)),
        out_shape=jax.ShapeDtypeStruct((n, W_NA), jnp.bfloat16),
        scratch_shapes=[pltpu.VMEM((3, NA_Q, NA_K), jnp.float32)],
        compiler_params=pltpu.CompilerParams(
            dimension_semantics=("parallel", "arbitrary"), vmem_limit_bytes=V7X_VMEM_LIMIT),
        name="na_attention",
    )(z, z, z, z, toep)


def _dl_kernel(qn_ref, kn_ref, vn_ref, gate_ref, qs_ref, ks_ref, vs_ref, bias_a_ref, bias_b_ref, o_ref,
               m_scr, l_scr, acc_scr):
    seq = qn_ref.shape[0]
    slen = seq // DL_STREAMS

    def stream_rows(r):
        return pl.ds(r, slen, stride=DL_STREAMS)

    def scores_b(r):
        return _dot_nt(qs_ref[r], ks_ref[r]) + bias_b_ref[...]

    def pv_b(r, p):
        return jnp.dot(p, vs_ref[r], preferred_element_type=jnp.float32)

    def stats_b(r, m):
        m_scr[stream_rows(r), :] = jnp.broadcast_to(m, (slen, HEAD_DIM))

    def finish_b(r, l, acc):
        l_scr[stream_rows(r), :] = jnp.broadcast_to(l, (slen, HEAD_DIM))
        acc_scr[stream_rows(r), :] = acc

    _pipelined_attention(DL_STREAMS, 1, scores_b, pv_b, finish_b, stats_fn=stats_b)

    def key_start(i):
        return int(np.clip(i * DL_Q - DL_HALF, 0, seq - DL_KA))

    def scores_a(i):
        k0 = key_start(i)
        pat = (i * DL_Q - k0) // DL_HALF
        return _dot_nt(qn_ref[i * DL_Q:(i + 1) * DL_Q, :], kn_ref[k0:k0 + DL_KA, :]) + bias_a_ref[pat]

    def pv_a(i, p):
        k0 = key_start(i)
        return jnp.dot(p, vn_ref[k0:k0 + DL_KA, :], preferred_element_type=jnp.float32)

    def prior_a(i):
        rows = slice(i * DL_Q, (i + 1) * DL_Q)
        return m_scr[rows, :], l_scr[rows, :], acc_scr[rows, :]

    def finish_a(i, l, acc):
        rows = slice(i * DL_Q, (i + 1) * DL_Q)
        o = (acc / l) * gate_ref[rows, :].astype(jnp.float32)
        o_ref[rows, :] = o.astype(o_ref.dtype)

    _pipelined_attention(seq // DL_Q, DL_DEPTH_A, scores_a, pv_a, finish_a, prior_fn=prior_a)


def _dl_bias_tables(seq):
    (w1, d1), (w4, d4), (w16, d16) = DIL_CONFIGS
    assert d1 == 1 and d4 == DL_STREAMS and d16 % d4 == 0
    assert (w1 // 2) // d1 == DL_HALF and (w4 // 2) // d4 == DL_HALF and (w16 // 2) // d16 == DL_HALF
    r = np.arange(DL_Q)[:, None]
    j = np.arange(DL_KA)[None, :]
    bias_a = np.stack([np.where(np.abs(j - off - r) <= DL_HALF, 0.0, NEG) for off in (0, DL_HALF, 2 * DL_HALF)])
    slen = seq // DL_STREAMS
    delta = np.arange(slen)[None, :] - np.arange(slen)[:, None]
    step = d16 // d4
    mult = (np.abs(delta) <= DL_HALF).astype(np.float64) + ((delta % step == 0) & (np.abs(delta) <= DL_HALF * step))
    bias_b = np.where(mult > 0, np.log2(np.maximum(mult, 1.0)), NEG)
    return jnp.asarray(bias_a, jnp.float32), jnp.asarray(bias_b, jnp.float32)


def _dl_attention(z, zs, bias_a, bias_b, batch, seq):
    n = z.shape[0]
    qb, kb, vb, gb = (OFF_DL_Q // HEAD_DIM, OFF_DL_K // HEAD_DIM, OFF_DL_V // HEAD_DIM, OFF_DL_G // HEAD_DIM)
    slen = seq // DL_STREAMS
    stream_spec = lambda c0: pl.BlockSpec((None, DL_STREAMS, slen, HEAD_DIM), lambda b, h: (b, 0, 0, c0 + h))
    return pl.pallas_call(
        _dl_kernel,
        grid=(batch, N_HEADS_DIL),
        in_specs=[
            pl.BlockSpec((seq, HEAD_DIM), lambda b, h: (b, qb + h)),
            pl.BlockSpec((seq, HEAD_DIM), lambda b, h: (b, kb + h)),
            pl.BlockSpec((seq, HEAD_DIM), lambda b, h: (b, vb + h)),
            pl.BlockSpec((seq, HEAD_DIM), lambda b, h: (b, gb + h)),
            stream_spec(0), stream_spec(N_HEADS_DIL), stream_spec(2 * N_HEADS_DIL),
            pl.BlockSpec(bias_a.shape, lambda b, h: (0, 0, 0)),
            pl.BlockSpec(bias_b.shape, lambda b, h: (0, 0)),
        ],
        out_specs=pl.BlockSpec((seq, HEAD_DIM), lambda b, h: (b, h)),
        out_shape=jax.ShapeDtypeStruct((n, W_DIL), jnp.bfloat16),
        scratch_shapes=[pltpu.VMEM((seq, HEAD_DIM), jnp.float32)] * 3,
        compiler_params=pltpu.CompilerParams(
            dimension_semantics=("parallel", "parallel"), vmem_limit_bytes=V7X_VMEM_LIMIT),
        name="dl_attention",
    )(z, z, z, z, zs, zs, zs, bias_a, bias_b)


def _mem_kernel(q_ref, k_ref, v_ref, gate_ref, o_ref):
    def scores(i):
        return _dot_nt(q_ref[i * MEM_Q:(i + 1) * MEM_Q, :], k_ref[...])

    def pv(i, p):
        return jnp.dot(p, v_ref[...], preferred_element_type=jnp.float32)

    def finish(i, l, acc):
        o = (acc / l) * gate_ref[i * MEM_Q:(i + 1) * MEM_Q, :].astype(jnp.float32)
        o_ref[i * MEM_Q:(i + 1) * MEM_Q, :] = o.astype(o_ref.dtype)

    _pipelined_attention(q_ref.shape[0] // MEM_Q, 1, scores, pv, finish)


def _mem_attention(z, mkv, layer, batch, seq):
    n = z.shape[0]
    qb, gb = OFF_M_Q // HEAD_DIM, OFF_M_G // HEAD_DIM
    kb = layer * 2 * N_HEADS_MEM
    vb = kb + N_HEADS_MEM
    return pl.pallas_call(
        _mem_kernel,
        grid=(batch, N_HEADS_MEM),
        in_specs=[
            pl.BlockSpec((seq, HEAD_DIM), lambda b, h: (b, qb + h)),
            pl.BlockSpec((N_MEM, HEAD_DIM), lambda b, h: (b, kb + h)),
            pl.BlockSpec((N_MEM, HEAD_DIM), lambda b, h: (b, vb + h)),
            pl.BlockSpec((seq, HEAD_DIM), lambda b, h: (b, gb + h)),
        ],
        out_specs=pl.BlockSpec((seq, HEAD_DIM), lambda b, h: (b, h)),
        out_shape=jax.ShapeDtypeStruct((n, W_MEM), jnp.bfloat16),
        compiler_params=pltpu.CompilerParams(dimension_semantics=("parallel", "parallel")),
        name="mem_attention",
    )(z, mkv, mkv, z)


def _rope_tables(seq):
    half = HEAD_DIM // 2
    inv = ROPE_THETA ** (-jnp.arange(half, dtype=jnp.float32) / half)
    ang = jnp.arange(seq).astype(jnp.float32)[:, None] * inv[None, :]
    cos, sin = jnp.cos(ang), jnp.sin(ang)
    return jnp.concatenate([cos, cos], axis=-1), jnp.concatenate([-sin, sin], axis=-1)


def kernel(x, mem, norm_g, w_in, na_rpb, mem_norm_g, w_mem_kv, w_out, final_g):
    batch, seq, d = x.shape
    depth = w_in.shape[0]
    assert d == D_MODEL and w_in.shape[2] == IN_COLS and seq % GRID_W == 0
    assert seq % PROJ_TM == 0 and seq % NA_Q == 0 and seq % MEM_Q == 0
    assert seq % (DL_STREAMS * DL_Q) == 0 and PROJ_TM % DL_STREAMS == 0

    cos2, sin2 = _rope_tables(seq)
    dl_bias_a, dl_bias_b = _dl_bias_tables(seq)
    na_toep = _na_toeplitz(na_rpb)
    w_in_b = w_in.astype(jnp.bfloat16)
    w_out_b = w_out.astype(jnp.bfloat16)
    w_mem_b = jnp.transpose(w_mem_kv, (1, 0, 2)).reshape(d, depth * 2 * W_MEM).astype(jnp.bfloat16)
    mkv = _in_proj(mem.reshape(batch * N_MEM, d), mem_norm_g, w_mem_b, cos2, sin2,
                   ((0, depth * 2 * W_MEM, "plain", None),), batch, N_MEM)

    xf = x.reshape(batch * seq, d)
    for l in range(depth):
        z, zs = _in_proj(xf, norm_g[l], w_in_b[l], cos2, sin2, IN_SEGMENTS, batch, seq)
        y_na = _na_attention(z, na_toep[l], batch, seq)
        y_dl = _dl_attention(z, zs, dl_bias_a, dl_bias_b, batch, seq)
        y_m = _mem_attention(z, mkv, l, batch, seq)
        xf = _out_proj(xf, y_na, y_dl, y_m, w_out_b[l], final_g, l == depth - 1)
    return xf.reshape(batch, seq, d)
```

```python
import collections
import functools

import numpy as np
import jax
import jax.numpy as jnp
from jax import lax
from jax.experimental import pallas as pl
from jax.experimental.pallas import tpu as pltpu

D_MODEL = 2048
HEAD_DIM = 128
N_HEADS_NA = 6
N_HEADS_DIL = 6
N_HEADS_MEM = 4
W_NA = N_HEADS_NA * HEAD_DIM
W_DIL = N_HEADS_DIL * HEAD_DIM
W_MEM = N_HEADS_MEM * HEAD_DIM
IN_COLS = 4 * W_NA + 4 * W_DIL + 2 * W_MEM
N_MEM = 256
GRID_W = 64
NA_WIN_ROWS = 8
NA_WIN_COLS = 16
DIL_CONFIGS = ((128, 1), (512, 4), (2048, 16))
ROPE_THETA = 10000.0
EPS = 1e-6
NEG = -1e30
SCALE = HEAD_DIM ** -0.5
LOG2E = float(np.log2(np.e))
Q_SCALE = SCALE * LOG2E

OFF_NA_Q, OFF_NA_K, OFF_NA_V, OFF_NA_G = 0, W_NA, 2 * W_NA, 3 * W_NA
OFF_DL_Q = 4 * W_NA
OFF_DL_K, OFF_DL_V, OFF_DL_G = OFF_DL_Q + W_DIL, OFF_DL_Q + 2 * W_DIL, OFF_DL_Q + 3 * W_DIL
OFF_M_Q = OFF_DL_Q + 4 * W_DIL
OFF_M_G = OFF_M_Q + W_MEM

IN_SEGMENTS = (
    (OFF_NA_Q, W_NA, "scale", None), (OFF_NA_K, W_NA, "plain", None),
    (OFF_NA_V, W_NA, "plain", None), (OFF_NA_G, W_NA, "silu", None),
    (OFF_DL_Q, W_DIL, "rope_scale", 0), (OFF_DL_K, W_DIL, "rope", W_DIL),
    (OFF_DL_V, W_DIL, "plain", 2 * W_DIL), (OFF_DL_G, W_DIL, "silu", None),
    (OFF_M_Q, W_MEM, "scale", None), (OFF_M_G, W_MEM, "silu", None),
)

V7X_VMEM_LIMIT = 56 * 1024 * 1024

PROJ_TM = 256
HEADS_PER_STEP = 2
HEAD_COLS = HEADS_PER_STEP * HEAD_DIM
NA_ROWS_PER_GROUP = 4
NA_Q = NA_ROWS_PER_GROUP * GRID_W
NA_KEY_ROWS = NA_ROWS_PER_GROUP + NA_WIN_ROWS
NA_K = NA_KEY_ROWS * GRID_W
NA_DEPTH = 2
MEM_Q = 512
MEM_DEPTH = 1

DL_STREAMS = 4
DL_HALF = 64
DL_Q = 128
DL_KA = DL_Q + 2 * DL_HALF
DL_DEPTH_A = 4
DL_DEPTH_B = 1


def _rms_norm_f32(x, g):
    return x * lax.rsqrt(jnp.mean(x * x, axis=-1, keepdims=True) + EPS) * g


def _dot_nt(a, b):
    return lax.dot_general(a, b, (((1,), (1,)), ((), ())), preferred_element_type=jnp.float32)


def _head_cols(hh):
    return slice(hh * HEAD_DIM, (hh + 1) * HEAD_DIM)


def _in_proj_kernel(x_ref, g_ref, w_ref, cos_ref, sin_ref, o_ref, *rest, segments):
    if rest:
        s_ref, scr = rest
    tm = x_ref.shape[0]
    h = _rms_norm_f32(x_ref[...], g_ref[...]).astype(jnp.bfloat16)

    def emit(a, c0, width, s0):
        o_ref[:, c0:c0 + width] = a.astype(o_ref.dtype)
        if s0 is not None:
            for j in range(0, width, HEAD_DIM):
                scr[j // HEAD_DIM] = a[:, j:j + HEAD_DIM]
                for r in range(DL_STREAMS):
                    rows = scr[j // HEAD_DIM, pl.ds(r, tm // DL_STREAMS, stride=DL_STREAMS), :]
                    s_ref[r, :, s0 + j:s0 + j + HEAD_DIM] = rows.astype(s_ref.dtype)

    for off, width, kind, s_off in segments:
        a = jnp.dot(h, w_ref[:, off:off + width], preferred_element_type=jnp.float32)
        if kind in ("rope", "rope_scale"):
            for j in range(0, width, HEAD_DIM):
                aj = a[:, j:j + HEAD_DIM]
                aj = aj * cos_ref[...] + pltpu.roll(aj, HEAD_DIM // 2, axis=1) * sin_ref[...]
                if kind == "rope_scale":
                    aj = aj * Q_SCALE
                emit(aj, off + j, HEAD_DIM, None if s_off is None else s_off + j)
        else:
            if kind == "scale":
                a = a * Q_SCALE
            elif kind == "silu":
                a = a * (1.0 / (1.0 + jnp.exp(-a)))
            emit(a, off, width, s_off)


def _in_proj(x, g, w_bf16, layer, cos2, sin2, segments, batch, seq):
    n, d = x.shape
    cols = w_bf16.shape[2]
    tm = PROJ_TM
    bps = seq // tm
    streamed = any(s[3] is not None for s in segments)
    out_specs = [pl.BlockSpec((tm, cols), lambda i: (i, 0))]
    out_shape = [jax.ShapeDtypeStruct((n, cols), jnp.bfloat16)]
    scratch = []
    if streamed:
        out_specs.append(pl.BlockSpec((None, DL_STREAMS, tm // DL_STREAMS, 3 * W_DIL),
                                      lambda i: (i // bps, 0, i % bps, 0)))
        out_shape.append(jax.ShapeDtypeStruct((batch, DL_STREAMS, seq // DL_STREAMS, 3 * W_DIL), jnp.bfloat16))
        scratch.append(pltpu.VMEM((W_DIL // HEAD_DIM, tm, HEAD_DIM), jnp.float32))
    res = pl.pallas_call(
        functools.partial(_in_proj_kernel, segments=segments),
        grid=(n // tm,),
        in_specs=[
            pl.BlockSpec((tm, d), lambda i: (i, 0)),
            pl.BlockSpec((1, d), lambda i: (0, 0)),
            pl.BlockSpec((None, d, cols), lambda i: (layer, 0, 0), pipeline_mode=pl.Buffered(1)),
            pl.BlockSpec((tm, HEAD_DIM), lambda i: (i % bps, 0)),
            pl.BlockSpec((tm, HEAD_DIM), lambda i: (i % bps, 0)),
        ],
        out_specs=out_specs,
        out_shape=out_shape,
        scratch_shapes=scratch,
        compiler_params=pltpu.CompilerParams(
            dimension_semantics=("parallel",), vmem_limit_bytes=V7X_VMEM_LIMIT),
        name="in_proj",
    )(x, g.reshape(1, d), w_bf16, cos2, sin2)
    return res if streamed else res[0]


def _out_proj_kernel(x_ref, yna_ref, ydl_ref, ym_ref, w_ref, g_ref, o_ref, *, final_norm):
    acc = jnp.dot(yna_ref[...], w_ref[0:W_NA, :], preferred_element_type=jnp.float32)
    acc += jnp.dot(ydl_ref[...], w_ref[W_NA:W_NA + W_DIL, :], preferred_element_type=jnp.float32)
    acc += jnp.dot(ym_ref[...], w_ref[W_NA + W_DIL:, :], preferred_element_type=jnp.float32)
    x = x_ref[...] + acc
    if final_norm:
        x = _rms_norm_f32(x, g_ref[...])
    o_ref[...] = x


def _out_proj(x, y_na, y_dl, y_m, w_bf16, layer, final_g, final_norm):
    n, d = x.shape
    tm = PROJ_TM
    return pl.pallas_call(
        functools.partial(_out_proj_kernel, final_norm=final_norm),
        grid=(n // tm,),
        in_specs=[
            pl.BlockSpec((tm, d), lambda i: (i, 0)),
            pl.BlockSpec((tm, W_NA), lambda i: (i, 0)),
            pl.BlockSpec((tm, W_DIL), lambda i: (i, 0)),
            pl.BlockSpec((tm, W_MEM), lambda i: (i, 0)),
            pl.BlockSpec((None, d, d), lambda i: (layer, 0, 0), pipeline_mode=pl.Buffered(1)),
            pl.BlockSpec((1, d), lambda i: (0, 0)),
        ],
        out_specs=pl.BlockSpec((tm, d), lambda i: (i, 0)),
        out_shape=jax.ShapeDtypeStruct((n, d), jnp.float32),
        compiler_params=pltpu.CompilerParams(
            dimension_semantics=("parallel",), vmem_limit_bytes=V7X_VMEM_LIMIT),
        name="out_proj",
    )(x, y_na, y_dl, y_m, w_bf16, final_g.reshape(1, d))


_Block = collections.namedtuple("_Block", "scores pv finish prior stats depth flush")


def _run_pipeline(blocks):
    issued = []
    pending = None
    for i, blk in enumerate(blocks):
        while len(issued) < min(i + 1 + blk.depth, len(blocks)):
            issued.append(blocks[len(issued)].scores())
        s = issued[i]
        issued[i] = None
        if blk.flush and pending is not None:
            pending[0].finish(*pending[1:])
            pending = None
        m = jnp.max(s, axis=-1, keepdims=True)
        if blk.stats is not None:
            blk.stats(m)
        if blk.prior is None:
            p = jnp.exp2(s - m)
            l = jnp.sum(p, axis=-1, keepdims=True)
            acc = blk.pv(p.astype(jnp.bfloat16))
        else:
            m_b, l_b, acc_b = blk.prior()
            m = jnp.maximum(m, m_b)
            p = jnp.exp2(s - jnp.concatenate([m] * (s.shape[1] // m.shape[1]), axis=1))
            alpha = jnp.exp2(m_b - m)
            l = alpha * l_b + jnp.sum(p, axis=-1, keepdims=True)
            acc = alpha * acc_b + blk.pv(p.astype(jnp.bfloat16))
        if pending is not None:
            pending[0].finish(*pending[1:])
        pending = (blk, l, acc)
    pending[0].finish(*pending[1:])


def _na_kernel(q_ref, k_ref, v_ref, gate_ref, toep_ref, o_ref, bias_ref):
    n_groups = q_ref.shape[0] // NA_Q
    rows = q_ref.shape[0] // GRID_W

    def key_start(g):
        return int(np.clip(g * NA_ROWS_PER_GROUP - NA_WIN_ROWS // 2, 0, rows - NA_KEY_ROWS)) * GRID_W

    @pl.when(pl.program_id(1) == 0)
    def _():
        outside = jnp.full((GRID_W, GRID_W), NEG, jnp.float32)
        for hh in range(HEADS_PER_STEP):
            for pat, g in enumerate((0, 1, n_groups - 1)):
                base = key_start(g) // GRID_W
                for qi in range(NA_ROWS_PER_GROUP):
                    r = g * NA_ROWS_PER_GROUP + qi
                    rs = int(np.clip(r - NA_WIN_ROWS // 2, 0, rows - NA_WIN_ROWS))
                    for kj in range(0, NA_KEY_ROWS, 2):
                        pair = [toep_ref[hh, base + kk - r + NA_WIN_ROWS - 1]
                                if rs <= base + kk < rs + NA_WIN_ROWS else outside for kk in (kj, kj + 1)]
                        bias_ref[hh, pat, qi * GRID_W:(qi + 1) * GRID_W, kj * GRID_W:(kj + 2) * GRID_W] = (
                            jnp.concatenate(pair, axis=1))

    def block(hh, g):
        cols = _head_cols(hh)
        qrows = slice(g * NA_Q, (g + 1) * NA_Q)
        krows = slice(key_start(g), key_start(g) + NA_K)
        pat = 0 if g == 0 else (2 if g == n_groups - 1 else 1)

        def scores():
            return _dot_nt(q_ref[qrows, cols], k_ref[krows, cols]) + bias_ref[hh, pat]

        def pv(p):
            return jnp.dot(p, v_ref[krows, cols], preferred_element_type=jnp.float32)

        def finish(l, acc):
            o = (acc / l) * gate_ref[qrows, cols].astype(jnp.float32)
            o_ref[qrows, cols] = o.astype(o_ref.dtype)

        return _Block(scores, pv, finish, None, None, NA_DEPTH, False)

    _run_pipeline([block(hh, g) for hh in range(HEADS_PER_STEP) for g in range(n_groups)])


def _na_toeplitz(rpb):
    c = np.arange(GRID_W)
    cs = np.clip(c - NA_WIN_COLS // 2, 0, GRID_W - NA_WIN_COLS)
    col_ok = (c[None, :] >= cs[:, None]) & (c[None, :] < cs[:, None] + NA_WIN_COLS)
    dc = np.clip(c[None, :] - c[:, None], -(NA_WIN_COLS - 1), NA_WIN_COLS - 1) + NA_WIN_COLS - 1
    onehot = (dc[None] == np.arange(2 * NA_WIN_COLS - 1)[:, None, None]).astype(np.float32)
    toep = jnp.einsum("lhaj,jqk->lhaqk", rpb.astype(jnp.float32), jnp.asarray(onehot),
                      precision=lax.Precision.HIGHEST)
    return jnp.where(col_ok, toep * LOG2E, NEG)


def _na_attention(z, toep, layer, batch, seq):
    n = z.shape[0]
    qb, kb, vb, gb = (OFF_NA_Q // HEAD_COLS, OFF_NA_K // HEAD_COLS, OFF_NA_V // HEAD_COLS, OFF_NA_G // HEAD_COLS)
    return pl.pallas_call(
        _na_kernel,
        grid=(N_HEADS_NA // HEADS_PER_STEP, batch),
        in_specs=[
            pl.BlockSpec((seq, HEAD_COLS), lambda h, b: (b, qb + h)),
            pl.BlockSpec((seq, HEAD_COLS), lambda h, b: (b, kb + h)),
            pl.BlockSpec((seq, HEAD_COLS), lambda h, b: (b, vb + h)),
            pl.BlockSpec((seq, HEAD_COLS), lambda h, b: (b, gb + h)),
            pl.BlockSpec((None, HEADS_PER_STEP) + toep.shape[2:], lambda h, b: (layer, h, 0, 0, 0)),
        ],
        out_specs=pl.BlockSpec((seq, HEAD_COLS), lambda h, b: (b, h)),
        out_shape=jax.ShapeDtypeStruct((n, W_NA), jnp.bfloat16),
        scratch_shapes=[pltpu.VMEM((HEADS_PER_STEP, 3, NA_Q, NA_K), jnp.float32)],
        compiler_params=pltpu.CompilerParams(
            dimension_semantics=("parallel", "arbitrary"), vmem_limit_bytes=V7X_VMEM_LIMIT),
        name="na_attention",
    )(z, z, z, z, toep)


def _dl_kernel(qn_ref, kn_ref, vn_ref, gate_ref, qs_ref, ks_ref, vs_ref, bias_a_ref, bias_b_ref, o_ref,
               m_scr, l_scr, acc_scr):
    seq = qn_ref.shape[0]
    slen = seq // DL_STREAMS

    def block_b(hh, r):
        cols = _head_cols(hh)
        rows = pl.ds(r, slen, stride=DL_STREAMS)

        def scores():
            return _dot_nt(qs_ref[r, :, cols], ks_ref[r, :, cols]) + bias_b_ref[...]

        def pv(p):
            return jnp.dot(p, vs_ref[r, :, cols], preferred_element_type=jnp.float32)

        def stats(m):
            m_scr[hh, rows, :] = jnp.broadcast_to(m, (slen, HEAD_DIM))

        def finish(l, acc):
            l_scr[hh, rows, :] = jnp.broadcast_to(l, (slen, HEAD_DIM))
            acc_scr[hh, rows, :] = acc

        return _Block(scores, pv, finish, None, stats, DL_DEPTH_B, False)

    def block_a(hh, i):
        cols = _head_cols(hh)
        qrows = slice(i * DL_Q, (i + 1) * DL_Q)
        k0 = int(np.clip(i * DL_Q - DL_HALF, 0, seq - DL_KA))
        krows = slice(k0, k0 + DL_KA)
        pat = (i * DL_Q - k0) // DL_HALF

        def scores():
            return _dot_nt(qn_ref[qrows, cols], kn_ref[krows, cols]) + bias_a_ref[pat]

        def pv(p):
            return jnp.dot(p, vn_ref[krows, cols], preferred_element_type=jnp.float32)

        def prior():
            return m_scr[hh, qrows, :], l_scr[hh, qrows, :], acc_scr[hh, qrows, :]

        def finish(l, acc):
            o = (acc / l) * gate_ref[qrows, cols].astype(jnp.float32)
            o_ref[qrows, cols] = o.astype(o_ref.dtype)

        return _Block(scores, pv, finish, prior, None, DL_DEPTH_A, i == 0)

    heads = range(HEADS_PER_STEP)
    _run_pipeline([block_b(hh, r) for hh in heads for r in range(DL_STREAMS)]
                  + [block_a(hh, i) for hh in heads for i in range(seq // DL_Q)])


def _dl_bias_tables(seq):
    (w1, d1), (w4, d4), (w16, d16) = DIL_CONFIGS
    assert d1 == 1 and d4 == DL_STREAMS and d16 % d4 == 0
    assert (w1 // 2) // d1 == DL_HALF and (w4 // 2) // d4 == DL_HALF and (w16 // 2) // d16 == DL_HALF
    r = np.arange(DL_Q)[:, None]
    j = np.arange(DL_KA)[None, :]
    bias_a = np.stack([np.where(np.abs(j - off - r) <= DL_HALF, 0.0, NEG) for off in (0, DL_HALF, 2 * DL_HALF)])
    slen = seq // DL_STREAMS
    delta = np.arange(slen)[None, :] - np.arange(slen)[:, None]
    step = d16 // d4
    mult = (np.abs(delta) <= DL_HALF).astype(np.float64) + ((delta % step == 0) & (np.abs(delta) <= DL_HALF * step))
    bias_b = np.where(mult > 0, np.log2(np.maximum(mult, 1.0)), NEG)
    return jnp.asarray(bias_a, jnp.float32), jnp.asarray(bias_b, jnp.float32)


def _dl_attention(z, zs, bias_a, bias_b, batch, seq):
    n = z.shape[0]
    qb, kb, vb, gb = (OFF_DL_Q // HEAD_COLS, OFF_DL_K // HEAD_COLS, OFF_DL_V // HEAD_COLS, OFF_DL_G // HEAD_COLS)
    slen = seq // DL_STREAMS
    steps = N_HEADS_DIL // HEADS_PER_STEP
    stream_spec = lambda c0: pl.BlockSpec((None, DL_STREAMS, slen, HEAD_COLS), lambda b, h: (b, 0, 0, c0 + h))
    return pl.pallas_call(
        _dl_kernel,
        grid=(batch, steps),
        in_specs=[
            pl.BlockSpec((seq, HEAD_COLS), lambda b, h: (b, qb + h)),
            pl.BlockSpec((seq, HEAD_COLS), lambda b, h: (b, kb + h)),
            pl.BlockSpec((seq, HEAD_COLS), lambda b, h: (b, vb + h)),
            pl.BlockSpec((seq, HEAD_COLS), lambda b, h: (b, gb + h)),
            stream_spec(0), stream_spec(steps), stream_spec(2 * steps),
            pl.BlockSpec(bias_a.shape, lambda b, h: (0, 0, 0)),
            pl.BlockSpec(bias_b.shape, lambda b, h: (0, 0)),
        ],
        out_specs=pl.BlockSpec((seq, HEAD_COLS), lambda b, h: (b, h)),
        out_shape=jax.ShapeDtypeStruct((n, W_DIL), jnp.bfloat16),
        scratch_shapes=[pltpu.VMEM((HEADS_PER_STEP, seq, HEAD_DIM), jnp.float32)] * 3,
        compiler_params=pltpu.CompilerParams(
            dimension_semantics=("parallel", "parallel"), vmem_limit_bytes=V7X_VMEM_LIMIT),
        name="dl_attention",
    )(z, z, z, z, zs, zs, zs, bias_a, bias_b)


def _mem_kernel(q_ref, k_ref, v_ref, gate_ref, o_ref):
    def block(hh, i):
        cols = _head_cols(hh)
        qrows = slice(i * MEM_Q, (i + 1) * MEM_Q)

        def scores():
            return _dot_nt(q_ref[qrows, cols], k_ref[:, cols])

        def pv(p):
            return jnp.dot(p, v_ref[:, cols], preferred_element_type=jnp.float32)

        def finish(l, acc):
            o = (acc / l) * gate_ref[qrows, cols].astype(jnp.float32)
            o_ref[qrows, cols] = o.astype(o_ref.dtype)

        return _Block(scores, pv, finish, None, None, MEM_DEPTH, False)

    _run_pipeline([block(hh, i) for hh in range(HEADS_PER_STEP) for i in range(q_ref.shape[0] // MEM_Q)])


def _mem_attention(z, mkv, layer, batch, seq):
    n = z.shape[0]
    qb, gb = OFF_M_Q // HEAD_COLS, OFF_M_G // HEAD_COLS
    kb = layer * 2 * W_MEM // HEAD_COLS
    vb = kb + W_MEM // HEAD_COLS
    return pl.pallas_call(
        _mem_kernel,
        grid=(batch, N_HEADS_MEM // HEADS_PER_STEP),
        in_specs=[
            pl.BlockSpec((seq, HEAD_COLS), lambda b, h: (b, qb + h)),
            pl.BlockSpec((N_MEM, HEAD_COLS), lambda b, h: (b, kb + h)),
            pl.BlockSpec((N_MEM, HEAD_COLS), lambda b, h: (b, vb + h)),
            pl.BlockSpec((seq, HEAD_COLS), lambda b, h: (b, gb + h)),
        ],
        out_specs=pl.BlockSpec((seq, HEAD_COLS), lambda b, h: (b, h)),
        out_shape=jax.ShapeDtypeStruct((n, W_MEM), jnp.bfloat16),
        compiler_params=pltpu.CompilerParams(dimension_semantics=("parallel", "parallel")),
        name="mem_attention",
    )(z, mkv, mkv, z)


def _rope_tables(seq):
    half = HEAD_DIM // 2
    inv = ROPE_THETA ** (-jnp.arange(half, dtype=jnp.float32) / half)
    ang = jnp.arange(seq).astype(jnp.float32)[:, None] * inv[None, :]
    cos, sin = jnp.cos(ang), jnp.sin(ang)
    return jnp.concatenate([cos, cos], axis=-1), jnp.concatenate([-sin, sin], axis=-1)


def kernel(x, mem, norm_g, w_in, na_rpb, mem_norm_g, w_mem_kv, w_out, final_g):
    batch, seq, d = x.shape
    depth = w_in.shape[0]
    assert d == D_MODEL and w_in.shape[2] == IN_COLS and seq % GRID_W == 0
    assert seq % PROJ_TM == 0 and seq % NA_Q == 0 and seq % MEM_Q == 0
    assert seq % (DL_STREAMS * DL_Q) == 0 and PROJ_TM % DL_STREAMS == 0
    assert all(off % HEAD_COLS == 0 for off, _, _, _ in IN_SEGMENTS) and W_MEM % HEAD_COLS == 0

    cos2, sin2 = _rope_tables(seq)
    dl_bias_a, dl_bias_b = _dl_bias_tables(seq)
    na_toep = _na_toeplitz(na_rpb)
    w_in_b = w_in.astype(jnp.bfloat16)
    w_out_b = w_out.astype(jnp.bfloat16)
    w_mem_b = jnp.transpose(w_mem_kv, (1, 0, 2)).reshape(1, d, depth * 2 * W_MEM).astype(jnp.bfloat16)
    mkv = _in_proj(mem.reshape(batch * N_MEM, d), mem_norm_g, w_mem_b, 0, cos2, sin2,
                   ((0, depth * 2 * W_MEM, "plain", None),), batch, N_MEM)

    xf = x.reshape(batch * seq, d)
    for l in range(depth):
        z, zs = _in_proj(xf, norm_g[l], w_in_b, l, cos2, sin2, IN_SEGMENTS, batch, seq)
        y_na = _na_attention(z, na_toep, l, batch, seq)
        y_dl = _dl_attention(z, zs, dl_bias_a, dl_bias_b, batch, seq)
        y_m = _mem_attention(z, mkv, l, batch, seq)
        xf = _out_proj(xf, y_na, y_dl, y_m, w_out_b, l, final_g, l == depth - 1)
    return xf.reshape(batch, seq, d)
```

```python
import collections
import functools

import numpy as np
import jax
import jax.numpy as jnp
from jax import lax
from jax.experimental import pallas as pl
from jax.experimental.pallas import tpu as pltpu

D_MODEL = 2048
HEAD_DIM = 128
N_HEADS_NA = 6
N_HEADS_DIL = 6
N_HEADS_MEM = 4
W_NA = N_HEADS_NA * HEAD_DIM
W_DIL = N_HEADS_DIL * HEAD_DIM
W_MEM = N_HEADS_MEM * HEAD_DIM
IN_COLS = 4 * W_NA + 4 * W_DIL + 2 * W_MEM
N_MEM = 256
GRID_W = 64
NA_WIN_ROWS = 8
NA_WIN_COLS = 16
DIL_CONFIGS = ((128, 1), (512, 4), (2048, 16))
ROPE_THETA = 10000.0
EPS = 1e-6
NEG = -1e30
SCALE = HEAD_DIM ** -0.5
LOG2E = float(np.log2(np.e))
Q_SCALE = SCALE * LOG2E

OFF_NA_Q, OFF_NA_K, OFF_NA_V, OFF_NA_G = 0, W_NA, 2 * W_NA, 3 * W_NA
OFF_DL_Q = 4 * W_NA
OFF_DL_K, OFF_DL_V, OFF_DL_G = OFF_DL_Q + W_DIL, OFF_DL_Q + 2 * W_DIL, OFF_DL_Q + 3 * W_DIL
OFF_M_Q = OFF_DL_Q + 4 * W_DIL
OFF_M_G = OFF_M_Q + W_MEM

IN_SEGMENTS = (
    (OFF_NA_Q, W_NA, "scale", None), (OFF_NA_K, W_NA, "plain", None),
    (OFF_NA_V, W_NA, "plain", None), (OFF_NA_G, W_NA, "silu", None),
    (OFF_DL_Q, W_DIL, "rope_scale", 0), (OFF_DL_K, W_DIL, "rope", W_DIL),
    (OFF_DL_V, W_DIL, "plain", 2 * W_DIL), (OFF_DL_G, W_DIL, "silu", None),
    (OFF_M_Q, W_MEM, "scale", None), (OFF_M_G, W_MEM, "silu", None),
)

V7X_VMEM_LIMIT = 56 * 1024 * 1024

PROJ_TM = 256
OUT_TM = 512
NA_HPS, DL_HPS, MEM_HPS = 3, 3, 4
NA_ROWS_PER_GROUP = 4
NA_Q = NA_ROWS_PER_GROUP * GRID_W
NA_KEY_ROWS = NA_ROWS_PER_GROUP + NA_WIN_ROWS
NA_K = NA_KEY_ROWS * GRID_W
NA_DEPTH = 2
MEM_Q = 512
MEM_DEPTH = 1

DL_STREAMS = 4
DL_HALF = 64
DL_Q = 128
DL_KA = DL_Q + 2 * DL_HALF
DL_DEPTH_A = 4
DL_DEPTH_B = 1


def _rms_norm_f32(x, g):
    return x * lax.rsqrt(jnp.mean(x * x, axis=-1, keepdims=True) + EPS) * g


def _dot_nt(a, b):
    return lax.dot_general(a, b, (((1,), (1,)), ((), ())), preferred_element_type=jnp.float32)


def _pv_with_rowsum(p, v):
    ones = jnp.ones((v.shape[0], HEAD_DIM), v.dtype)
    r = jnp.dot(p, jnp.concatenate([v, ones], axis=1), preferred_element_type=jnp.float32)
    return r[:, :HEAD_DIM], r[:, HEAD_DIM:]


def _head_cols(hh):
    return slice(hh * HEAD_DIM, (hh + 1) * HEAD_DIM)


def _in_proj_kernel(x_ref, g_ref, w_ref, cos_ref, sin_ref, o_ref, *rest, segments):
    if rest:
        s_ref, scr = rest
    tm = x_ref.shape[0]
    h = _rms_norm_f32(x_ref[...], g_ref[...]).astype(jnp.bfloat16)

    def emit(a, c0, width, s0):
        o_ref[:, c0:c0 + width] = a.astype(o_ref.dtype)
        if s0 is not None:
            for j in range(0, width, HEAD_DIM):
                scr[j // HEAD_DIM] = a[:, j:j + HEAD_DIM]
                for r in range(DL_STREAMS):
                    rows = scr[j // HEAD_DIM, pl.ds(r, tm // DL_STREAMS, stride=DL_STREAMS), :]
                    s_ref[r, :, s0 + j:s0 + j + HEAD_DIM] = rows.astype(s_ref.dtype)

    for off, width, kind, s_off in segments:
        a = jnp.dot(h, w_ref[:, off:off + width], preferred_element_type=jnp.float32)
        if kind in ("rope", "rope_scale"):
            for j in range(0, width, HEAD_DIM):
                aj = a[:, j:j + HEAD_DIM]
                aj = aj * cos_ref[...] + pltpu.roll(aj, HEAD_DIM // 2, axis=1) * sin_ref[...]
                if kind == "rope_scale":
                    aj = aj * Q_SCALE
                emit(aj, off + j, HEAD_DIM, None if s_off is None else s_off + j)
        else:
            if kind == "scale":
                a = a * Q_SCALE
            elif kind == "silu":
                a = a * (1.0 / (1.0 + jnp.exp(-a)))
            emit(a, off, width, s_off)


def _in_proj(x, g, w_bf16, layer, cos2, sin2, segments, batch, seq):
    n, d = x.shape
    cols = w_bf16.shape[2]
    tm = PROJ_TM
    bps = seq // tm
    streamed = any(s[3] is not None for s in segments)
    out_specs = [pl.BlockSpec((tm, cols), lambda i: (i, 0))]
    out_shape = [jax.ShapeDtypeStruct((n, cols), jnp.bfloat16)]
    scratch = []
    if streamed:
        out_specs.append(pl.BlockSpec((None, DL_STREAMS, tm // DL_STREAMS, 3 * W_DIL),
                                      lambda i: (i // bps, 0, i % bps, 0)))
        out_shape.append(jax.ShapeDtypeStruct((batch, DL_STREAMS, seq // DL_STREAMS, 3 * W_DIL), jnp.bfloat16))
        scratch.append(pltpu.VMEM((W_DIL // HEAD_DIM, tm, HEAD_DIM), jnp.float32))
    res = pl.pallas_call(
        functools.partial(_in_proj_kernel, segments=segments),
        grid=(n // tm,),
        in_specs=[
            pl.BlockSpec((tm, d), lambda i: (i, 0)),
            pl.BlockSpec((1, d), lambda i: (0, 0)),
            pl.BlockSpec((None, d, cols), lambda i: (layer, 0, 0), pipeline_mode=pl.Buffered(1)),
            pl.BlockSpec((tm, HEAD_DIM), lambda i: (i % bps, 0)),
            pl.BlockSpec((tm, HEAD_DIM), lambda i: (i % bps, 0)),
        ],
        out_specs=out_specs,
        out_shape=out_shape,
        scratch_shapes=scratch,
        compiler_params=pltpu.CompilerParams(
            dimension_semantics=("parallel",), vmem_limit_bytes=V7X_VMEM_LIMIT),
        name="in_proj",
    )(x, g.reshape(1, d), w_bf16, cos2, sin2)
    return res if streamed else res[0]


def _out_proj_kernel(x_ref, yna_ref, ydl_ref, ym_ref, w_ref, g_ref, o_ref, *, final_norm):
    acc = jnp.dot(yna_ref[...], w_ref[0:W_NA, :], preferred_element_type=jnp.float32)
    acc += jnp.dot(ydl_ref[...], w_ref[W_NA:W_NA + W_DIL, :], preferred_element_type=jnp.float32)
    acc += jnp.dot(ym_ref[...], w_ref[W_NA + W_DIL:, :], preferred_element_type=jnp.float32)
    x = x_ref[...] + acc
    if final_norm:
        x = _rms_norm_f32(x, g_ref[...])
    o_ref[...] = x


def _out_proj(x, y_na, y_dl, y_m, w_bf16, layer, final_g, final_norm):
    n, d = x.shape
    tm = OUT_TM
    return pl.pallas_call(
        functools.partial(_out_proj_kernel, final_norm=final_norm),
        grid=(n // tm,),
        in_specs=[
            pl.BlockSpec((tm, d), lambda i: (i, 0)),
            pl.BlockSpec((tm, W_NA), lambda i: (i, 0)),
            pl.BlockSpec((tm, W_DIL), lambda i: (i, 0)),
            pl.BlockSpec((tm, W_MEM), lambda i: (i, 0)),
            pl.BlockSpec((None, d, d), lambda i: (layer, 0, 0), pipeline_mode=pl.Buffered(1)),
            pl.BlockSpec((1, d), lambda i: (0, 0)),
        ],
        out_specs=pl.BlockSpec((tm, d), lambda i: (i, 0)),
        out_shape=jax.ShapeDtypeStruct((n, d), jnp.float32),
        compiler_params=pltpu.CompilerParams(
            dimension_semantics=("parallel",), vmem_limit_bytes=V7X_VMEM_LIMIT),
        name="out_proj",
    )(x, y_na, y_dl, y_m, w_bf16, final_g.reshape(1, d))


_Block = collections.namedtuple("_Block", "scores pv finish prior stats depth flush")


def _run_pipeline(blocks):
    issued = []
    pending = None
    for i, blk in enumerate(blocks):
        while len(issued) < min(i + 1 + blk.depth, len(blocks)):
            issued.append(blocks[len(issued)].scores())
        s = issued[i]
        issued[i] = None
        if blk.flush and pending is not None:
            pending[0].finish(*pending[1:])
            pending = None
        m = jnp.max(s, axis=-1, keepdims=True)
        if blk.stats is not None:
            blk.stats(m)
        if blk.prior is None:
            p = jnp.exp2(s - m)
            acc, l = blk.pv(p.astype(jnp.bfloat16))
        else:
            m_b, l_b, acc_b = blk.prior()
            m = jnp.maximum(m, m_b)
            p = jnp.exp2(s - jnp.concatenate([m] * (s.shape[1] // m.shape[1]), axis=1))
            alpha = jnp.exp2(m_b - m)
            acc, l = blk.pv(p.astype(jnp.bfloat16))
            l = alpha * l_b + l
            acc = alpha * acc_b + acc
        if pending is not None:
            pending[0].finish(*pending[1:])
        pending = (blk, l, acc)
    pending[0].finish(*pending[1:])


def _na_kernel(q_ref, k_ref, v_ref, gate_ref, toep_ref, o_ref, bias_ref):
    n_groups = q_ref.shape[0] // NA_Q
    rows = q_ref.shape[0] // GRID_W

    def key_start(g):
        return int(np.clip(g * NA_ROWS_PER_GROUP - NA_WIN_ROWS // 2, 0, rows - NA_KEY_ROWS)) * GRID_W

    @pl.when(pl.program_id(1) == 0)
    def _():
        outside = jnp.full((GRID_W, GRID_W), NEG, jnp.float32)
        for hh in range(q_ref.shape[1] // HEAD_DIM):
            for pat, g in enumerate((0, 1, n_groups - 1)):
                base = key_start(g) // GRID_W
                for qi in range(NA_ROWS_PER_GROUP):
                    r = g * NA_ROWS_PER_GROUP + qi
                    rs = int(np.clip(r - NA_WIN_ROWS // 2, 0, rows - NA_WIN_ROWS))
                    for kj in range(0, NA_KEY_ROWS, 2):
                        pair = [toep_ref[hh, base + kk - r + NA_WIN_ROWS - 1]
                                if rs <= base + kk < rs + NA_WIN_ROWS else outside for kk in (kj, kj + 1)]
                        bias_ref[hh, pat, qi * GRID_W:(qi + 1) * GRID_W, kj * GRID_W:(kj + 2) * GRID_W] = (
                            jnp.concatenate(pair, axis=1))

    def block(hh, g):
        cols = _head_cols(hh)
        qrows = slice(g * NA_Q, (g + 1) * NA_Q)
        krows = slice(key_start(g), key_start(g) + NA_K)
        pat = 0 if g == 0 else (2 if g == n_groups - 1 else 1)

        def scores():
            return _dot_nt(q_ref[qrows, cols], k_ref[krows, cols]) + bias_ref[hh, pat]

        def pv(p):
            return _pv_with_rowsum(p, v_ref[krows, cols])

        def finish(l, acc):
            o = (acc / l) * gate_ref[qrows, cols].astype(jnp.float32)
            o_ref[qrows, cols] = o.astype(o_ref.dtype)

        return _Block(scores, pv, finish, None, None, NA_DEPTH, False)

    _run_pipeline([block(hh, g) for hh in range(q_ref.shape[1] // HEAD_DIM) for g in range(n_groups)])


def _na_toeplitz(rpb):
    c = np.arange(GRID_W)
    cs = np.clip(c - NA_WIN_COLS // 2, 0, GRID_W - NA_WIN_COLS)
    col_ok = (c[None, :] >= cs[:, None]) & (c[None, :] < cs[:, None] + NA_WIN_COLS)
    dc = np.clip(c[None, :] - c[:, None], -(NA_WIN_COLS - 1), NA_WIN_COLS - 1) + NA_WIN_COLS - 1
    onehot = (dc[None] == np.arange(2 * NA_WIN_COLS - 1)[:, None, None]).astype(np.float32)
    toep = jnp.einsum("lhaj,jqk->lhaqk", rpb.astype(jnp.float32), jnp.asarray(onehot),
                      precision=lax.Precision.HIGHEST)
    return jnp.where(col_ok, toep * LOG2E, NEG)


def _na_attention(z, toep, layer, batch, seq):
    n = z.shape[0]
    hc = NA_HPS * HEAD_DIM
    qb, kb, vb, gb = (OFF_NA_Q // hc, OFF_NA_K // hc, OFF_NA_V // hc, OFF_NA_G // hc)
    return pl.pallas_call(
        _na_kernel,
        grid=(N_HEADS_NA // NA_HPS, batch),
        in_specs=[
            pl.BlockSpec((seq, hc), lambda h, b: (b, qb + h)),
            pl.BlockSpec((seq, hc), lambda h, b: (b, kb + h)),
            pl.BlockSpec((seq, hc), lambda h, b: (b, vb + h)),
            pl.BlockSpec((seq, hc), lambda h, b: (b, gb + h)),
            pl.BlockSpec((None, NA_HPS) + toep.shape[2:], lambda h, b: (layer, h, 0, 0, 0)),
        ],
        out_specs=pl.BlockSpec((seq, hc), lambda h, b: (b, h)),
        out_shape=jax.ShapeDtypeStruct((n, W_NA), jnp.bfloat16),
        scratch_shapes=[pltpu.VMEM((NA_HPS, 3, NA_Q, NA_K), jnp.float32)],
        compiler_params=pltpu.CompilerParams(
            dimension_semantics=("parallel", "arbitrary"), vmem_limit_bytes=V7X_VMEM_LIMIT),
        name="na_attention",
    )(z, z, z, z, toep)


def _dl_kernel(qn_ref, kn_ref, vn_ref, gate_ref, qs_ref, ks_ref, vs_ref, bias_a_ref, bias_b_ref, o_ref,
               m_scr, l_scr, acc_scr):
    seq = qn_ref.shape[0]
    slen = seq // DL_STREAMS

    def block_b(hh, r):
        cols = _head_cols(hh)
        rows = pl.ds(r, slen, stride=DL_STREAMS)

        def scores():
            return _dot_nt(qs_ref[r, :, cols], ks_ref[r, :, cols]) + bias_b_ref[...]

        def pv(p):
            return _pv_with_rowsum(p, vs_ref[r, :, cols])

        def stats(m):
            m_scr[hh, rows, :] = jnp.broadcast_to(m, (slen, HEAD_DIM))

        def finish(l, acc):
            l_scr[hh, rows, :] = l
            acc_scr[hh, rows, :] = acc

        return _Block(scores, pv, finish, None, stats, DL_DEPTH_B, False)

    def block_a(hh, i):
        cols = _head_cols(hh)
        qrows = slice(i * DL_Q, (i + 1) * DL_Q)
        k0 = int(np.clip(i * DL_Q - DL_HALF, 0, seq - DL_KA))
        krows = slice(k0, k0 + DL_KA)
        pat = (i * DL_Q - k0) // DL_HALF

        def scores():
            return _dot_nt(qn_ref[qrows, cols], kn_ref[krows, cols]) + bias_a_ref[pat]

        def pv(p):
            return _pv_with_rowsum(p, vn_ref[krows, cols])

        def prior():
            return m_scr[hh, qrows, :], l_scr[hh, qrows, :], acc_scr[hh, qrows, :]

        def finish(l, acc):
            o = (acc / l) * gate_ref[qrows, cols].astype(jnp.float32)
            o_ref[qrows, cols] = o.astype(o_ref.dtype)

        return _Block(scores, pv, finish, prior, None, DL_DEPTH_A, i == 0)

    heads = range(qn_ref.shape[1] // HEAD_DIM)
    _run_pipeline([block_b(hh, r) for hh in heads for r in range(DL_STREAMS)]
                  + [block_a(hh, i) for hh in heads for i in range(seq // DL_Q)])


def _dl_bias_tables(seq):
    (w1, d1), (w4, d4), (w16, d16) = DIL_CONFIGS
    assert d1 == 1 and d4 == DL_STREAMS and d16 % d4 == 0
    assert (w1 // 2) // d1 == DL_HALF and (w4 // 2) // d4 == DL_HALF and (w16 // 2) // d16 == DL_HALF
    r = np.arange(DL_Q)[:, None]
    j = np.arange(DL_KA)[None, :]
    bias_a = np.stack([np.where(np.abs(j - off - r) <= DL_HALF, 0.0, NEG) for off in (0, DL_HALF, 2 * DL_HALF)])
    slen = seq // DL_STREAMS
    delta = np.arange(slen)[None, :] - np.arange(slen)[:, None]
    step = d16 // d4
    mult = (np.abs(delta) <= DL_HALF).astype(np.float64) + ((delta % step == 0) & (np.abs(delta) <= DL_HALF * step))
    bias_b = np.where(mult > 0, np.log2(np.maximum(mult, 1.0)), NEG)
    return jnp.asarray(bias_a, jnp.float32), jnp.asarray(bias_b, jnp.float32)


def _dl_attention(z, zs, bias_a, bias_b, batch, seq):
    n = z.shape[0]
    hc = DL_HPS * HEAD_DIM
    qb, kb, vb, gb = (OFF_DL_Q // hc, OFF_DL_K // hc, OFF_DL_V // hc, OFF_DL_G // hc)
    slen = seq // DL_STREAMS
    steps = N_HEADS_DIL // DL_HPS
    stream_spec = lambda c0: pl.BlockSpec((None, DL_STREAMS, slen, hc), lambda b, h: (b, 0, 0, c0 + h))
    return pl.pallas_call(
        _dl_kernel,
        grid=(batch, steps),
        in_specs=[
            pl.BlockSpec((seq, hc), lambda b, h: (b, qb + h)),
            pl.BlockSpec((seq, hc), lambda b, h: (b, kb + h)),
            pl.BlockSpec((seq, hc), lambda b, h: (b, vb + h)),
            pl.BlockSpec((seq, hc), lambda b, h: (b, gb + h)),
            stream_spec(0), stream_spec(steps), stream_spec(2 * steps),
            pl.BlockSpec(bias_a.shape, lambda b, h: (0, 0, 0)),
            pl.BlockSpec(bias_b.shape, lambda b, h: (0, 0)),
        ],
        out_specs=pl.BlockSpec((seq, hc), lambda b, h: (b, h)),
        out_shape=jax.ShapeDtypeStruct((n, W_DIL), jnp.bfloat16),
        scratch_shapes=[pltpu.VMEM((DL_HPS, seq, HEAD_DIM), jnp.float32)] * 3,
        compiler_params=pltpu.CompilerParams(
            dimension_semantics=("parallel", "parallel"), vmem_limit_bytes=V7X_VMEM_LIMIT),
        name="dl_attention",
    )(z, z, z, z, zs, zs, zs, bias_a, bias_b)


def _mem_kernel(q_ref, k_ref, v_ref, gate_ref, o_ref):
    def block(hh, i):
        cols = _head_cols(hh)
        qrows = slice(i * MEM_Q, (i + 1) * MEM_Q)

        def scores():
            return _dot_nt(q_ref[qrows, cols], k_ref[:, cols])

        def pv(p):
            return _pv_with_rowsum(p, v_ref[:, cols])

        def finish(l, acc):
            o = (acc / l) * gate_ref[qrows, cols].astype(jnp.float32)
            o_ref[qrows, cols] = o.astype(o_ref.dtype)

        return _Block(scores, pv, finish, None, None, MEM_DEPTH, False)

    heads = range(q_ref.shape[1] // HEAD_DIM)
    _run_pipeline([block(hh, i) for hh in heads for i in range(q_ref.shape[0] // MEM_Q)])


def _mem_attention(z, mkv, layer, batch, seq):
    n = z.shape[0]
    hc = MEM_HPS * HEAD_DIM
    qb, gb = OFF_M_Q // hc, OFF_M_G // hc
    kb = layer * 2 * W_MEM // hc
    vb = kb + W_MEM // hc
    return pl.pallas_call(
        _mem_kernel,
        grid=(batch, N_HEADS_MEM // MEM_HPS),
        in_specs=[
            pl.BlockSpec((seq, hc), lambda b, h: (b, qb + h)),
            pl.BlockSpec((N_MEM, hc), lambda b, h: (b, kb + h)),
            pl.BlockSpec((N_MEM, hc), lambda b, h: (b, vb + h)),
            pl.BlockSpec((seq, hc), lambda b, h: (b, gb + h)),
        ],
        out_specs=pl.BlockSpec((seq, hc), lambda b, h: (b, h)),
        out_shape=jax.ShapeDtypeStruct((n, W_MEM), jnp.bfloat16),
        compiler_params=pltpu.CompilerParams(dimension_semantics=("parallel", "parallel")),
        name="mem_attention",
    )(z, mkv, mkv, z)


def _rope_tables(seq):
    half = HEAD_DIM // 2
    inv = ROPE_THETA ** (-jnp.arange(half, dtype=jnp.float32) / half)
    ang = jnp.arange(seq).astype(jnp.float32)[:, None] * inv[None, :]
    cos, sin = jnp.cos(ang), jnp.sin(ang)
    return jnp.concatenate([cos, cos], axis=-1), jnp.concatenate([-sin, sin], axis=-1)


def kernel(x, mem, norm_g, w_in, na_rpb, mem_norm_g, w_mem_kv, w_out, final_g):
    batch, seq, d = x.shape
    depth = w_in.shape[0]
    assert d == D_MODEL and w_in.shape[2] == IN_COLS and seq % GRID_W == 0
    assert seq % PROJ_TM == 0 and seq % OUT_TM == 0 and seq % NA_Q == 0 and seq % MEM_Q == 0
    assert seq % (DL_STREAMS * DL_Q) == 0 and PROJ_TM % DL_STREAMS == 0
    assert all(off % (NA_HPS * HEAD_DIM) == 0 for off in (OFF_NA_Q, OFF_NA_K, OFF_NA_V, OFF_NA_G))
    assert all(off % (DL_HPS * HEAD_DIM) == 0 for off in (OFF_DL_Q, OFF_DL_K, OFF_DL_V, OFF_DL_G))
    assert all(off % (MEM_HPS * HEAD_DIM) == 0 for off in (OFF_M_Q, OFF_M_G, W_MEM))
    assert N_HEADS_NA % NA_HPS == 0 and N_HEADS_DIL % DL_HPS == 0 and N_HEADS_MEM % MEM_HPS == 0

    cos2, sin2 = _rope_tables(seq)
    dl_bias_a, dl_bias_b = _dl_bias_tables(seq)
    na_toep = _na_toeplitz(na_rpb)
    w_in_b = w_in.astype(jnp.bfloat16)
    w_out_b = w_out.astype(jnp.bfloat16)
    w_mem_b = jnp.transpose(w_mem_kv, (1, 0, 2)).reshape(1, d, depth * 2 * W_MEM).astype(jnp.bfloat16)
    mkv = _in_proj(mem.reshape(batch * N_MEM, d), mem_norm_g, w_mem_b, 0, cos2, sin2,
                   ((0, depth * 2 * W_MEM, "plain", None),), batch, N_MEM)

    xf = x.reshape(batch * seq, d)
    for l in range(depth):
        z, zs = _in_proj(xf, norm_g[l], w_in_b, l, cos2, sin2, IN_SEGMENTS, batch, seq)
        y_na = _na_attention(z, na_toep, l, batch, seq)
        y_dl = _dl_attention(z, zs, dl_bias_a, dl_bias_b, batch, seq)
        y_m = _mem_attention(z, mkv, l, batch, seq)
        xf = _out_proj(xf, y_na, y_dl, y_m, w_out_b, l, final_g, l == depth - 1)
    return xf.reshape(batch, seq, d)
```

```python
import collections
import functools

import numpy as np
import jax
import jax.numpy as jnp
from jax import lax
from jax.experimental import pallas as pl
from jax.experimental.pallas import tpu as pltpu

D_MODEL = 2048
HEAD_DIM = 128
N_HEADS_NA = 6
N_HEADS_DIL = 6
N_HEADS_MEM = 4
W_NA = N_HEADS_NA * HEAD_DIM
W_DIL = N_HEADS_DIL * HEAD_DIM
W_MEM = N_HEADS_MEM * HEAD_DIM
IN_COLS = 4 * W_NA + 4 * W_DIL + 2 * W_MEM
N_MEM = 256
GRID_W = 64
NA_WIN_ROWS = 8
NA_WIN_COLS = 16
DIL_CONFIGS = ((128, 1), (512, 4), (2048, 16))
ROPE_THETA = 10000.0
EPS = 1e-6
NEG = -1e30
SCALE = HEAD_DIM ** -0.5
LOG2E = float(np.log2(np.e))
Q_SCALE = SCALE * LOG2E

OFF_NA_Q, OFF_NA_K, OFF_NA_V, OFF_NA_G = 0, W_NA, 2 * W_NA, 3 * W_NA
OFF_DL_Q = 4 * W_NA
OFF_DL_K, OFF_DL_V, OFF_DL_G = OFF_DL_Q + W_DIL, OFF_DL_Q + 2 * W_DIL, OFF_DL_Q + 3 * W_DIL
OFF_M_Q = OFF_DL_Q + 4 * W_DIL
OFF_M_G = OFF_M_Q + W_MEM

IN_SEGMENTS = (
    (OFF_NA_Q, W_NA, "scale", None), (OFF_NA_K, W_NA, "plain", None),
    (OFF_NA_V, W_NA, "plain", None), (OFF_NA_G, W_NA, "silu", None),
    (OFF_DL_Q, W_DIL, "rope_scale", 0), (OFF_DL_K, W_DIL, "rope", W_DIL),
    (OFF_DL_V, W_DIL, "plain", 2 * W_DIL), (OFF_DL_G, W_DIL, "silu", None),
    (OFF_M_Q, W_MEM, "scale", None), (OFF_M_G, W_MEM, "silu", None),
)

V7X_VMEM_LIMIT = 62 * 1024 * 1024

PROJ_TM = 512
OUT_TM = 512
NA_HPS, DL_HPS, MEM_HPS = 3, 3, 4
NA_ROWS_PER_GROUP = 4
NA_Q = NA_ROWS_PER_GROUP * GRID_W
NA_KEY_ROWS = NA_ROWS_PER_GROUP + NA_WIN_ROWS
NA_K = NA_KEY_ROWS * GRID_W
NA_DEPTH = 2
MEM_Q = 512
MEM_DEPTH = 1

DL_STREAMS = 4
DL_HALF = 64
DL_Q = 128
DL_KA = DL_Q + 2 * DL_HALF
DL_DEPTH_A = 4
DL_DEPTH_B = 1


def _rms_norm_f32(x, g):
    return x * lax.rsqrt(jnp.mean(x * x, axis=-1, keepdims=True) + EPS) * g


def _dot_nt(a, b):
    return lax.dot_general(a, b, (((1,), (1,)), ((), ())), preferred_element_type=jnp.float32)


def _pv_with_rowsum(p, v):
    ones = jnp.ones((v.shape[0], HEAD_DIM), v.dtype)
    r = jnp.dot(p, jnp.concatenate([v, ones], axis=1), preferred_element_type=jnp.float32)
    return r[:, :HEAD_DIM], r[:, HEAD_DIM:]


def _head_cols(hh):
    return slice(hh * HEAD_DIM, (hh + 1) * HEAD_DIM)


def _in_proj_kernel(x_ref, g_ref, w_ref, cos_ref, sin_ref, o_ref, *rest, segments):
    if rest:
        s_ref, scr = rest
    tm = x_ref.shape[0]
    h = _rms_norm_f32(x_ref[...], g_ref[...]).astype(jnp.bfloat16)

    def emit(a, c0, width, s0):
        o_ref[:, c0:c0 + width] = a.astype(o_ref.dtype)
        if s0 is not None:
            for j in range(0, width, HEAD_DIM):
                scr[j // HEAD_DIM] = a[:, j:j + HEAD_DIM]
                for r in range(DL_STREAMS):
                    rows = scr[j // HEAD_DIM, pl.ds(r, tm // DL_STREAMS, stride=DL_STREAMS), :]
                    s_ref[r, :, s0 + j:s0 + j + HEAD_DIM] = rows.astype(s_ref.dtype)

    for off, width, kind, s_off in segments:
        a = jnp.dot(h, w_ref[:, off:off + width], preferred_element_type=jnp.float32)
        if kind in ("rope", "rope_scale"):
            for j in range(0, width, HEAD_DIM):
                aj = a[:, j:j + HEAD_DIM]
                aj = aj * cos_ref[...] + pltpu.roll(aj, HEAD_DIM // 2, axis=1) * sin_ref[...]
                if kind == "rope_scale":
                    aj = aj * Q_SCALE
                emit(aj, off + j, HEAD_DIM, None if s_off is None else s_off + j)
        else:
            if kind == "scale":
                a = a * Q_SCALE
            elif kind == "silu":
                a = a * (1.0 / (1.0 + jnp.exp(-a)))
            emit(a, off, width, s_off)


def _in_proj(x, g, w_bf16, layer, cos2, sin2, segments, batch, seq):
    n, d = x.shape
    cols = w_bf16.shape[2]
    tm = PROJ_TM
    bps = max(seq // tm, 1)
    streamed = any(s[3] is not None for s in segments)
    out_specs = [pl.BlockSpec((tm, cols), lambda i: (i, 0))]
    out_shape = [jax.ShapeDtypeStruct((n, cols), jnp.bfloat16)]
    scratch = []
    if streamed:
        out_specs.append(pl.BlockSpec((None, DL_STREAMS, tm // DL_STREAMS, 3 * W_DIL),
                                      lambda i: (i // bps, 0, i % bps, 0)))
        out_shape.append(jax.ShapeDtypeStruct((batch, DL_STREAMS, seq // DL_STREAMS, 3 * W_DIL), jnp.bfloat16))
        scratch.append(pltpu.VMEM((W_DIL // HEAD_DIM, tm, HEAD_DIM), jnp.float32))
    res = pl.pallas_call(
        functools.partial(_in_proj_kernel, segments=segments),
        grid=(n // tm,),
        in_specs=[
            pl.BlockSpec((tm, d), lambda i: (i, 0)),
            pl.BlockSpec((1, d), lambda i: (0, 0)),
            pl.BlockSpec((None, d, cols), lambda i: (layer, 0, 0), pipeline_mode=pl.Buffered(1)),
            pl.BlockSpec((tm, HEAD_DIM), lambda i: (i % bps, 0)),
            pl.BlockSpec((tm, HEAD_DIM), lambda i: (i % bps, 0)),
        ],
        out_specs=out_specs,
        out_shape=out_shape,
        scratch_shapes=scratch,
        compiler_params=pltpu.CompilerParams(
            dimension_semantics=("parallel",), vmem_limit_bytes=V7X_VMEM_LIMIT),
        name="in_proj",
    )(x, g.reshape(1, d), w_bf16, cos2, sin2)
    return res if streamed else res[0]


def _out_proj_kernel(x_ref, yna_ref, ydl_ref, ym_ref, w_ref, g_ref, o_ref, *, final_norm):
    acc = jnp.dot(yna_ref[...], w_ref[0:W_NA, :], preferred_element_type=jnp.float32)
    acc += jnp.dot(ydl_ref[...], w_ref[W_NA:W_NA + W_DIL, :], preferred_element_type=jnp.float32)
    acc += jnp.dot(ym_ref[...], w_ref[W_NA + W_DIL:, :], preferred_element_type=jnp.float32)
    x = x_ref[...] + acc
    if final_norm:
        x = _rms_norm_f32(x, g_ref[...])
    o_ref[...] = x


def _out_proj(x, y_na, y_dl, y_m, w_bf16, layer, final_g, final_norm):
    n, d = x.shape
    tm = OUT_TM
    return pl.pallas_call(
        functools.partial(_out_proj_kernel, final_norm=final_norm),
        grid=(n // tm,),
        in_specs=[
            pl.BlockSpec((tm, d), lambda i: (i, 0)),
            pl.BlockSpec((tm, W_NA), lambda i: (i, 0)),
            pl.BlockSpec((tm, W_DIL), lambda i: (i, 0)),
            pl.BlockSpec((tm, W_MEM), lambda i: (i, 0)),
            pl.BlockSpec((None, d, d), lambda i: (layer, 0, 0), pipeline_mode=pl.Buffered(1)),
            pl.BlockSpec((1, d), lambda i: (0, 0)),
        ],
        out_specs=pl.BlockSpec((tm, d), lambda i: (i, 0)),
        out_shape=jax.ShapeDtypeStruct((n, d), jnp.float32),
        compiler_params=pltpu.CompilerParams(
            dimension_semantics=("parallel",), vmem_limit_bytes=V7X_VMEM_LIMIT),
        name="out_proj",
    )(x, y_na, y_dl, y_m, w_bf16, final_g.reshape(1, d))


_Block = collections.namedtuple("_Block", "scores pv finish prior stats depth flush")


def _run_pipeline(blocks):
    issued = []
    pending = None
    for i, blk in enumerate(blocks):
        while len(issued) < min(i + 1 + blk.depth, len(blocks)):
            issued.append(blocks[len(issued)].scores())
        s = issued[i]
        issued[i] = None
        if blk.flush and pending is not None:
            pending[0].finish(*pending[1:])
            pending = None
        m = jnp.max(s, axis=-1, keepdims=True)
        if blk.stats is not None:
            blk.stats(m)
        if blk.prior is None:
            p = jnp.exp2(s - m)
            acc, l = blk.pv(p.astype(jnp.bfloat16))
        else:
            m_b, l_b, acc_b = blk.prior()
            m = jnp.maximum(m, m_b)
            p = jnp.exp2(s - jnp.concatenate([m] * (s.shape[1] // m.shape[1]), axis=1))
            alpha = jnp.exp2(m_b - m)
            acc, l = blk.pv(p.astype(jnp.bfloat16))
            l = alpha * l_b + l
            acc = alpha * acc_b + acc
        if pending is not None:
            pending[0].finish(*pending[1:])
        pending = (blk, l, acc)
    pending[0].finish(*pending[1:])


def _na_kernel(q_ref, k_ref, v_ref, gate_ref, toep_ref, o_ref, bias_ref):
    n_groups = q_ref.shape[0] // NA_Q
    rows = q_ref.shape[0] // GRID_W

    def key_start(g):
        return int(np.clip(g * NA_ROWS_PER_GROUP - NA_WIN_ROWS // 2, 0, rows - NA_KEY_ROWS)) * GRID_W

    @pl.when(pl.program_id(1) == 0)
    def _():
        outside = jnp.full((GRID_W, GRID_W), NEG, jnp.float32)
        for hh in range(q_ref.shape[1] // HEAD_DIM):
            for pat, g in enumerate((0, 1, n_groups - 1)):
                base = key_start(g) // GRID_W
                for qi in range(NA_ROWS_PER_GROUP):
                    r = g * NA_ROWS_PER_GROUP + qi
                    rs = int(np.clip(r - NA_WIN_ROWS // 2, 0, rows - NA_WIN_ROWS))
                    for kj in range(0, NA_KEY_ROWS, 2):
                        pair = [toep_ref[hh, base + kk - r + NA_WIN_ROWS - 1]
                                if rs <= base + kk < rs + NA_WIN_ROWS else outside for kk in (kj, kj + 1)]
                        bias_ref[hh, pat, qi * GRID_W:(qi + 1) * GRID_W, kj * GRID_W:(kj + 2) * GRID_W] = (
                            jnp.concatenate(pair, axis=1))

    def block(hh, g):
        cols = _head_cols(hh)
        qrows = slice(g * NA_Q, (g + 1) * NA_Q)
        krows = slice(key_start(g), key_start(g) + NA_K)
        pat = 0 if g == 0 else (2 if g == n_groups - 1 else 1)

        def scores():
            return _dot_nt(q_ref[qrows, cols], k_ref[krows, cols]) + bias_ref[hh, pat]

        def pv(p):
            return _pv_with_rowsum(p, v_ref[krows, cols])

        def finish(l, acc):
            o = (acc / l) * gate_ref[qrows, cols].astype(jnp.float32)
            o_ref[qrows, cols] = o.astype(o_ref.dtype)

        return _Block(scores, pv, finish, None, None, NA_DEPTH, False)

    _run_pipeline([block(hh, g) for hh in range(q_ref.shape[1] // HEAD_DIM) for g in range(n_groups)])


def _na_toeplitz(rpb):
    c = np.arange(GRID_W)
    cs = np.clip(c - NA_WIN_COLS // 2, 0, GRID_W - NA_WIN_COLS)
    col_ok = (c[None, :] >= cs[:, None]) & (c[None, :] < cs[:, None] + NA_WIN_COLS)
    dc = np.clip(c[None, :] - c[:, None], -(NA_WIN_COLS - 1), NA_WIN_COLS - 1) + NA_WIN_COLS - 1
    onehot = (dc[None] == np.arange(2 * NA_WIN_COLS - 1)[:, None, None]).astype(np.float32)
    toep = jnp.einsum("lhaj,jqk->lhaqk", rpb.astype(jnp.float32), jnp.asarray(onehot),
                      precision=lax.Precision.HIGHEST)
    return jnp.where(col_ok, toep * LOG2E, NEG)


def _na_attention(z, toep, layer, batch, seq):
    n = z.shape[0]
    hc = NA_HPS * HEAD_DIM
    qb, kb, vb, gb = (OFF_NA_Q // hc, OFF_NA_K // hc, OFF_NA_V // hc, OFF_NA_G // hc)
    return pl.pallas_call(
        _na_kernel,
        grid=(N_HEADS_NA // NA_HPS, batch),
        in_specs=[
            pl.BlockSpec((seq, hc), lambda h, b: (b, qb + h)),
            pl.BlockSpec((seq, hc), lambda h, b: (b, kb + h)),
            pl.BlockSpec((seq, hc), lambda h, b: (b, vb + h)),
            pl.BlockSpec((seq, hc), lambda h, b: (b, gb + h)),
            pl.BlockSpec((None, NA_HPS) + toep.shape[2:], lambda h, b: (layer, h, 0, 0, 0)),
        ],
        out_specs=pl.BlockSpec((seq, hc), lambda h, b: (b, h)),
        out_shape=jax.ShapeDtypeStruct((n, W_NA), jnp.bfloat16),
        scratch_shapes=[pltpu.VMEM((NA_HPS, 3, NA_Q, NA_K), jnp.float32)],
        compiler_params=pltpu.CompilerParams(
            dimension_semantics=("parallel", "arbitrary"), vmem_limit_bytes=V7X_VMEM_LIMIT),
        name="na_attention",
    )(z, z, z, z, toep)


def _dl_kernel(qn_ref, kn_ref, vn_ref, gate_ref, qs_ref, ks_ref, vs_ref, bias_a_ref, bias_b_ref, o_ref,
               m_scr, l_scr, acc_scr):
    seq = qn_ref.shape[0]
    slen = seq // DL_STREAMS

    def block_b(hh, r):
        cols = _head_cols(hh)
        rows = pl.ds(r, slen, stride=DL_STREAMS)

        def scores():
            return _dot_nt(qs_ref[r, :, cols], ks_ref[r, :, cols]) + bias_b_ref[...]

        def pv(p):
            return _pv_with_rowsum(p, vs_ref[r, :, cols])

        def stats(m):
            m_scr[hh, rows, :] = jnp.broadcast_to(m, (slen, HEAD_DIM))

        def finish(l, acc):
            l_scr[hh, rows, :] = l
            acc_scr[hh, rows, :] = acc

        return _Block(scores, pv, finish, None, stats, DL_DEPTH_B, False)

    def block_a(hh, i):
        cols = _head_cols(hh)
        qrows = slice(i * DL_Q, (i + 1) * DL_Q)
        k0 = int(np.clip(i * DL_Q - DL_HALF, 0, seq - DL_KA))
        krows = slice(k0, k0 + DL_KA)
        pat = (i * DL_Q - k0) // DL_HALF

        def scores():
            return _dot_nt(qn_ref[qrows, cols], kn_ref[krows, cols]) + bias_a_ref[pat]

        def pv(p):
            return _pv_with_rowsum(p, vn_ref[krows, cols])

        def prior():
            return m_scr[hh, qrows, :], l_scr[hh, qrows, :], acc_scr[hh, qrows, :]

        def finish(l, acc):
            o = (acc / l) * gate_ref[qrows, cols].astype(jnp.float32)
            o_ref[qrows, cols] = o.astype(o_ref.dtype)

        return _Block(scores, pv, finish, prior, None, DL_DEPTH_A, i == 0)

    heads = range(qn_ref.shape[1] // HEAD_DIM)
    _run_pipeline([block_b(hh, r) for hh in heads for r in range(DL_STREAMS)]
                  + [block_a(hh, i) for hh in heads for i in range(seq // DL_Q)])


def _dl_bias_tables(seq):
    (w1, d1), (w4, d4), (w16, d16) = DIL_CONFIGS
    assert d1 == 1 and d4 == DL_STREAMS and d16 % d4 == 0
    assert (w1 // 2) // d1 == DL_HALF and (w4 // 2) // d4 == DL_HALF and (w16 // 2) // d16 == DL_HALF
    r = np.arange(DL_Q)[:, None]
    j = np.arange(DL_KA)[None, :]
    bias_a = np.stack([np.where(np.abs(j - off - r) <= DL_HALF, 0.0, NEG) for off in (0, DL_HALF, 2 * DL_HALF)])
    slen = seq // DL_STREAMS
    delta = np.arange(slen)[None, :] - np.arange(slen)[:, None]
    step = d16 // d4
    mult = (np.abs(delta) <= DL_HALF).astype(np.float64) + ((delta % step == 0) & (np.abs(delta) <= DL_HALF * step))
    bias_b = np.where(mult > 0, np.log2(np.maximum(mult, 1.0)), NEG)
    return jnp.asarray(bias_a, jnp.float32), jnp.asarray(bias_b, jnp.float32)


def _dl_attention(z, zs, bias_a, bias_b, batch, seq):
    n = z.shape[0]
    hc = DL_HPS * HEAD_DIM
    qb, kb, vb, gb = (OFF_DL_Q // hc, OFF_DL_K // hc, OFF_DL_V // hc, OFF_DL_G // hc)
    slen = seq // DL_STREAMS
    steps = N_HEADS_DIL // DL_HPS
    stream_spec = lambda c0: pl.BlockSpec((None, DL_STREAMS, slen, hc), lambda b, h: (b, 0, 0, c0 + h))
    return pl.pallas_call(
        _dl_kernel,
        grid=(batch, steps),
        in_specs=[
            pl.BlockSpec((seq, hc), lambda b, h: (b, qb + h)),
            pl.BlockSpec((seq, hc), lambda b, h: (b, kb + h)),
            pl.BlockSpec((seq, hc), lambda b, h: (b, vb + h)),
            pl.BlockSpec((seq, hc), lambda b, h: (b, gb + h)),
            stream_spec(0), stream_spec(steps), stream_spec(2 * steps),
            pl.BlockSpec(bias_a.shape, lambda b, h: (0, 0, 0)),
            pl.BlockSpec(bias_b.shape, lambda b, h: (0, 0)),
        ],
        out_specs=pl.BlockSpec((seq, hc), lambda b, h: (b, h)),
        out_shape=jax.ShapeDtypeStruct((n, W_DIL), jnp.bfloat16),
        scratch_shapes=[pltpu.VMEM((DL_HPS, seq, HEAD_DIM), jnp.float32)] * 3,
        compiler_params=pltpu.CompilerParams(
            dimension_semantics=("parallel", "parallel"), vmem_limit_bytes=V7X_VMEM_LIMIT),
        name="dl_attention",
    )(z, z, z, z, zs, zs, zs, bias_a, bias_b)


def _mem_kernel(q_ref, k_ref, v_ref, gate_ref, o_ref):
    def block(hh, i):
        cols = _head_cols(hh)
        qrows = slice(i * MEM_Q, (i + 1) * MEM_Q)

        def scores():
            return _dot_nt(q_ref[qrows, cols], k_ref[:, cols])

        def pv(p):
            return _pv_with_rowsum(p, v_ref[:, cols])

        def finish(l, acc):
            o = (acc / l) * gate_ref[qrows, cols].astype(jnp.float32)
            o_ref[qrows, cols] = o.astype(o_ref.dtype)

        return _Block(scores, pv, finish, None, None, MEM_DEPTH, False)

    heads = range(q_ref.shape[1] // HEAD_DIM)
    _run_pipeline([block(hh, i) for hh in heads for i in range(q_ref.shape[0] // MEM_Q)])


def _mem_attention(z, mkv, layer, batch, seq):
    n = z.shape[0]
    hc = MEM_HPS * HEAD_DIM
    qb, gb = OFF_M_Q // hc, OFF_M_G // hc
    kb = layer * 2 * W_MEM // hc
    vb = kb + W_MEM // hc
    return pl.pallas_call(
        _mem_kernel,
        grid=(batch, N_HEADS_MEM // MEM_HPS),
        in_specs=[
            pl.BlockSpec((seq, hc), lambda b, h: (b, qb + h)),
            pl.BlockSpec((N_MEM, hc), lambda b, h: (b, kb + h)),
            pl.BlockSpec((N_MEM, hc), lambda b, h: (b, vb + h)),
            pl.BlockSpec((seq, hc), lambda b, h: (b, gb + h)),
        ],
        out_specs=pl.BlockSpec((seq, hc), lambda b, h: (b, h)),
        out_shape=jax.ShapeDtypeStruct((n, W_MEM), jnp.bfloat16),
        compiler_params=pltpu.CompilerParams(dimension_semantics=("parallel", "parallel")),
        name="mem_attention",
    )(z, mkv, mkv, z)


def _rope_tables(seq):
    half = HEAD_DIM // 2
    inv = ROPE_THETA ** (-jnp.arange(half, dtype=jnp.float32) / half)
    ang = jnp.arange(seq).astype(jnp.float32)[:, None] * inv[None, :]
    cos, sin = jnp.cos(ang), jnp.sin(ang)
    return jnp.concatenate([cos, cos], axis=-1), jnp.concatenate([-sin, sin], axis=-1)


def kernel(x, mem, norm_g, w_in, na_rpb, mem_norm_g, w_mem_kv, w_out, final_g):
    batch, seq, d = x.shape
    depth = w_in.shape[0]
    assert d == D_MODEL and w_in.shape[2] == IN_COLS and seq % GRID_W == 0
    assert seq % PROJ_TM == 0 and seq % OUT_TM == 0 and seq % NA_Q == 0 and seq % MEM_Q == 0
    assert seq % (DL_STREAMS * DL_Q) == 0 and PROJ_TM % DL_STREAMS == 0
    assert all(off % (NA_HPS * HEAD_DIM) == 0 for off in (OFF_NA_Q, OFF_NA_K, OFF_NA_V, OFF_NA_G))
    assert all(off % (DL_HPS * HEAD_DIM) == 0 for off in (OFF_DL_Q, OFF_DL_K, OFF_DL_V, OFF_DL_G))
    assert all(off % (MEM_HPS * HEAD_DIM) == 0 for off in (OFF_M_Q, OFF_M_G, W_MEM))
    assert N_HEADS_NA % NA_HPS == 0 and N_HEADS_DIL % DL_HPS == 0 and N_HEADS_MEM % MEM_HPS == 0

    cos2, sin2 = _rope_tables(seq)
    dl_bias_a, dl_bias_b = _dl_bias_tables(seq)
    na_toep = _na_toeplitz(na_rpb)
    w_in_b = w_in.astype(jnp.bfloat16)
    w_out_b = w_out.astype(jnp.bfloat16)
    w_mem_b = jnp.transpose(w_mem_kv, (1, 0, 2)).reshape(1, d, depth * 2 * W_MEM).astype(jnp.bfloat16)
    mkv = _in_proj(mem.reshape(batch * N_MEM, d), mem_norm_g, w_mem_b, 0, cos2, sin2,
                   ((0, depth * 2 * W_MEM, "plain", None),), batch, N_MEM)

    xf = x.reshape(batch * seq, d)
    for l in range(depth):
        z, zs = _in_proj(xf, norm_g[l], w_in_b, l, cos2, sin2, IN_SEGMENTS, batch, seq)
        y_na = _na_attention(z, na_toep, l, batch, seq)
        y_dl = _dl_attention(z, zs, dl_bias_a, dl_bias_b, batch, seq)
        y_m = _mem_attention(z, mkv, l, batch, seq)
        xf = _out_proj(xf, y_na, y_dl, y_m, w_out_b, l, final_g, l == depth - 1)
    return xf.reshape(batch, seq, d)
```

```python
import collections
import functools

import numpy as np
import jax
import jax.numpy as jnp
from jax import lax
from jax.experimental import pallas as pl
from jax.experimental.pallas import tpu as pltpu

D_MODEL = 2048
HEAD_DIM = 128
N_HEADS_NA = 6
N_HEADS_DIL = 6
N_HEADS_MEM = 4
W_NA = N_HEADS_NA * HEAD_DIM
W_DIL = N_HEADS_DIL * HEAD_DIM
W_MEM = N_HEADS_MEM * HEAD_DIM
IN_COLS = 4 * W_NA + 4 * W_DIL + 2 * W_MEM
N_MEM = 256
GRID_W = 64
NA_WIN_ROWS = 8
NA_WIN_COLS = 16
DIL_CONFIGS = ((128, 1), (512, 4), (2048, 16))
ROPE_THETA = 10000.0
EPS = 1e-6
NEG = -1e30
SCALE = HEAD_DIM ** -0.5
LOG2E = float(np.log2(np.e))
Q_SCALE = SCALE * LOG2E

OFF_NA_Q, OFF_NA_K, OFF_NA_V, OFF_NA_G = 0, W_NA, 2 * W_NA, 3 * W_NA
OFF_DL_Q = 4 * W_NA
OFF_DL_K, OFF_DL_V, OFF_DL_G = OFF_DL_Q + W_DIL, OFF_DL_Q + 2 * W_DIL, OFF_DL_Q + 3 * W_DIL
OFF_M_Q = OFF_DL_Q + 4 * W_DIL
OFF_M_G = OFF_M_Q + W_MEM

IN_SEGMENTS = (
    (OFF_NA_Q, W_NA, "scale", None), (OFF_NA_K, W_NA, "plain", None),
    (OFF_NA_V, W_NA, "plain", None), (OFF_NA_G, W_NA, "silu", None),
    (OFF_DL_Q, W_DIL, "rope_scale", 0), (OFF_DL_K, W_DIL, "rope", W_DIL),
    (OFF_DL_V, W_DIL, "plain", 2 * W_DIL), (OFF_DL_G, W_DIL, "silu", None),
    (OFF_M_Q, W_MEM, "scale", None), (OFF_M_G, W_MEM, "silu", None),
)

V7X_VMEM_LIMIT = 62 * 1024 * 1024

PROJ_TM = 512
OUT_TM = 512
NA_HPS, DL_HPS, MEM_HPS = 3, 3, 4
NA_ROWS_PER_GROUP = 4
NA_Q = NA_ROWS_PER_GROUP * GRID_W
NA_KEY_ROWS = NA_ROWS_PER_GROUP + NA_WIN_ROWS
NA_K = NA_KEY_ROWS * GRID_W
NA_DEPTH = 2
MEM_Q = 512
MEM_DEPTH = 1

DL_STREAMS = 4
DL_HALF = 64
DL_Q = 128
DL_KA = DL_Q + 2 * DL_HALF
DL_DEPTH_A = 4
DL_DEPTH_B = 2


def _rms_norm_f32(x, g):
    return x * lax.rsqrt(jnp.mean(x * x, axis=-1, keepdims=True) + EPS) * g


def _dot_nt(a, b):
    return lax.dot_general(a, b, (((1,), (1,)), ((), ())), preferred_element_type=jnp.float32)


def _pv_with_rowsum(p, v):
    ones = jnp.ones((v.shape[0], HEAD_DIM), v.dtype)
    r = jnp.dot(p, jnp.concatenate([v, ones], axis=1), preferred_element_type=jnp.float32)
    return r[:, :HEAD_DIM], r[:, HEAD_DIM:]


def _head_cols(hh):
    return slice(hh * HEAD_DIM, (hh + 1) * HEAD_DIM)


def _in_proj_kernel(x_ref, g_ref, w_ref, cos_ref, sin_ref, o_ref, *rest, segments):
    if rest:
        s_ref, scr = rest
    tm = x_ref.shape[0]
    h = _rms_norm_f32(x_ref[...], g_ref[...]).astype(jnp.bfloat16)

    def emit(a, c0, width, s0):
        o_ref[:, c0:c0 + width] = a.astype(o_ref.dtype)
        if s0 is not None:
            for j in range(0, width, HEAD_DIM):
                scr[j // HEAD_DIM] = a[:, j:j + HEAD_DIM]
                for r in range(DL_STREAMS):
                    rows = scr[j // HEAD_DIM, pl.ds(r, tm // DL_STREAMS, stride=DL_STREAMS), :]
                    s_ref[r, :, s0 + j:s0 + j + HEAD_DIM] = rows.astype(s_ref.dtype)

    for off, width, kind, s_off in segments:
        a = jnp.dot(h, w_ref[:, off:off + width], preferred_element_type=jnp.float32)
        if kind in ("rope", "rope_scale"):
            for j in range(0, width, HEAD_DIM):
                aj = a[:, j:j + HEAD_DIM]
                aj = aj * cos_ref[...] + pltpu.roll(aj, HEAD_DIM // 2, axis=1) * sin_ref[...]
                if kind == "rope_scale":
                    aj = aj * Q_SCALE
                emit(aj, off + j, HEAD_DIM, None if s_off is None else s_off + j)
        else:
            if kind == "scale":
                a = a * Q_SCALE
            elif kind == "silu":
                a = a * (1.0 / (1.0 + jnp.exp(-a)))
            emit(a, off, width, s_off)


def _in_proj(x, g, w_bf16, layer, cos2, sin2, segments, batch, seq):
    n, d = x.shape
    cols = w_bf16.shape[2]
    tm = PROJ_TM
    bps = max(seq // tm, 1)
    streamed = any(s[3] is not None for s in segments)
    out_specs = [pl.BlockSpec((tm, cols), lambda i: (i, 0))]
    out_shape = [jax.ShapeDtypeStruct((n, cols), jnp.bfloat16)]
    scratch = []
    if streamed:
        out_specs.append(pl.BlockSpec((None, DL_STREAMS, tm // DL_STREAMS, 3 * W_DIL),
                                      lambda i: (i // bps, 0, i % bps, 0)))
        out_shape.append(jax.ShapeDtypeStruct((batch, DL_STREAMS, seq // DL_STREAMS, 3 * W_DIL), jnp.bfloat16))
        scratch.append(pltpu.VMEM((W_DIL // HEAD_DIM, tm, HEAD_DIM), jnp.float32))
    res = pl.pallas_call(
        functools.partial(_in_proj_kernel, segments=segments),
        grid=(n // tm,),
        in_specs=[
            pl.BlockSpec((tm, d), lambda i: (i, 0)),
            pl.BlockSpec((1, d), lambda i: (0, 0)),
            pl.BlockSpec((None, d, cols), lambda i: (layer, 0, 0), pipeline_mode=pl.Buffered(1)),
            pl.BlockSpec((tm, HEAD_DIM), lambda i: (i % bps, 0)),
            pl.BlockSpec((tm, HEAD_DIM), lambda i: (i % bps, 0)),
        ],
        out_specs=out_specs,
        out_shape=out_shape,
        scratch_shapes=scratch,
        compiler_params=pltpu.CompilerParams(
            dimension_semantics=("parallel",), vmem_limit_bytes=V7X_VMEM_LIMIT),
        name="in_proj",
    )(x, g.reshape(1, d), w_bf16, cos2, sin2)
    return res if streamed else res[0]


def _out_proj_kernel(x_ref, yna_ref, ydl_ref, ym_ref, w_ref, g_ref, *rest, final_norm):
    if len(rest) == 3:
        wnext_ref, o_ref, wnext_bf16_ref = rest
        wnext_bf16_ref[...] = wnext_ref[...].astype(wnext_bf16_ref.dtype)
    else:
        (o_ref,) = rest
    acc = jnp.dot(yna_ref[...], w_ref[0:W_NA, :], preferred_element_type=jnp.float32)
    acc += jnp.dot(ydl_ref[...], w_ref[W_NA:W_NA + W_DIL, :], preferred_element_type=jnp.float32)
    acc += jnp.dot(ym_ref[...], w_ref[W_NA + W_DIL:, :], preferred_element_type=jnp.float32)
    x = x_ref[...] + acc
    if final_norm:
        x = _rms_norm_f32(x, g_ref[...])
    o_ref[...] = x


def _out_proj(x, y_na, y_dl, y_m, w_bf16, layer, final_g, final_norm, w_in_f32=None):
    n, d = x.shape
    tm = OUT_TM
    steps = n // tm
    in_specs = [
        pl.BlockSpec((tm, d), lambda i: (i, 0)),
        pl.BlockSpec((tm, W_NA), lambda i: (i, 0)),
        pl.BlockSpec((tm, W_DIL), lambda i: (i, 0)),
        pl.BlockSpec((tm, W_MEM), lambda i: (i, 0)),
        pl.BlockSpec((None, d, d), lambda i: (layer, 0, 0), pipeline_mode=pl.Buffered(1)),
        pl.BlockSpec((1, d), lambda i: (0, 0)),
    ]
    out_specs = [pl.BlockSpec((tm, d), lambda i: (i, 0))]
    out_shape = [jax.ShapeDtypeStruct((n, d), jnp.float32)]
    args = [x, y_na, y_dl, y_m, w_bf16, final_g.reshape(1, d)]
    if w_in_f32 is not None:
        cols = w_in_f32.shape[2]
        assert d % steps == 0
        slab = d // steps
        in_specs.append(pl.BlockSpec((None, slab, cols), lambda i: (layer + 1, i, 0)))
        out_specs.append(pl.BlockSpec((None, slab, cols), lambda i: (0, i, 0)))
        out_shape.append(jax.ShapeDtypeStruct((1, d, cols), jnp.bfloat16))
        args.append(w_in_f32)
    res = pl.pallas_call(
        functools.partial(_out_proj_kernel, final_norm=final_norm),
        grid=(steps,),
        in_specs=in_specs,
        out_specs=out_specs,
        out_shape=out_shape,
        compiler_params=pltpu.CompilerParams(
            dimension_semantics=("parallel",), vmem_limit_bytes=V7X_VMEM_LIMIT),
        name="out_proj",
    )(*args)
    return res if w_in_f32 is not None else res[0]


_Block = collections.namedtuple("_Block", "scores pv finish prior stats depth flush")


def _run_pipeline(blocks):
    issued = []
    pending = None
    for i, blk in enumerate(blocks):
        while len(issued) < min(i + 1 + blk.depth, len(blocks)):
            issued.append(blocks[len(issued)].scores())
        s = issued[i]
        issued[i] = None
        if blk.flush and pending is not None:
            pending[0].finish(*pending[1:])
            pending = None
        m = jnp.max(s, axis=-1, keepdims=True)
        if blk.stats is not None:
            blk.stats(m)
        if blk.prior is None:
            p = jnp.exp2(s - m)
            acc, l = blk.pv(p.astype(jnp.bfloat16))
        else:
            m_b, l_b, acc_b = blk.prior()
            m = jnp.maximum(m, m_b)
            p = jnp.exp2(s - jnp.concatenate([m] * (s.shape[1] // m.shape[1]), axis=1))
            alpha = jnp.exp2(m_b - m)
            acc, l = blk.pv(p.astype(jnp.bfloat16))
            l = alpha * l_b + l
            acc = alpha * acc_b + acc
        if pending is not None:
            pending[0].finish(*pending[1:])
        pending = (blk, l, acc)
    pending[0].finish(*pending[1:])


def _na_kernel(q_ref, k_ref, v_ref, gate_ref, toep_ref, o_ref, bias_ref):
    n_groups = q_ref.shape[0] // NA_Q
    rows = q_ref.shape[0] // GRID_W

    def key_start(g):
        return int(np.clip(g * NA_ROWS_PER_GROUP - NA_WIN_ROWS // 2, 0, rows - NA_KEY_ROWS)) * GRID_W

    @pl.when(pl.program_id(1) == 0)
    def _():
        outside = jnp.full((GRID_W, GRID_W), NEG, jnp.float32)
        for hh in range(q_ref.shape[1] // HEAD_DIM):
            for pat, g in enumerate((0, 1, n_groups - 1)):
                base = key_start(g) // GRID_W
                for qi in range(NA_ROWS_PER_GROUP):
                    r = g * NA_ROWS_PER_GROUP + qi
                    rs = int(np.clip(r - NA_WIN_ROWS // 2, 0, rows - NA_WIN_ROWS))
                    for kj in range(0, NA_KEY_ROWS, 2):
                        pair = [toep_ref[hh, base + kk - r + NA_WIN_ROWS - 1]
                                if rs <= base + kk < rs + NA_WIN_ROWS else outside for kk in (kj, kj + 1)]
                        bias_ref[hh, pat, qi * GRID_W:(qi + 1) * GRID_W, kj * GRID_W:(kj + 2) * GRID_W] = (
                            jnp.concatenate(pair, axis=1))

    def block(hh, g):
        cols = _head_cols(hh)
        qrows = slice(g * NA_Q, (g + 1) * NA_Q)
        krows = slice(key_start(g), key_start(g) + NA_K)
        pat = 0 if g == 0 else (2 if g == n_groups - 1 else 1)

        def scores():
            return _dot_nt(q_ref[qrows, cols], k_ref[krows, cols]) + bias_ref[hh, pat]

        def pv(p):
            return _pv_with_rowsum(p, v_ref[krows, cols])

        def finish(l, acc):
            o = (acc / l) * gate_ref[qrows, cols].astype(jnp.float32)
            o_ref[qrows, cols] = o.astype(o_ref.dtype)

        return _Block(scores, pv, finish, None, None, NA_DEPTH, False)

    _run_pipeline([block(hh, g) for hh in range(q_ref.shape[1] // HEAD_DIM) for g in range(n_groups)])


def _na_toeplitz(rpb):
    c = np.arange(GRID_W)
    cs = np.clip(c - NA_WIN_COLS // 2, 0, GRID_W - NA_WIN_COLS)
    col_ok = (c[None, :] >= cs[:, None]) & (c[None, :] < cs[:, None] + NA_WIN_COLS)
    dc = np.clip(c[None, :] - c[:, None], -(NA_WIN_COLS - 1), NA_WIN_COLS - 1) + NA_WIN_COLS - 1
    onehot = (dc[None] == np.arange(2 * NA_WIN_COLS - 1)[:, None, None]).astype(np.float32)
    toep = jnp.einsum("lhaj,jqk->lhaqk", rpb.astype(jnp.float32), jnp.asarray(onehot),
                      precision=lax.Precision.HIGHEST)
    return jnp.where(col_ok, toep * LOG2E, NEG)


def _na_attention(z, toep, layer, batch, seq):
    n = z.shape[0]
    hc = NA_HPS * HEAD_DIM
    qb, kb, vb, gb = (OFF_NA_Q // hc, OFF_NA_K // hc, OFF_NA_V // hc, OFF_NA_G // hc)
    return pl.pallas_call(
        _na_kernel,
        grid=(N_HEADS_NA // NA_HPS, batch),
        in_specs=[
            pl.BlockSpec((seq, hc), lambda h, b: (b, qb + h)),
            pl.BlockSpec((seq, hc), lambda h, b: (b, kb + h)),
            pl.BlockSpec((seq, hc), lambda h, b: (b, vb + h)),
            pl.BlockSpec((seq, hc), lambda h, b: (b, gb + h)),
            pl.BlockSpec((None, NA_HPS) + toep.shape[2:], lambda h, b: (layer, h, 0, 0, 0)),
        ],
        out_specs=pl.BlockSpec((seq, hc), lambda h, b: (b, h)),
        out_shape=jax.ShapeDtypeStruct((n, W_NA), jnp.bfloat16),
        scratch_shapes=[pltpu.VMEM((NA_HPS, 3, NA_Q, NA_K), jnp.float32)],
        compiler_params=pltpu.CompilerParams(
            dimension_semantics=("parallel", "arbitrary"), vmem_limit_bytes=V7X_VMEM_LIMIT),
        name="na_attention",
    )(z, z, z, z, toep)


def _dl_kernel(qn_ref, kn_ref, vn_ref, gate_ref, qs_ref, ks_ref, vs_ref, bias_a_ref, bias_b_ref, o_ref,
               m_scr, l_scr, acc_scr):
    seq = qn_ref.shape[0]
    slen = seq // DL_STREAMS

    def block_b(hh, r):
        cols = _head_cols(hh)
        rows = pl.ds(r, slen, stride=DL_STREAMS)

        def scores():
            return _dot_nt(qs_ref[r, :, cols], ks_ref[r, :, cols]) + bias_b_ref[...]

        def pv(p):
            return _pv_with_rowsum(p, vs_ref[r, :, cols])

        def stats(m):
            m_scr[hh, rows, :] = jnp.broadcast_to(m, (slen, HEAD_DIM))

        def finish(l, acc):
            l_scr[hh, rows, :] = l
            acc_scr[hh, rows, :] = acc

        return _Block(scores, pv, finish, None, stats, DL_DEPTH_B, False)

    def block_a(hh, i):
        cols = _head_cols(hh)
        qrows = slice(i * DL_Q, (i + 1) * DL_Q)
        k0 = int(np.clip(i * DL_Q - DL_HALF, 0, seq - DL_KA))
        krows = slice(k0, k0 + DL_KA)
        pat = (i * DL_Q - k0) // DL_HALF

        def scores():
            return _dot_nt(qn_ref[qrows, cols], kn_ref[krows, cols]) + bias_a_ref[pat]

        def pv(p):
            return _pv_with_rowsum(p, vn_ref[krows, cols])

        def prior():
            return m_scr[hh, qrows, :], l_scr[hh, qrows, :], acc_scr[hh, qrows, :]

        def finish(l, acc):
            o = (acc / l) * gate_ref[qrows, cols].astype(jnp.float32)
            o_ref[qrows, cols] = o.astype(o_ref.dtype)

        return _Block(scores, pv, finish, prior, None, DL_DEPTH_A, i == 0)

    heads = range(qn_ref.shape[1] // HEAD_DIM)
    _run_pipeline([block_b(hh, r) for hh in heads for r in range(DL_STREAMS)]
                  + [block_a(hh, i) for hh in heads for i in range(seq // DL_Q)])


def _dl_bias_tables(seq):
    (w1, d1), (w4, d4), (w16, d16) = DIL_CONFIGS
    assert d1 == 1 and d4 == DL_STREAMS and d16 % d4 == 0
    assert (w1 // 2) // d1 == DL_HALF and (w4 // 2) // d4 == DL_HALF and (w16 // 2) // d16 == DL_HALF
    r = np.arange(DL_Q)[:, None]
    j = np.arange(DL_KA)[None, :]
    bias_a = np.stack([np.where(np.abs(j - off - r) <= DL_HALF, 0.0, NEG) for off in (0, DL_HALF, 2 * DL_HALF)])
    slen = seq // DL_STREAMS
    delta = np.arange(slen)[None, :] - np.arange(slen)[:, None]
    step = d16 // d4
    mult = (np.abs(delta) <= DL_HALF).astype(np.float64) + ((delta % step == 0) & (np.abs(delta) <= DL_HALF * step))
    bias_b = np.where(mult > 0, np.log2(np.maximum(mult, 1.0)), NEG)
    return jnp.asarray(bias_a, jnp.float32), jnp.asarray(bias_b, jnp.float32)


def _dl_attention(z, zs, bias_a, bias_b, batch, seq):
    n = z.shape[0]
    hc = DL_HPS * HEAD_DIM
    qb, kb, vb, gb = (OFF_DL_Q // hc, OFF_DL_K // hc, OFF_DL_V // hc, OFF_DL_G // hc)
    slen = seq // DL_STREAMS
    steps = N_HEADS_DIL // DL_HPS
    stream_spec = lambda c0: pl.BlockSpec((None, DL_STREAMS, slen, hc), lambda b, h: (b, 0, 0, c0 + h))
    return pl.pallas_call(
        _dl_kernel,
        grid=(batch, steps),
        in_specs=[
            pl.BlockSpec((seq, hc), lambda b, h: (b, qb + h)),
            pl.BlockSpec((seq, hc), lambda b, h: (b, kb + h)),
            pl.BlockSpec((seq, hc), lambda b, h: (b, vb + h)),
            pl.BlockSpec((seq, hc), lambda b, h: (b, gb + h)),
            stream_spec(0), stream_spec(steps), stream_spec(2 * steps),
            pl.BlockSpec(bias_a.shape, lambda b, h: (0, 0, 0)),
            pl.BlockSpec(bias_b.shape, lambda b, h: (0, 0)),
        ],
        out_specs=pl.BlockSpec((seq, hc), lambda b, h: (b, h)),
        out_shape=jax.ShapeDtypeStruct((n, W_DIL), jnp.bfloat16),
        scratch_shapes=[pltpu.VMEM((DL_HPS, seq, HEAD_DIM), jnp.float32)] * 3,
        compiler_params=pltpu.CompilerParams(
            dimension_semantics=("parallel", "parallel"), vmem_limit_bytes=V7X_VMEM_LIMIT),
        name="dl_attention",
    )(z, z, z, z, zs, zs, zs, bias_a, bias_b)


def _mem_kernel(q_ref, k_ref, v_ref, gate_ref, o_ref):
    def block(hh, i):
        cols = _head_cols(hh)
        qrows = slice(i * MEM_Q, (i + 1) * MEM_Q)

        def scores():
            return _dot_nt(q_ref[qrows, cols], k_ref[:, cols])

        def pv(p):
            return _pv_with_rowsum(p, v_ref[:, cols])

        def finish(l, acc):
            o = (acc / l) * gate_ref[qrows, cols].astype(jnp.float32)
            o_ref[qrows, cols] = o.astype(o_ref.dtype)

        return _Block(scores, pv, finish, None, None, MEM_DEPTH, False)

    heads = range(q_ref.shape[1] // HEAD_DIM)
    _run_pipeline([block(hh, i) for hh in heads for i in range(q_ref.shape[0] // MEM_Q)])


def _mem_attention(z, mkv, batch, seq):
    n = z.shape[0]
    hc = MEM_HPS * HEAD_DIM
    qb, gb = OFF_M_Q // hc, OFF_M_G // hc
    kb, vb = 0, W_MEM // hc
    return pl.pallas_call(
        _mem_kernel,
        grid=(batch, N_HEADS_MEM // MEM_HPS),
        in_specs=[
            pl.BlockSpec((seq, hc), lambda b, h: (b, qb + h)),
            pl.BlockSpec((N_MEM, hc), lambda b, h: (b, kb + h)),
            pl.BlockSpec((N_MEM, hc), lambda b, h: (b, vb + h)),
            pl.BlockSpec((seq, hc), lambda b, h: (b, gb + h)),
        ],
        out_specs=pl.BlockSpec((seq, hc), lambda b, h: (b, h)),
        out_shape=jax.ShapeDtypeStruct((n, W_MEM), jnp.bfloat16),
        compiler_params=pltpu.CompilerParams(dimension_semantics=("parallel", "parallel")),
        name="mem_attention",
    )(z, mkv, mkv, z)


def _rope_tables(seq):
    half = HEAD_DIM // 2
    inv = ROPE_THETA ** (-jnp.arange(half, dtype=jnp.float32) / half)
    ang = jnp.arange(seq).astype(jnp.float32)[:, None] * inv[None, :]
    cos, sin = jnp.cos(ang), jnp.sin(ang)
    return jnp.concatenate([cos, cos], axis=-1), jnp.concatenate([-sin, sin], axis=-1)


def kernel(x, mem, norm_g, w_in, na_rpb, mem_norm_g, w_mem_kv, w_out, final_g):
    batch, seq, d = x.shape
    depth = w_in.shape[0]
    assert d == D_MODEL and w_in.shape[2] == IN_COLS and seq % GRID_W == 0
    assert seq % PROJ_TM == 0 and seq % OUT_TM == 0 and seq % NA_Q == 0 and seq % MEM_Q == 0
    assert seq % (DL_STREAMS * DL_Q) == 0 and PROJ_TM % DL_STREAMS == 0
    assert all(off % (NA_HPS * HEAD_DIM) == 0 for off in (OFF_NA_Q, OFF_NA_K, OFF_NA_V, OFF_NA_G))
    assert all(off % (DL_HPS * HEAD_DIM) == 0 for off in (OFF_DL_Q, OFF_DL_K, OFF_DL_V, OFF_DL_G))
    assert all(off % (MEM_HPS * HEAD_DIM) == 0 for off in (OFF_M_Q, OFF_M_G, W_MEM))
    assert N_HEADS_NA % NA_HPS == 0 and N_HEADS_DIL % DL_HPS == 0 and N_HEADS_MEM % MEM_HPS == 0

    cos2, sin2 = _rope_tables(seq)
    dl_bias_a, dl_bias_b = _dl_bias_tables(seq)
    na_toep = _na_toeplitz(na_rpb)
    w_in_b = w_in[:1].astype(jnp.bfloat16)
    w_out_b = w_out.astype(jnp.bfloat16)
    w_mem_b = w_mem_kv.astype(jnp.bfloat16)
    mem_f = mem.reshape(batch * N_MEM, d)
    mem_segments = ((0, 2 * W_MEM, "plain", None),)

    xf = x.reshape(batch * seq, d)
    for l in range(depth):
        z, zs = _in_proj(xf, norm_g[l], w_in_b, 0, cos2, sin2, IN_SEGMENTS, batch, seq)
        mkv = _in_proj(mem_f, mem_norm_g, w_mem_b, l, cos2, sin2, mem_segments, batch, N_MEM)
        y_na = _na_attention(z, na_toep, l, batch, seq)
        y_dl = _dl_attention(z, zs, dl_bias_a, dl_bias_b, batch, seq)
        y_m = _mem_attention(z, mkv, batch, seq)
        if l + 1 < depth:
            xf, w_in_b = _out_proj(xf, y_na, y_dl, y_m, w_out_b, l, final_g, False, w_in_f32=w_in)
        else:
            xf = _out_proj(xf, y_na, y_dl, y_m, w_out_b, l, final_g, True)
    return xf.reshape(batch, seq, d)
```

```python
import collections
import functools

import numpy as np
import jax
import jax.numpy as jnp
from jax import lax
from jax.experimental import pallas as pl
from jax.experimental.pallas import tpu as pltpu

D_MODEL = 2048
HEAD_DIM = 128
N_HEADS_NA = 6
N_HEADS_DIL = 6
N_HEADS_MEM = 4
W_NA = N_HEADS_NA * HEAD_DIM
W_DIL = N_HEADS_DIL * HEAD_DIM
W_MEM = N_HEADS_MEM * HEAD_DIM
IN_COLS = 4 * W_NA + 4 * W_DIL + 2 * W_MEM
N_MEM = 256
GRID_W = 64
NA_WIN_ROWS = 8
NA_WIN_COLS = 16
DIL_CONFIGS = ((128, 1), (512, 4), (2048, 16))
ROPE_THETA = 10000.0
EPS = 1e-6
NEG = -1e30
SCALE = HEAD_DIM ** -0.5
LOG2E = float(np.log2(np.e))
Q_SCALE = SCALE * LOG2E

OFF_NA_Q, OFF_NA_K, OFF_NA_V, OFF_NA_G = 0, W_NA, 2 * W_NA, 3 * W_NA
OFF_DL_Q = 4 * W_NA
OFF_DL_K, OFF_DL_V, OFF_DL_G = OFF_DL_Q + W_DIL, OFF_DL_Q + 2 * W_DIL, OFF_DL_Q + 3 * W_DIL
OFF_M_Q = OFF_DL_Q + 4 * W_DIL
OFF_M_G = OFF_M_Q + W_MEM

IN_SEGMENTS = (
    (OFF_NA_Q, W_NA, "scale", None), (OFF_NA_K, W_NA, "plain", None),
    (OFF_NA_V, W_NA, "plain", None), (OFF_NA_G, W_NA, "silu", None),
    (OFF_DL_Q, W_DIL, "rope_scale", 0), (OFF_DL_K, W_DIL, "rope", W_DIL),
    (OFF_DL_V, W_DIL, "plain", 2 * W_DIL), (OFF_DL_G, W_DIL, "silu", None),
    (OFF_M_Q, W_MEM, "scale", None), (OFF_M_G, W_MEM, "silu", None),
)

V7X_VMEM_LIMIT = 62 * 1024 * 1024

PROJ_TM = 512
OUT_TM = 512
NA_HPS, DL_HPS, MEM_HPS = 6, 3, 4
NA_ROWS_PER_GROUP = 4
NA_Q = NA_ROWS_PER_GROUP * GRID_W
NA_KEY_ROWS = NA_ROWS_PER_GROUP + NA_WIN_ROWS
NA_K = NA_KEY_ROWS * GRID_W
NA_DEPTH = 2
MEM_Q = 512
MEM_DEPTH = 1

DL_STREAMS = 4
DL_HALF = 64
DL_Q = 128
DL_KA = DL_Q + 2 * DL_HALF
DL_DEPTH_A = 4
DL_DEPTH_B = 2


def _rms_norm_f32(x, g):
    return x * lax.rsqrt(jnp.mean(x * x, axis=-1, keepdims=True) + EPS) * g


def _dot_nt(a, b):
    return lax.dot_general(a, b, (((1,), (1,)), ((), ())), preferred_element_type=jnp.float32)


def _pv_with_rowsum(p, v):
    ones = jnp.ones((v.shape[0], HEAD_DIM), v.dtype)
    r = jnp.dot(p, jnp.concatenate([v, ones], axis=1), preferred_element_type=jnp.float32)
    return r[:, :HEAD_DIM], r[:, HEAD_DIM:]


def _head_cols(hh):
    return slice(hh * HEAD_DIM, (hh + 1) * HEAD_DIM)


def _in_proj_kernel(x_ref, g_ref, w_ref, cos_ref, sin_ref, o_ref, *rest, segments):
    if rest:
        s_ref, scr = rest
    tm = x_ref.shape[0]
    h = _rms_norm_f32(x_ref[...], g_ref[...]).astype(jnp.bfloat16)

    def emit(a, c0, width, s0):
        o_ref[:, c0:c0 + width] = a.astype(o_ref.dtype)
        if s0 is not None:
            for j in range(0, width, HEAD_DIM):
                scr[j // HEAD_DIM] = a[:, j:j + HEAD_DIM]
                for r in range(DL_STREAMS):
                    rows = scr[j // HEAD_DIM, pl.ds(r, tm // DL_STREAMS, stride=DL_STREAMS), :]
                    s_ref[r, :, s0 + j:s0 + j + HEAD_DIM] = rows.astype(s_ref.dtype)

    for off, width, kind, s_off in segments:
        a = jnp.dot(h, w_ref[:, off:off + width], preferred_element_type=jnp.float32)
        if kind in ("rope", "rope_scale"):
            for j in range(0, width, HEAD_DIM):
                aj = a[:, j:j + HEAD_DIM]
                aj = aj * cos_ref[...] + pltpu.roll(aj, HEAD_DIM // 2, axis=1) * sin_ref[...]
                if kind == "rope_scale":
                    aj = aj * Q_SCALE
                emit(aj, off + j, HEAD_DIM, None if s_off is None else s_off + j)
        else:
            if kind == "scale":
                a = a * Q_SCALE
            elif kind == "silu":
                a = a * (1.0 / (1.0 + jnp.exp(-a)))
            emit(a, off, width, s_off)


def _in_proj(x, g, w_bf16, layer, cos2, sin2, segments, batch, seq):
    n, d = x.shape
    cols = w_bf16.shape[2]
    tm = PROJ_TM
    bps = max(seq // tm, 1)
    streamed = any(s[3] is not None for s in segments)
    out_specs = [pl.BlockSpec((tm, cols), lambda i: (i, 0))]
    out_shape = [jax.ShapeDtypeStruct((n, cols), jnp.bfloat16)]
    scratch = []
    if streamed:
        out_specs.append(pl.BlockSpec((None, DL_STREAMS, tm // DL_STREAMS, 3 * W_DIL),
                                      lambda i: (i // bps, 0, i % bps, 0)))
        out_shape.append(jax.ShapeDtypeStruct((batch, DL_STREAMS, seq // DL_STREAMS, 3 * W_DIL), jnp.bfloat16))
        scratch.append(pltpu.VMEM((W_DIL // HEAD_DIM, tm, HEAD_DIM), jnp.float32))
    res = pl.pallas_call(
        functools.partial(_in_proj_kernel, segments=segments),
        grid=(n // tm,),
        in_specs=[
            pl.BlockSpec((tm, d), lambda i: (i, 0)),
            pl.BlockSpec((1, d), lambda i: (0, 0)),
            pl.BlockSpec((None, d, cols), lambda i: (layer, 0, 0), pipeline_mode=pl.Buffered(1)),
            pl.BlockSpec((tm, HEAD_DIM), lambda i: (i % bps, 0)),
            pl.BlockSpec((tm, HEAD_DIM), lambda i: (i % bps, 0)),
        ],
        out_specs=out_specs,
        out_shape=out_shape,
        scratch_shapes=scratch,
        compiler_params=pltpu.CompilerParams(
            dimension_semantics=("parallel",), vmem_limit_bytes=V7X_VMEM_LIMIT),
        name="in_proj",
    )(x, g.reshape(1, d), w_bf16, cos2, sin2)
    return res if streamed else res[0]


def _out_proj_kernel(x_ref, yna_ref, ydl_ref, ym_ref, w_ref, g_ref, *rest, final_norm):
    if len(rest) == 3:
        wnext_ref, o_ref, wnext_bf16_ref = rest
        wnext_bf16_ref[...] = wnext_ref[...].astype(wnext_bf16_ref.dtype)
    else:
        (o_ref,) = rest
    acc = jnp.dot(yna_ref[...], w_ref[0:W_NA, :], preferred_element_type=jnp.float32)
    acc += jnp.dot(ydl_ref[...], w_ref[W_NA:W_NA + W_DIL, :], preferred_element_type=jnp.float32)
    acc += jnp.dot(ym_ref[...], w_ref[W_NA + W_DIL:, :], preferred_element_type=jnp.float32)
    x = x_ref[...] + acc
    if final_norm:
        x = _rms_norm_f32(x, g_ref[...])
    o_ref[...] = x


def _out_proj(x, y_na, y_dl, y_m, w_bf16, layer, final_g, final_norm, w_in_f32=None):
    n, d = x.shape
    tm = OUT_TM
    steps = n // tm
    in_specs = [
        pl.BlockSpec((tm, d), lambda i: (i, 0)),
        pl.BlockSpec((tm, W_NA), lambda i: (i, 0)),
        pl.BlockSpec((tm, W_DIL), lambda i: (i, 0)),
        pl.BlockSpec((tm, W_MEM), lambda i: (i, 0)),
        pl.BlockSpec((None, d, d), lambda i: (layer, 0, 0), pipeline_mode=pl.Buffered(1)),
        pl.BlockSpec((1, d), lambda i: (0, 0)),
    ]
    out_specs = [pl.BlockSpec((tm, d), lambda i: (i, 0))]
    out_shape = [jax.ShapeDtypeStruct((n, d), jnp.float32)]
    args = [x, y_na, y_dl, y_m, w_bf16, final_g.reshape(1, d)]
    if w_in_f32 is not None:
        cols = w_in_f32.shape[2]
        assert d % steps == 0
        slab = d // steps
        in_specs.append(pl.BlockSpec((None, slab, cols), lambda i: (layer + 1, i, 0)))
        out_specs.append(pl.BlockSpec((None, slab, cols), lambda i: (0, i, 0)))
        out_shape.append(jax.ShapeDtypeStruct((1, d, cols), jnp.bfloat16))
        args.append(w_in_f32)
    res = pl.pallas_call(
        functools.partial(_out_proj_kernel, final_norm=final_norm),
        grid=(steps,),
        in_specs=in_specs,
        out_specs=out_specs,
        out_shape=out_shape,
        compiler_params=pltpu.CompilerParams(
            dimension_semantics=("parallel",), vmem_limit_bytes=V7X_VMEM_LIMIT),
        name="out_proj",
    )(*args)
    return res if w_in_f32 is not None else res[0]


_Block = collections.namedtuple("_Block", "scores pv finish prior stats depth flush")


def _run_pipeline(blocks):
    issued = []
    pending = None
    for i, blk in enumerate(blocks):
        while len(issued) < min(i + 1 + blk.depth, len(blocks)):
            issued.append(blocks[len(issued)].scores())
        s = issued[i]
        issued[i] = None
        if blk.flush and pending is not None:
            pending[0].finish(*pending[1:])
            pending = None
        m = jnp.max(s, axis=-1, keepdims=True)
        if blk.stats is not None:
            blk.stats(m)
        if blk.prior is None:
            p = jnp.exp2(s - m)
            acc, l = blk.pv(p.astype(jnp.bfloat16))
        else:
            m_b, l_b, acc_b = blk.prior()
            m = jnp.maximum(m, m_b)
            p = jnp.exp2(s - jnp.concatenate([m] * (s.shape[1] // m.shape[1]), axis=1))
            alpha = jnp.exp2(m_b - m)
            acc, l = blk.pv(p.astype(jnp.bfloat16))
            l = alpha * l_b + l
            acc = alpha * acc_b + acc
        if pending is not None:
            pending[0].finish(*pending[1:])
        pending = (blk, l, acc)
    pending[0].finish(*pending[1:])


def _na_kernel(q_ref, k_ref, v_ref, gate_ref, toep_ref, o_ref, bias_ref):
    n_groups = q_ref.shape[0] // NA_Q
    rows = q_ref.shape[0] // GRID_W

    def key_start(g):
        return int(np.clip(g * NA_ROWS_PER_GROUP - NA_WIN_ROWS // 2, 0, rows - NA_KEY_ROWS)) * GRID_W

    @pl.when(pl.program_id(1) == 0)
    def _():
        outside = jnp.full((GRID_W, GRID_W), NEG, jnp.float32)
        for hh in range(q_ref.shape[1] // HEAD_DIM):
            for pat, g in enumerate((0, 1, n_groups - 1)):
                base = key_start(g) // GRID_W
                for qi in range(NA_ROWS_PER_GROUP):
                    r = g * NA_ROWS_PER_GROUP + qi
                    rs = int(np.clip(r - NA_WIN_ROWS // 2, 0, rows - NA_WIN_ROWS))
                    for kj in range(0, NA_KEY_ROWS, 2):
                        pair = [toep_ref[hh, base + kk - r + NA_WIN_ROWS - 1]
                                if rs <= base + kk < rs + NA_WIN_ROWS else outside for kk in (kj, kj + 1)]
                        bias_ref[hh, pat, qi * GRID_W:(qi + 1) * GRID_W, kj * GRID_W:(kj + 2) * GRID_W] = (
                            jnp.concatenate(pair, axis=1))

    def block(hh, g):
        cols = _head_cols(hh)
        qrows = slice(g * NA_Q, (g + 1) * NA_Q)
        krows = slice(key_start(g), key_start(g) + NA_K)
        pat = 0 if g == 0 else (2 if g == n_groups - 1 else 1)

        def scores():
            return _dot_nt(q_ref[qrows, cols], k_ref[krows, cols]) + bias_ref[hh, pat]

        def pv(p):
            return _pv_with_rowsum(p, v_ref[krows, cols])

        def finish(l, acc):
            o = (acc / l) * gate_ref[qrows, cols].astype(jnp.float32)
            o_ref[qrows, cols] = o.astype(o_ref.dtype)

        return _Block(scores, pv, finish, None, None, NA_DEPTH, False)

    _run_pipeline([block(hh, g) for hh in range(q_ref.shape[1] // HEAD_DIM) for g in range(n_groups)])


def _na_toeplitz(rpb):
    c = np.arange(GRID_W)
    cs = np.clip(c - NA_WIN_COLS // 2, 0, GRID_W - NA_WIN_COLS)
    col_ok = (c[None, :] >= cs[:, None]) & (c[None, :] < cs[:, None] + NA_WIN_COLS)
    dc = np.clip(c[None, :] - c[:, None], -(NA_WIN_COLS - 1), NA_WIN_COLS - 1) + NA_WIN_COLS - 1
    onehot = (dc[None] == np.arange(2 * NA_WIN_COLS - 1)[:, None, None]).astype(np.float32)
    toep = jnp.einsum("lhaj,jqk->lhaqk", rpb.astype(jnp.float32), jnp.asarray(onehot),
                      precision=lax.Precision.HIGHEST)
    return jnp.where(col_ok, toep * LOG2E, NEG)


def _na_attention(z, toep, layer, batch, seq):
    n = z.shape[0]
    hc = NA_HPS * HEAD_DIM
    qb, kb, vb, gb = (OFF_NA_Q // hc, OFF_NA_K // hc, OFF_NA_V // hc, OFF_NA_G // hc)
    return pl.pallas_call(
        _na_kernel,
        grid=(N_HEADS_NA // NA_HPS, batch),
        in_specs=[
            pl.BlockSpec((seq, hc), lambda h, b: (b, qb + h)),
            pl.BlockSpec((seq, hc), lambda h, b: (b, kb + h)),
            pl.BlockSpec((seq, hc), lambda h, b: (b, vb + h)),
            pl.BlockSpec((seq, hc), lambda h, b: (b, gb + h)),
            pl.BlockSpec((None, NA_HPS) + toep.shape[2:], lambda h, b: (layer, h, 0, 0, 0)),
        ],
        out_specs=pl.BlockSpec((seq, hc), lambda h, b: (b, h)),
        out_shape=jax.ShapeDtypeStruct((n, W_NA), jnp.bfloat16),
        scratch_shapes=[pltpu.VMEM((NA_HPS, 3, NA_Q, NA_K), jnp.float32)],
        compiler_params=pltpu.CompilerParams(
            dimension_semantics=("parallel", "arbitrary"), vmem_limit_bytes=V7X_VMEM_LIMIT),
        name="na_attention",
    )(z, z, z, z, toep)


def _dl_kernel(qn_ref, kn_ref, vn_ref, gate_ref, qs_ref, ks_ref, vs_ref, bias_a_ref, bias_b_ref, o_ref,
               m_scr, l_scr, acc_scr):
    seq = qn_ref.shape[0]
    slen = seq // DL_STREAMS

    def block_b(hh, r):
        cols = _head_cols(hh)
        rows = pl.ds(r, slen, stride=DL_STREAMS)

        def scores():
            return _dot_nt(qs_ref[r, :, cols], ks_ref[r, :, cols]) + bias_b_ref[...]

        def pv(p):
            return _pv_with_rowsum(p, vs_ref[r, :, cols])

        def stats(m):
            m_scr[hh, rows, :] = jnp.broadcast_to(m, (slen, HEAD_DIM))

        def finish(l, acc):
            l_scr[hh, rows, :] = l
            acc_scr[hh, rows, :] = acc

        return _Block(scores, pv, finish, None, stats, DL_DEPTH_B, False)

    def block_a(hh, i):
        cols = _head_cols(hh)
        qrows = slice(i * DL_Q, (i + 1) * DL_Q)
        k0 = int(np.clip(i * DL_Q - DL_HALF, 0, seq - DL_KA))
        krows = slice(k0, k0 + DL_KA)
        pat = (i * DL_Q - k0) // DL_HALF

        def scores():
            return _dot_nt(qn_ref[qrows, cols], kn_ref[krows, cols]) + bias_a_ref[pat]

        def pv(p):
            return _pv_with_rowsum(p, vn_ref[krows, cols])

        def prior():
            return m_scr[hh, qrows, :], l_scr[hh, qrows, :], acc_scr[hh, qrows, :]

        def finish(l, acc):
            o = (acc / l) * gate_ref[qrows, cols].astype(jnp.float32)
            o_ref[qrows, cols] = o.astype(o_ref.dtype)

        return _Block(scores, pv, finish, prior, None, DL_DEPTH_A, i == 0)

    heads = range(qn_ref.shape[1] // HEAD_DIM)
    _run_pipeline([block_b(hh, r) for hh in heads for r in range(DL_STREAMS)]
                  + [block_a(hh, i) for hh in heads for i in range(seq // DL_Q)])


def _dl_bias_tables(seq):
    (w1, d1), (w4, d4), (w16, d16) = DIL_CONFIGS
    assert d1 == 1 and d4 == DL_STREAMS and d16 % d4 == 0
    assert (w1 // 2) // d1 == DL_HALF and (w4 // 2) // d4 == DL_HALF and (w16 // 2) // d16 == DL_HALF
    r = np.arange(DL_Q)[:, None]
    j = np.arange(DL_KA)[None, :]
    bias_a = np.stack([np.where(np.abs(j - off - r) <= DL_HALF, 0.0, NEG) for off in (0, DL_HALF, 2 * DL_HALF)])
    slen = seq // DL_STREAMS
    delta = np.arange(slen)[None, :] - np.arange(slen)[:, None]
    step = d16 // d4
    mult = (np.abs(delta) <= DL_HALF).astype(np.float64) + ((delta % step == 0) & (np.abs(delta) <= DL_HALF * step))
    bias_b = np.where(mult > 0, np.log2(np.maximum(mult, 1.0)), NEG)
    return jnp.asarray(bias_a, jnp.float32), jnp.asarray(bias_b, jnp.float32)


def _dl_attention(z, zs, bias_a, bias_b, batch, seq):
    n = z.shape[0]
    hc = DL_HPS * HEAD_DIM
    qb, kb, vb, gb = (OFF_DL_Q // hc, OFF_DL_K // hc, OFF_DL_V // hc, OFF_DL_G // hc)
    slen = seq // DL_STREAMS
    steps = N_HEADS_DIL // DL_HPS
    stream_spec = lambda c0: pl.BlockSpec((None, DL_STREAMS, slen, hc), lambda b, h: (b, 0, 0, c0 + h))
    return pl.pallas_call(
        _dl_kernel,
        grid=(batch, steps),
        in_specs=[
            pl.BlockSpec((seq, hc), lambda b, h: (b, qb + h)),
            pl.BlockSpec((seq, hc), lambda b, h: (b, kb + h)),
            pl.BlockSpec((seq, hc), lambda b, h: (b, vb + h)),
            pl.BlockSpec((seq, hc), lambda b, h: (b, gb + h)),
            stream_spec(0), stream_spec(steps), stream_spec(2 * steps),
            pl.BlockSpec(bias_a.shape, lambda b, h: (0, 0, 0)),
            pl.BlockSpec(bias_b.shape, lambda b, h: (0, 0)),
        ],
        out_specs=pl.BlockSpec((seq, hc), lambda b, h: (b, h)),
        out_shape=jax.ShapeDtypeStruct((n, W_DIL), jnp.bfloat16),
        scratch_shapes=[pltpu.VMEM((DL_HPS, seq, HEAD_DIM), jnp.float32)] * 3,
        compiler_params=pltpu.CompilerParams(
            dimension_semantics=("parallel", "parallel"), vmem_limit_bytes=V7X_VMEM_LIMIT),
        name="dl_attention",
    )(z, z, z, z, zs, zs, zs, bias_a, bias_b)


def _mem_kernel(q_ref, k_ref, v_ref, gate_ref, o_ref):
    def block(hh, i):
        cols = _head_cols(hh)
        qrows = slice(i * MEM_Q, (i + 1) * MEM_Q)

        def scores():
            return _dot_nt(q_ref[qrows, cols], k_ref[:, cols])

        def pv(p):
            return _pv_with_rowsum(p, v_ref[:, cols])

        def finish(l, acc):
            o = (acc / l) * gate_ref[qrows, cols].astype(jnp.float32)
            o_ref[qrows, cols] = o.astype(o_ref.dtype)

        return _Block(scores, pv, finish, None, None, MEM_DEPTH, False)

    heads = range(q_ref.shape[1] // HEAD_DIM)
    _run_pipeline([block(hh, i) for hh in heads for i in range(q_ref.shape[0] // MEM_Q)])


def _mem_attention(z, mkv, batch, seq):
    n = z.shape[0]
    hc = MEM_HPS * HEAD_DIM
    qb, gb = OFF_M_Q // hc, OFF_M_G // hc
    kb, vb = 0, W_MEM // hc
    return pl.pallas_call(
        _mem_kernel,
        grid=(batch, N_HEADS_MEM // MEM_HPS),
        in_specs=[
            pl.BlockSpec((seq, hc), lambda b, h: (b, qb + h)),
            pl.BlockSpec((N_MEM, hc), lambda b, h: (b, kb + h)),
            pl.BlockSpec((N_MEM, hc), lambda b, h: (b, vb + h)),
            pl.BlockSpec((seq, hc), lambda b, h: (b, gb + h)),
        ],
        out_specs=pl.BlockSpec((seq, hc), lambda b, h: (b, h)),
        out_shape=jax.ShapeDtypeStruct((n, W_MEM), jnp.bfloat16),
        compiler_params=pltpu.CompilerParams(dimension_semantics=("parallel", "parallel")),
        name="mem_attention",
    )(z, mkv, mkv, z)


def _rope_tables(seq):
    half = HEAD_DIM // 2
    inv = ROPE_THETA ** (-jnp.arange(half, dtype=jnp.float32) / half)
    ang = jnp.arange(seq).astype(jnp.float32)[:, None] * inv[None, :]
    cos, sin = jnp.cos(ang), jnp.sin(ang)
    return jnp.concatenate([cos, cos], axis=-1), jnp.concatenate([-sin, sin], axis=-1)


def kernel(x, mem, norm_g, w_in, na_rpb, mem_norm_g, w_mem_kv, w_out, final_g):
    batch, seq, d = x.shape
    depth = w_in.shape[0]
    assert d == D_MODEL and w_in.shape[2] == IN_COLS and seq % GRID_W == 0
    assert seq % PROJ_TM == 0 and seq % OUT_TM == 0 and seq % NA_Q == 0 and seq % MEM_Q == 0
    assert seq % (DL_STREAMS * DL_Q) == 0 and PROJ_TM % DL_STREAMS == 0
    assert all(off % (NA_HPS * HEAD_DIM) == 0 for off in (OFF_NA_Q, OFF_NA_K, OFF_NA_V, OFF_NA_G))
    assert all(off % (DL_HPS * HEAD_DIM) == 0 for off in (OFF_DL_Q, OFF_DL_K, OFF_DL_V, OFF_DL_G))
    assert all(off % (MEM_HPS * HEAD_DIM) == 0 for off in (OFF_M_Q, OFF_M_G, W_MEM))
    assert N_HEADS_NA % NA_HPS == 0 and N_HEADS_DIL % DL_HPS == 0 and N_HEADS_MEM % MEM_HPS == 0

    cos2, sin2 = _rope_tables(seq)
    dl_bias_a, dl_bias_b = _dl_bias_tables(seq)
    na_toep = _na_toeplitz(na_rpb)
    w_in_b = w_in[0].astype(jnp.bfloat16)[None]
    w_out_b = w_out.astype(jnp.bfloat16)
    w_mem_b = w_mem_kv.astype(jnp.bfloat16)
    mem_f = mem.reshape(batch * N_MEM, d)
    mem_segments = ((0, 2 * W_MEM, "plain", None),)

    xf = x.reshape(batch * seq, d)
    for l in range(depth):
        z, zs = _in_proj(xf, norm_g[l], w_in_b, 0, cos2, sin2, IN_SEGMENTS, batch, seq)
        mkv = _in_proj(mem_f, mem_norm_g, w_mem_b, l, cos2, sin2, mem_segments, batch, N_MEM)
        y_na = _na_attention(z, na_toep, l, batch, seq)
        y_dl = _dl_attention(z, zs, dl_bias_a, dl_bias_b, batch, seq)
        y_m = _mem_attention(z, mkv, batch, seq)
        if l + 1 < depth:
            xf, w_in_b = _out_proj(xf, y_na, y_dl, y_m, w_out_b, l, final_g, False, w_in_f32=w_in)
        else:
            xf = _out_proj(xf, y_na, y_dl, y_m, w_out_b, l, final_g, True)
    return xf.reshape(batch, seq, d)
```

```python
import collections
import functools

import numpy as np
import jax
import jax.numpy as jnp
from jax import lax
from jax.experimental import pallas as pl
from jax.experimental.pallas import tpu as pltpu

D_MODEL = 2048
HEAD_DIM = 128
N_HEADS_NA = 6
N_HEADS_DIL = 6
N_HEADS_MEM = 4
W_NA = N_HEADS_NA * HEAD_DIM
W_DIL = N_HEADS_DIL * HEAD_DIM
W_MEM = N_HEADS_MEM * HEAD_DIM
IN_COLS = 4 * W_NA + 4 * W_DIL + 2 * W_MEM
N_MEM = 256
GRID_W = 64
NA_WIN_ROWS = 8
NA_WIN_COLS = 16
DIL_CONFIGS = ((128, 1), (512, 4), (2048, 16))
ROPE_THETA = 10000.0
EPS = 1e-6
NEG = -1e30
SCALE = HEAD_DIM ** -0.5
LOG2E = float(np.log2(np.e))
Q_SCALE = SCALE * LOG2E

OFF_NA_Q, OFF_NA_K, OFF_NA_V, OFF_NA_G = 0, W_NA, 2 * W_NA, 3 * W_NA
OFF_DL_Q = 4 * W_NA
OFF_DL_K, OFF_DL_V, OFF_DL_G = OFF_DL_Q + W_DIL, OFF_DL_Q + 2 * W_DIL, OFF_DL_Q + 3 * W_DIL
OFF_M_Q = OFF_DL_Q + 4 * W_DIL
OFF_M_G = OFF_M_Q + W_MEM

IN_SEGMENTS = (
    (OFF_NA_Q, W_NA, "scale", None), (OFF_NA_K, W_NA, "plain", None),
    (OFF_NA_V, W_NA, "plain", None), (OFF_NA_G, W_NA, "silu", None),
    (OFF_DL_Q, W_DIL, "rope_scale", 0), (OFF_DL_K, W_DIL, "rope", W_DIL),
    (OFF_DL_V, W_DIL, "plain", 2 * W_DIL), (OFF_DL_G, W_DIL, "silu", None),
    (OFF_M_Q, W_MEM, "scale", None), (OFF_M_G, W_MEM, "silu", None),
)

V7X_VMEM_LIMIT = 62 * 1024 * 1024

PROJ_TM = 512
OUT_TM = 512
MEM_PROJ_TM = 512
NA_HPS, DL_HPS, MEM_HPS = 6, 3, 4
NA_ROWS_PER_GROUP = 4
NA_Q = NA_ROWS_PER_GROUP * GRID_W
NA_KEY_ROWS = NA_ROWS_PER_GROUP + NA_WIN_ROWS
NA_K = NA_KEY_ROWS * GRID_W
NA_DEPTH = 2
MEM_Q = 512
MEM_DEPTH = 1

DL_STREAMS = 4
DL_HALF = 64
DL_Q = 128
DL_KA = DL_Q + 2 * DL_HALF
DL_DEPTH_A = 4
DL_DEPTH_B = 2


def _rms_norm_f32(x, g):
    return x * lax.rsqrt(jnp.mean(x * x, axis=-1, keepdims=True) + EPS) * g


def _dot_nt(a, b):
    return lax.dot_general(a, b, (((1,), (1,)), ((), ())), preferred_element_type=jnp.float32)


def _pv_with_rowsum(p, v):
    ones = jnp.ones((v.shape[0], HEAD_DIM), v.dtype)
    r = jnp.dot(p, jnp.concatenate([v, ones], axis=1), preferred_element_type=jnp.float32)
    return r[:, :HEAD_DIM], r[:, HEAD_DIM:]


def _head_cols(hh):
    return slice(hh * HEAD_DIM, (hh + 1) * HEAD_DIM)


def _in_proj_kernel(x_ref, g_ref, w_ref, cos_ref, sin_ref, o_ref, *rest, segments):
    if rest:
        s_ref, scr = rest
    tm = x_ref.shape[0]
    h = _rms_norm_f32(x_ref[...], g_ref[...]).astype(jnp.bfloat16)

    def emit(a, c0, width, s0):
        o_ref[:, c0:c0 + width] = a.astype(o_ref.dtype)
        if s0 is not None:
            for j in range(0, width, HEAD_DIM):
                scr[j // HEAD_DIM] = a[:, j:j + HEAD_DIM]
                for r in range(DL_STREAMS):
                    rows = scr[j // HEAD_DIM, pl.ds(r, tm // DL_STREAMS, stride=DL_STREAMS), :]
                    s_ref[r, :, s0 + j:s0 + j + HEAD_DIM] = rows.astype(s_ref.dtype)

    for off, width, kind, s_off in segments:
        a = jnp.dot(h, w_ref[:, off:off + width], preferred_element_type=jnp.float32)
        if kind in ("rope", "rope_scale"):
            for j in range(0, width, HEAD_DIM):
                aj = a[:, j:j + HEAD_DIM]
                aj = aj * cos_ref[...] + pltpu.roll(aj, HEAD_DIM // 2, axis=1) * sin_ref[...]
                if kind == "rope_scale":
                    aj = aj * Q_SCALE
                emit(aj, off + j, HEAD_DIM, None if s_off is None else s_off + j)
        else:
            if kind == "scale":
                a = a * Q_SCALE
            elif kind == "silu":
                a = a * (1.0 / (1.0 + jnp.exp(-a)))
            emit(a, off, width, s_off)


def _in_proj(x, g, w_bf16, layer, cos2, sin2, segments, batch, seq):
    n, d = x.shape
    cols = w_bf16.shape[2]
    tm = PROJ_TM
    bps = seq // tm
    streamed = any(s[3] is not None for s in segments)
    out_specs = [pl.BlockSpec((tm, cols), lambda i: (i, 0))]
    out_shape = [jax.ShapeDtypeStruct((n, cols), jnp.bfloat16)]
    scratch = []
    if streamed:
        out_specs.append(pl.BlockSpec((None, DL_STREAMS, tm // DL_STREAMS, 3 * W_DIL),
                                      lambda i: (i // bps, 0, i % bps, 0)))
        out_shape.append(jax.ShapeDtypeStruct((batch, DL_STREAMS, seq // DL_STREAMS, 3 * W_DIL), jnp.bfloat16))
        scratch.append(pltpu.VMEM((W_DIL // HEAD_DIM, tm, HEAD_DIM), jnp.float32))
    res = pl.pallas_call(
        functools.partial(_in_proj_kernel, segments=segments),
        grid=(n // tm,),
        in_specs=[
            pl.BlockSpec((tm, d), lambda i: (i, 0)),
            pl.BlockSpec((1, d), lambda i: (0, 0)),
            pl.BlockSpec((None, d, cols), lambda i: (layer, 0, 0), pipeline_mode=pl.Buffered(1)),
            pl.BlockSpec((tm, HEAD_DIM), lambda i: (i % bps, 0)),
            pl.BlockSpec((tm, HEAD_DIM), lambda i: (i % bps, 0)),
        ],
        out_specs=out_specs,
        out_shape=out_shape,
        scratch_shapes=scratch,
        compiler_params=pltpu.CompilerParams(
            dimension_semantics=("parallel",), vmem_limit_bytes=V7X_VMEM_LIMIT),
        name="in_proj",
    )(x, g.reshape(1, d), w_bf16, cos2, sin2)
    return res if streamed else res[0]


def _mem_proj_kernel(x_ref, g_ref, w_ref, *rest):
    n_casts = (len(rest) - 1) // 2
    o_ref = rest[n_casts]
    for src_ref, dst_ref in zip(rest[:n_casts], rest[n_casts + 1:]):
        dst_ref[...] = src_ref[...].astype(dst_ref.dtype)
    h = _rms_norm_f32(x_ref[...], g_ref[...]).astype(jnp.bfloat16)
    o_ref[...] = jnp.dot(h, w_ref[...], preferred_element_type=jnp.float32).astype(o_ref.dtype)


def _mem_proj(mem, g, w_bf16, layer, casts):
    n, d = mem.shape
    cols = w_bf16.shape[2]
    tm = MEM_PROJ_TM
    steps = n // tm
    in_specs = [
        pl.BlockSpec((tm, d), lambda i: (i, 0)),
        pl.BlockSpec((1, d), lambda i: (0, 0)),
        pl.BlockSpec((None, d, cols), lambda i: (layer, 0, 0), pipeline_mode=pl.Buffered(1)),
    ]
    out_specs = [pl.BlockSpec((tm, cols), lambda i: (i, 0))]
    out_shape = [jax.ShapeDtypeStruct((n, cols), jnp.bfloat16)]
    for w, idx in casts:
        _, r, c = w.shape
        assert r % steps == 0
        in_specs.append(pl.BlockSpec((None, r // steps, c), lambda i, idx=idx: (idx, i, 0)))
        out_specs.append(pl.BlockSpec((r // steps, c), lambda i: (i, 0)))
        out_shape.append(jax.ShapeDtypeStruct((r, c), jnp.bfloat16))
    return pl.pallas_call(
        _mem_proj_kernel,
        grid=(steps,),
        in_specs=in_specs,
        out_specs=out_specs,
        out_shape=out_shape,
        compiler_params=pltpu.CompilerParams(
            dimension_semantics=("parallel",), vmem_limit_bytes=V7X_VMEM_LIMIT),
        name="mem_proj",
    )(mem, g.reshape(1, d), w_bf16, *[w for w, _ in casts])


def _out_proj_kernel(x_ref, yna_ref, ydl_ref, ym_ref, w_ref, g_ref, *rest, final_norm):
    if len(rest) == 3:
        wnext_ref, o_ref, wnext_bf16_ref = rest
        wnext_bf16_ref[...] = wnext_ref[...].astype(wnext_bf16_ref.dtype)
    else:
        (o_ref,) = rest
    acc = jnp.dot(yna_ref[...], w_ref[0:W_NA, :], preferred_element_type=jnp.float32)
    acc += jnp.dot(ydl_ref[...], w_ref[W_NA:W_NA + W_DIL, :], preferred_element_type=jnp.float32)
    acc += jnp.dot(ym_ref[...], w_ref[W_NA + W_DIL:, :], preferred_element_type=jnp.float32)
    x = x_ref[...] + acc
    if final_norm:
        x = _rms_norm_f32(x, g_ref[...])
    o_ref[...] = x


def _out_proj(x, y_na, y_dl, y_m, w_bf16, final_g, final_norm, w_in_f32=None, next_layer=None):
    n, d = x.shape
    tm = OUT_TM
    steps = n // tm
    in_specs = [
        pl.BlockSpec((tm, d), lambda i: (i, 0)),
        pl.BlockSpec((tm, W_NA), lambda i: (i, 0)),
        pl.BlockSpec((tm, W_DIL), lambda i: (i, 0)),
        pl.BlockSpec((tm, W_MEM), lambda i: (i, 0)),
        pl.BlockSpec((d, d), lambda i: (0, 0), pipeline_mode=pl.Buffered(1)),
        pl.BlockSpec((1, d), lambda i: (0, 0)),
    ]
    out_specs = [pl.BlockSpec((tm, d), lambda i: (i, 0))]
    out_shape = [jax.ShapeDtypeStruct((n, d), jnp.float32)]
    args = [x, y_na, y_dl, y_m, w_bf16, final_g.reshape(1, d)]
    if w_in_f32 is not None:
        cols = w_in_f32.shape[2]
        assert d % steps == 0
        slab = d // steps
        in_specs.append(pl.BlockSpec((None, slab, cols), lambda i: (next_layer, i, 0)))
        out_specs.append(pl.BlockSpec((None, slab, cols), lambda i: (0, i, 0)))
        out_shape.append(jax.ShapeDtypeStruct((1, d, cols), jnp.bfloat16))
        args.append(w_in_f32)
    res = pl.pallas_call(
        functools.partial(_out_proj_kernel, final_norm=final_norm),
        grid=(steps,),
        in_specs=in_specs,
        out_specs=out_specs,
        out_shape=out_shape,
        compiler_params=pltpu.CompilerParams(
            dimension_semantics=("parallel",), vmem_limit_bytes=V7X_VMEM_LIMIT),
        name="out_proj",
    )(*args)
    return res if w_in_f32 is not None else res[0]


_Block = collections.namedtuple("_Block", "scores pv finish prior stats depth flush")


def _run_pipeline(blocks):
    issued = []
    pending = None
    for i, blk in enumerate(blocks):
        while len(issued) < min(i + 1 + blk.depth, len(blocks)):
            issued.append(blocks[len(issued)].scores())
        s = issued[i]
        issued[i] = None
        if blk.flush and pending is not None:
            pending[0].finish(*pending[1:])
            pending = None
        m = jnp.max(s, axis=-1, keepdims=True)
        if blk.stats is not None:
            blk.stats(m)
        if blk.prior is None:
            p = jnp.exp2(s - m)
            acc, l = blk.pv(p.astype(jnp.bfloat16))
        else:
            m_b, l_b, acc_b = blk.prior()
            m = jnp.maximum(m, m_b)
            p = jnp.exp2(s - jnp.concatenate([m] * (s.shape[1] // m.shape[1]), axis=1))
            alpha = jnp.exp2(m_b - m)
            acc, l = blk.pv(p.astype(jnp.bfloat16))
            l = alpha * l_b + l
            acc = alpha * acc_b + acc
        if pending is not None:
            pending[0].finish(*pending[1:])
        pending = (blk, l, acc)
    pending[0].finish(*pending[1:])


def _na_kernel(q_ref, k_ref, v_ref, gate_ref, toep_ref, o_ref, bias_ref):
    n_groups = q_ref.shape[0] // NA_Q
    rows = q_ref.shape[0] // GRID_W

    def key_start(g):
        return int(np.clip(g * NA_ROWS_PER_GROUP - NA_WIN_ROWS // 2, 0, rows - NA_KEY_ROWS)) * GRID_W

    @pl.when(pl.program_id(1) == 0)
    def _():
        outside = jnp.full((GRID_W, GRID_W), NEG, jnp.float32)
        for hh in range(q_ref.shape[1] // HEAD_DIM):
            for pat, g in enumerate((0, 1, n_groups - 1)):
                base = key_start(g) // GRID_W
                for qi in range(NA_ROWS_PER_GROUP):
                    r = g * NA_ROWS_PER_GROUP + qi
                    rs = int(np.clip(r - NA_WIN_ROWS // 2, 0, rows - NA_WIN_ROWS))
                    for kj in range(0, NA_KEY_ROWS, 2):
                        pair = [toep_ref[hh, base + kk - r + NA_WIN_ROWS - 1]
                                if rs <= base + kk < rs + NA_WIN_ROWS else outside for kk in (kj, kj + 1)]
                        bias_ref[hh, pat, qi * GRID_W:(qi + 1) * GRID_W, kj * GRID_W:(kj + 2) * GRID_W] = (
                            jnp.concatenate(pair, axis=1))

    def block(hh, g):
        cols = _head_cols(hh)
        qrows = slice(g * NA_Q, (g + 1) * NA_Q)
        krows = slice(key_start(g), key_start(g) + NA_K)
        pat = 0 if g == 0 else (2 if g == n_groups - 1 else 1)

        def scores():
            return _dot_nt(q_ref[qrows, cols], k_ref[krows, cols]) + bias_ref[hh, pat]

        def pv(p):
            return _pv_with_rowsum(p, v_ref[krows, cols])

        def finish(l, acc):
            o = (acc / l) * gate_ref[qrows, cols].astype(jnp.float32)
            o_ref[qrows, cols] = o.astype(o_ref.dtype)

        return _Block(scores, pv, finish, None, None, NA_DEPTH, False)

    _run_pipeline([block(hh, g) for hh in range(q_ref.shape[1] // HEAD_DIM) for g in range(n_groups)])


def _na_toeplitz(rpb):
    c = np.arange(GRID_W)
    cs = np.clip(c - NA_WIN_COLS // 2, 0, GRID_W - NA_WIN_COLS)
    col_ok = (c[None, :] >= cs[:, None]) & (c[None, :] < cs[:, None] + NA_WIN_COLS)
    dc = np.clip(c[None, :] - c[:, None], -(NA_WIN_COLS - 1), NA_WIN_COLS - 1) + NA_WIN_COLS - 1
    onehot = (dc[None] == np.arange(2 * NA_WIN_COLS - 1)[:, None, None]).astype(np.float32)
    toep = jnp.einsum("lhaj,jqk->lhaqk", rpb.astype(jnp.float32), jnp.asarray(onehot),
                      precision=lax.Precision.HIGHEST)
    return jnp.where(col_ok, toep * LOG2E, NEG)


def _na_attention(z, toep, layer, batch, seq):
    n = z.shape[0]
    hc = NA_HPS * HEAD_DIM
    qb, kb, vb, gb = (OFF_NA_Q // hc, OFF_NA_K // hc, OFF_NA_V // hc, OFF_NA_G // hc)
    return pl.pallas_call(
        _na_kernel,
        grid=(N_HEADS_NA // NA_HPS, batch),
        in_specs=[
            pl.BlockSpec((seq, hc), lambda h, b: (b, qb + h)),
            pl.BlockSpec((seq, hc), lambda h, b: (b, kb + h)),
            pl.BlockSpec((seq, hc), lambda h, b: (b, vb + h)),
            pl.BlockSpec((seq, hc), lambda h, b: (b, gb + h)),
            pl.BlockSpec((None, NA_HPS) + toep.shape[2:], lambda h, b: (layer, h, 0, 0, 0)),
        ],
        out_specs=pl.BlockSpec((seq, hc), lambda h, b: (b, h)),
        out_shape=jax.ShapeDtypeStruct((n, W_NA), jnp.bfloat16),
        scratch_shapes=[pltpu.VMEM((NA_HPS, 3, NA_Q, NA_K), jnp.float32)],
        compiler_params=pltpu.CompilerParams(
            dimension_semantics=("parallel", "arbitrary"), vmem_limit_bytes=V7X_VMEM_LIMIT),
        name="na_attention",
    )(z, z, z, z, toep)


def _dl_kernel(qn_ref, kn_ref, vn_ref, gate_ref, qs_ref, ks_ref, vs_ref, bias_a_ref, bias_b_ref, o_ref,
               m_scr, l_scr, acc_scr):
    seq = qn_ref.shape[0]
    slen = seq // DL_STREAMS

    def block_b(hh, r):
        cols = _head_cols(hh)
        rows = pl.ds(r, slen, stride=DL_STREAMS)

        def scores():
            return _dot_nt(qs_ref[r, :, cols], ks_ref[r, :, cols]) + bias_b_ref[...]

        def pv(p):
            return _pv_with_rowsum(p, vs_ref[r, :, cols])

        def stats(m):
            m_scr[hh, rows, :] = jnp.broadcast_to(m, (slen, HEAD_DIM))

        def finish(l, acc):
            l_scr[hh, rows, :] = l
            acc_scr[hh, rows, :] = acc

        return _Block(scores, pv, finish, None, stats, DL_DEPTH_B, False)

    def block_a(hh, i):
        cols = _head_cols(hh)
        qrows = slice(i * DL_Q, (i + 1) * DL_Q)
        k0 = int(np.clip(i * DL_Q - DL_HALF, 0, seq - DL_KA))
        krows = slice(k0, k0 + DL_KA)
        pat = (i * DL_Q - k0) // DL_HALF

        def scores():
            return _dot_nt(qn_ref[qrows, cols], kn_ref[krows, cols]) + bias_a_ref[pat]

        def pv(p):
            return _pv_with_rowsum(p, vn_ref[krows, cols])

        def prior():
            return m_scr[hh, qrows, :], l_scr[hh, qrows, :], acc_scr[hh, qrows, :]

        def finish(l, acc):
            o = (acc / l) * gate_ref[qrows, cols].astype(jnp.float32)
            o_ref[qrows, cols] = o.astype(o_ref.dtype)

        return _Block(scores, pv, finish, prior, None, DL_DEPTH_A, i == 0)

    heads = range(qn_ref.shape[1] // HEAD_DIM)
    _run_pipeline([block_b(hh, r) for hh in heads for r in range(DL_STREAMS)]
                  + [block_a(hh, i) for hh in heads for i in range(seq // DL_Q)])


def _dl_bias_tables(seq):
    (w1, d1), (w4, d4), (w16, d16) = DIL_CONFIGS
    assert d1 == 1 and d4 == DL_STREAMS and d16 % d4 == 0
    assert (w1 // 2) // d1 == DL_HALF and (w4 // 2) // d4 == DL_HALF and (w16 // 2) // d16 == DL_HALF
    r = np.arange(DL_Q)[:, None]
    j = np.arange(DL_KA)[None, :]
    bias_a = np.stack([np.where(np.abs(j - off - r) <= DL_HALF, 0.0, NEG) for off in (0, DL_HALF, 2 * DL_HALF)])
    slen = seq // DL_STREAMS
    delta = np.arange(slen)[None, :] - np.arange(slen)[:, None]
    step = d16 // d4
    mult = (np.abs(delta) <= DL_HALF).astype(np.float64) + ((delta % step == 0) & (np.abs(delta) <= DL_HALF * step))
    bias_b = np.where(mult > 0, np.log2(np.maximum(mult, 1.0)), NEG)
    return jnp.asarray(bias_a, jnp.float32), jnp.asarray(bias_b, jnp.float32)


def _dl_attention(z, zs, bias_a, bias_b, batch, seq):
    n = z.shape[0]
    hc = DL_HPS * HEAD_DIM
    qb, kb, vb, gb = (OFF_DL_Q // hc, OFF_DL_K // hc, OFF_DL_V // hc, OFF_DL_G // hc)
    slen = seq // DL_STREAMS
    steps = N_HEADS_DIL // DL_HPS
    stream_spec = lambda c0: pl.BlockSpec((None, DL_STREAMS, slen, hc), lambda b, h: (b, 0, 0, c0 + h))
    return pl.pallas_call(
        _dl_kernel,
        grid=(batch, steps),
        in_specs=[
            pl.BlockSpec((seq, hc), lambda b, h: (b, qb + h)),
            pl.BlockSpec((seq, hc), lambda b, h: (b, kb + h)),
            pl.BlockSpec((seq, hc), lambda b, h: (b, vb + h)),
            pl.BlockSpec((seq, hc), lambda b, h: (b, gb + h)),
            stream_spec(0), stream_spec(steps), stream_spec(2 * steps),
            pl.BlockSpec(bias_a.shape, lambda b, h: (0, 0, 0)),
            pl.BlockSpec(bias_b.shape, lambda b, h: (0, 0)),
        ],
        out_specs=pl.BlockSpec((seq, hc), lambda b, h: (b, h)),
        out_shape=jax.ShapeDtypeStruct((n, W_DIL), jnp.bfloat16),
        scratch_shapes=[pltpu.VMEM((DL_HPS, seq, HEAD_DIM), jnp.float32)] * 3,
        compiler_params=pltpu.CompilerParams(
            dimension_semantics=("parallel", "parallel"), vmem_limit_bytes=V7X_VMEM_LIMIT),
        name="dl_attention",
    )(z, z, z, z, zs, zs, zs, bias_a, bias_b)


def _mem_kernel(q_ref, k_ref, v_ref, gate_ref, o_ref):
    def block(hh, i):
        cols = _head_cols(hh)
        qrows = slice(i * MEM_Q, (i + 1) * MEM_Q)

        def scores():
            return _dot_nt(q_ref[qrows, cols], k_ref[:, cols])

        def pv(p):
            return _pv_with_rowsum(p, v_ref[:, cols])

        def finish(l, acc):
            o = (acc / l) * gate_ref[qrows, cols].astype(jnp.float32)
            o_ref[qrows, cols] = o.astype(o_ref.dtype)

        return _Block(scores, pv, finish, None, None, MEM_DEPTH, False)

    heads = range(q_ref.shape[1] // HEAD_DIM)
    _run_pipeline([block(hh, i) for hh in heads for i in range(q_ref.shape[0] // MEM_Q)])


def _mem_attention(z, mkv, batch, seq):
    n = z.shape[0]
    hc = MEM_HPS * HEAD_DIM
    qb, gb = OFF_M_Q // hc, OFF_M_G // hc
    kb, vb = 0, W_MEM // hc
    return pl.pallas_call(
        _mem_kernel,
        grid=(batch, N_HEADS_MEM // MEM_HPS),
        in_specs=[
            pl.BlockSpec((seq, hc), lambda b, h: (b, qb + h)),
            pl.BlockSpec((N_MEM, hc), lambda b, h: (b, kb + h)),
            pl.BlockSpec((N_MEM, hc), lambda b, h: (b, vb + h)),
            pl.BlockSpec((seq, hc), lambda b, h: (b, gb + h)),
        ],
        out_specs=pl.BlockSpec((seq, hc), lambda b, h: (b, h)),
        out_shape=jax.ShapeDtypeStruct((n, W_MEM), jnp.bfloat16),
        compiler_params=pltpu.CompilerParams(dimension_semantics=("parallel", "parallel")),
        name="mem_attention",
    )(z, mkv, mkv, z)


def _rope_tables(seq):
    half = HEAD_DIM // 2
    inv = ROPE_THETA ** (-jnp.arange(half, dtype=jnp.float32) / half)
    ang = jnp.arange(seq).astype(jnp.float32)[:, None] * inv[None, :]
    cos, sin = jnp.cos(ang), jnp.sin(ang)
    return jnp.concatenate([cos, cos], axis=-1), jnp.concatenate([-sin, sin], axis=-1)


def kernel(x, mem, norm_g, w_in, na_rpb, mem_norm_g, w_mem_kv, w_out, final_g):
    batch, seq, d = x.shape
    depth = w_in.shape[0]
    assert d == D_MODEL and w_in.shape[2] == IN_COLS and seq % GRID_W == 0
    assert seq % PROJ_TM == 0 and seq % OUT_TM == 0 and seq % NA_Q == 0 and seq % MEM_Q == 0
    assert seq % (DL_STREAMS * DL_Q) == 0 and PROJ_TM % DL_STREAMS == 0
    assert all(off % (NA_HPS * HEAD_DIM) == 0 for off in (OFF_NA_Q, OFF_NA_K, OFF_NA_V, OFF_NA_G))
    assert all(off % (DL_HPS * HEAD_DIM) == 0 for off in (OFF_DL_Q, OFF_DL_K, OFF_DL_V, OFF_DL_G))
    assert all(off % (MEM_HPS * HEAD_DIM) == 0 for off in (OFF_M_Q, OFF_M_G, W_MEM))
    assert N_HEADS_NA % NA_HPS == 0 and N_HEADS_DIL % DL_HPS == 0 and N_HEADS_MEM % MEM_HPS == 0

    cos2, sin2 = _rope_tables(seq)
    dl_bias_a, dl_bias_b = _dl_bias_tables(seq)
    na_toep = _na_toeplitz(na_rpb)
    w_mem_b = w_mem_kv.astype(jnp.bfloat16)
    mem_f = mem.reshape(batch * N_MEM, d)

    xf = x.reshape(batch * seq, d)
    w_in_b = None
    for l in range(depth):
        casts = [(w_out, l)] + ([(w_in, 0)] if l == 0 else [])
        mkv, w_out_b, *first = _mem_proj(mem_f, mem_norm_g, w_mem_b, l, casts)
        if first:
            w_in_b = first[0][None]
        z, zs = _in_proj(xf, norm_g[l], w_in_b, 0, cos2, sin2, IN_SEGMENTS, batch, seq)
        y_na = _na_attention(z, na_toep, l, batch, seq)
        y_dl = _dl_attention(z, zs, dl_bias_a, dl_bias_b, batch, seq)
        y_m = _mem_attention(z, mkv, batch, seq)
        if l + 1 < depth:
            xf, w_in_b = _out_proj(xf, y_na, y_dl, y_m, w_out_b, final_g, False, w_in_f32=w_in, next_layer=l + 1)
        else:
            xf = _out_proj(xf, y_na, y_dl, y_m, w_out_b, final_g, True)
    return xf.reshape(batch, seq, d)
```

```python
import collections
import functools

import numpy as np
import jax
import jax.numpy as jnp
from jax import lax
from jax.experimental import pallas as pl
from jax.experimental.pallas import tpu as pltpu

D_MODEL = 2048
HEAD_DIM = 128
N_HEADS_NA = 6
N_HEADS_DIL = 6
N_HEADS_MEM = 4
W_NA = N_HEADS_NA * HEAD_DIM
W_DIL = N_HEADS_DIL * HEAD_DIM
W_MEM = N_HEADS_MEM * HEAD_DIM
IN_COLS = 4 * W_NA + 4 * W_DIL + 2 * W_MEM
N_MEM = 256
GRID_W = 64
NA_WIN_ROWS = 8
NA_WIN_COLS = 16
DIL_CONFIGS = ((128, 1), (512, 4), (2048, 16))
ROPE_THETA = 10000.0
EPS = 1e-6
NEG = -1e30
SCALE = HEAD_DIM ** -0.5
LOG2E = float(np.log2(np.e))
Q_SCALE = SCALE * LOG2E

OFF_NA_Q, OFF_NA_K, OFF_NA_V, OFF_NA_G = 0, W_NA, 2 * W_NA, 3 * W_NA
OFF_DL_Q = 4 * W_NA
OFF_DL_K, OFF_DL_V, OFF_DL_G = OFF_DL_Q + W_DIL, OFF_DL_Q + 2 * W_DIL, OFF_DL_Q + 3 * W_DIL
OFF_M_Q = OFF_DL_Q + 4 * W_DIL
OFF_M_G = OFF_M_Q + W_MEM

IN_SEGMENTS = (
    (OFF_NA_Q, W_NA, "scale", None), (OFF_NA_K, W_NA, "plain", None),
    (OFF_NA_V, W_NA, "plain", None), (OFF_NA_G, W_NA, "silu", None),
    (OFF_DL_Q, W_DIL, "rope_scale", 0), (OFF_DL_K, W_DIL, "rope", W_DIL),
    (OFF_DL_V, W_DIL, "plain", 2 * W_DIL), (OFF_DL_G, W_DIL, "silu", None),
    (OFF_M_Q, W_MEM, "scale", None), (OFF_M_G, W_MEM, "silu", None),
)

V7X_VMEM_LIMIT = 62 * 1024 * 1024

PROJ_TM = 512
OUT_TM = 512
MEM_PROJ_TM = 512
NA_HPS, DL_HPS = 6, 3
MEM_HPS = N_HEADS_MEM * DL_HPS // N_HEADS_DIL
NA_ROWS_PER_GROUP = 4
NA_Q = NA_ROWS_PER_GROUP * GRID_W
NA_KEY_ROWS = NA_ROWS_PER_GROUP + NA_WIN_ROWS
NA_K = NA_KEY_ROWS * GRID_W
NA_DEPTH = 2
MEM_Q = 512
MEM_DEPTH = 1

DL_STREAMS = 4
DL_HALF = 64
DL_Q = 128
DL_KA = DL_Q + 2 * DL_HALF
DL_DEPTH_A = 4
DL_DEPTH_B = 2


def _rms_norm_f32(x, g):
    return x * lax.rsqrt(jnp.mean(x * x, axis=-1, keepdims=True) + EPS) * g


def _dot_nt(a, b):
    return lax.dot_general(a, b, (((1,), (1,)), ((), ())), preferred_element_type=jnp.float32)


def _pv_with_rowsum(p, v):
    ones = jnp.ones((v.shape[0], HEAD_DIM), v.dtype)
    r = jnp.dot(p, jnp.concatenate([v, ones], axis=1), preferred_element_type=jnp.float32)
    return r[:, :HEAD_DIM], r[:, HEAD_DIM:]


def _head_cols(hh):
    return slice(hh * HEAD_DIM, (hh + 1) * HEAD_DIM)


def _in_proj_kernel(x_ref, g_ref, w_ref, cos_ref, sin_ref, o_ref, *rest, segments):
    if rest:
        s_ref, scr = rest
    tm = x_ref.shape[0]
    h = _rms_norm_f32(x_ref[...], g_ref[...]).astype(jnp.bfloat16)

    def emit(a, c0, width, s0):
        o_ref[:, c0:c0 + width] = a.astype(o_ref.dtype)
        if s0 is not None:
            for j in range(0, width, HEAD_DIM):
                scr[j // HEAD_DIM] = a[:, j:j + HEAD_DIM]
                for r in range(DL_STREAMS):
                    rows = scr[j // HEAD_DIM, pl.ds(r, tm // DL_STREAMS, stride=DL_STREAMS), :]
                    s_ref[r, :, s0 + j:s0 + j + HEAD_DIM] = rows.astype(s_ref.dtype)

    for off, width, kind, s_off in segments:
        a = jnp.dot(h, w_ref[:, off:off + width], preferred_element_type=jnp.float32)
        if kind in ("rope", "rope_scale"):
            for j in range(0, width, HEAD_DIM):
                aj = a[:, j:j + HEAD_DIM]
                aj = aj * cos_ref[...] + pltpu.roll(aj, HEAD_DIM // 2, axis=1) * sin_ref[...]
                if kind == "rope_scale":
                    aj = aj * Q_SCALE
                emit(aj, off + j, HEAD_DIM, None if s_off is None else s_off + j)
        else:
            if kind == "scale":
                a = a * Q_SCALE
            elif kind == "silu":
                a = a * (1.0 / (1.0 + jnp.exp(-a)))
            emit(a, off, width, s_off)


def _in_proj(x, g, w_bf16, layer, cos2, sin2, segments, batch, seq):
    n, d = x.shape
    cols = w_bf16.shape[2]
    tm = PROJ_TM
    bps = seq // tm
    streamed = any(s[3] is not None for s in segments)
    out_specs = [pl.BlockSpec((tm, cols), lambda i: (i, 0))]
    out_shape = [jax.ShapeDtypeStruct((n, cols), jnp.bfloat16)]
    scratch = []
    if streamed:
        out_specs.append(pl.BlockSpec((None, DL_STREAMS, tm // DL_STREAMS, 3 * W_DIL),
                                      lambda i: (i // bps, 0, i % bps, 0)))
        out_shape.append(jax.ShapeDtypeStruct((batch, DL_STREAMS, seq // DL_STREAMS, 3 * W_DIL), jnp.bfloat16))
        scratch.append(pltpu.VMEM((W_DIL // HEAD_DIM, tm, HEAD_DIM), jnp.float32))
    res = pl.pallas_call(
        functools.partial(_in_proj_kernel, segments=segments),
        grid=(n // tm,),
        in_specs=[
            pl.BlockSpec((tm, d), lambda i: (i, 0)),
            pl.BlockSpec((1, d), lambda i: (0, 0)),
            pl.BlockSpec((None, d, cols), lambda i: (layer, 0, 0), pipeline_mode=pl.Buffered(1)),
            pl.BlockSpec((tm, HEAD_DIM), lambda i: (i % bps, 0)),
            pl.BlockSpec((tm, HEAD_DIM), lambda i: (i % bps, 0)),
        ],
        out_specs=out_specs,
        out_shape=out_shape,
        scratch_shapes=scratch,
        compiler_params=pltpu.CompilerParams(
            dimension_semantics=("parallel",), vmem_limit_bytes=V7X_VMEM_LIMIT),
        name="in_proj",
    )(x, g.reshape(1, d), w_bf16, cos2, sin2)
    return res if streamed else res[0]


def _mem_proj_kernel(x_ref, g_ref, w_ref, *rest):
    n_casts = (len(rest) - 1) // 2
    o_ref = rest[n_casts]
    for src_ref, dst_ref in zip(rest[:n_casts], rest[n_casts + 1:]):
        dst_ref[...] = src_ref[...].astype(dst_ref.dtype)
    h = _rms_norm_f32(x_ref[...], g_ref[...]).astype(jnp.bfloat16)
    o_ref[...] = jnp.dot(h, w_ref[...], preferred_element_type=jnp.float32).astype(o_ref.dtype)


def _mem_proj(mem, g, w_bf16, layer, casts):
    n, d = mem.shape
    cols = w_bf16.shape[2]
    tm = MEM_PROJ_TM
    steps = n // tm
    in_specs = [
        pl.BlockSpec((tm, d), lambda i: (i, 0)),
        pl.BlockSpec((1, d), lambda i: (0, 0)),
        pl.BlockSpec((None, d, cols), lambda i: (layer, 0, 0), pipeline_mode=pl.Buffered(1)),
    ]
    out_specs = [pl.BlockSpec((tm, cols), lambda i: (i, 0))]
    out_shape = [jax.ShapeDtypeStruct((n, cols), jnp.bfloat16)]
    for w, idx in casts:
        _, r, c = w.shape
        assert r % steps == 0
        in_specs.append(pl.BlockSpec((None, r // steps, c), lambda i, idx=idx: (idx, i, 0)))
        out_specs.append(pl.BlockSpec((r // steps, c), lambda i: (i, 0)))
        out_shape.append(jax.ShapeDtypeStruct((r, c), jnp.bfloat16))
    return pl.pallas_call(
        _mem_proj_kernel,
        grid=(steps,),
        in_specs=in_specs,
        out_specs=out_specs,
        out_shape=out_shape,
        compiler_params=pltpu.CompilerParams(
            dimension_semantics=("parallel",), vmem_limit_bytes=V7X_VMEM_LIMIT),
        name="mem_proj",
    )(mem, g.reshape(1, d), w_bf16, *[w for w, _ in casts])


def _out_proj_kernel(x_ref, yna_ref, ydl_ref, ym_ref, w_ref, g_ref, *rest, final_norm):
    if len(rest) == 3:
        wnext_ref, o_ref, wnext_bf16_ref = rest
        wnext_bf16_ref[...] = wnext_ref[...].astype(wnext_bf16_ref.dtype)
    else:
        (o_ref,) = rest
    acc = jnp.dot(yna_ref[...], w_ref[0:W_NA, :], preferred_element_type=jnp.float32)
    acc += jnp.dot(ydl_ref[...], w_ref[W_NA:W_NA + W_DIL, :], preferred_element_type=jnp.float32)
    acc += jnp.dot(ym_ref[...], w_ref[W_NA + W_DIL:, :], preferred_element_type=jnp.float32)
    x = x_ref[...] + acc
    if final_norm:
        x = _rms_norm_f32(x, g_ref[...])
    o_ref[...] = x


def _out_proj(x, y_na, y_dl, y_m, w_bf16, final_g, final_norm, w_in_f32=None, next_layer=None):
    n, d = x.shape
    tm = OUT_TM
    steps = n // tm
    in_specs = [
        pl.BlockSpec((tm, d), lambda i: (i, 0)),
        pl.BlockSpec((tm, W_NA), lambda i: (i, 0)),
        pl.BlockSpec((tm, W_DIL), lambda i: (i, 0)),
        pl.BlockSpec((tm, W_MEM), lambda i: (i, 0)),
        pl.BlockSpec((d, d), lambda i: (0, 0), pipeline_mode=pl.Buffered(1)),
        pl.BlockSpec((1, d), lambda i: (0, 0)),
    ]
    out_specs = [pl.BlockSpec((tm, d), lambda i: (i, 0))]
    out_shape = [jax.ShapeDtypeStruct((n, d), jnp.float32)]
    args = [x, y_na, y_dl, y_m, w_bf16, final_g.reshape(1, d)]
    if w_in_f32 is not None:
        cols = w_in_f32.shape[2]
        assert d % steps == 0
        slab = d // steps
        in_specs.append(pl.BlockSpec((None, slab, cols), lambda i: (next_layer, i, 0)))
        out_specs.append(pl.BlockSpec((None, slab, cols), lambda i: (0, i, 0)))
        out_shape.append(jax.ShapeDtypeStruct((1, d, cols), jnp.bfloat16))
        args.append(w_in_f32)
    res = pl.pallas_call(
        functools.partial(_out_proj_kernel, final_norm=final_norm),
        grid=(steps,),
        in_specs=in_specs,
        out_specs=out_specs,
        out_shape=out_shape,
        compiler_params=pltpu.CompilerParams(
            dimension_semantics=("parallel",), vmem_limit_bytes=V7X_VMEM_LIMIT),
        name="out_proj",
    )(*args)
    return res if w_in_f32 is not None else res[0]


_Block = collections.namedtuple("_Block", "scores pv finish prior stats depth flush")


def _run_pipeline(blocks):
    issued = []
    pending = None
    for i, blk in enumerate(blocks):
        while len(issued) < min(i + 1 + blk.depth, len(blocks)):
            issued.append(blocks[len(issued)].scores())
        s = issued[i]
        issued[i] = None
        if blk.flush and pending is not None:
            pending[0].finish(*pending[1:])
            pending = None
        m = jnp.max(s, axis=-1, keepdims=True)
        if blk.stats is not None:
            blk.stats(m)
        if blk.prior is None:
            p = jnp.exp2(s - m)
            acc, l = blk.pv(p.astype(jnp.bfloat16))
        else:
            m_b, l_b, acc_b = blk.prior()
            m = jnp.maximum(m, m_b)
            p = jnp.exp2(s - jnp.concatenate([m] * (s.shape[1] // m.shape[1]), axis=1))
            alpha = jnp.exp2(m_b - m)
            acc, l = blk.pv(p.astype(jnp.bfloat16))
            l = alpha * l_b + l
            acc = alpha * acc_b + acc
        if pending is not None:
            pending[0].finish(*pending[1:])
        pending = (blk, l, acc)
    pending[0].finish(*pending[1:])


def _na_kernel(q_ref, k_ref, v_ref, gate_ref, toep_ref, o_ref, bias_ref):
    n_groups = q_ref.shape[0] // NA_Q
    rows = q_ref.shape[0] // GRID_W

    def key_start(g):
        return int(np.clip(g * NA_ROWS_PER_GROUP - NA_WIN_ROWS // 2, 0, rows - NA_KEY_ROWS)) * GRID_W

    @pl.when(pl.program_id(1) == 0)
    def _():
        outside = jnp.full((GRID_W, GRID_W), NEG, jnp.float32)
        for hh in range(q_ref.shape[1] // HEAD_DIM):
            for pat, g in enumerate((0, 1, n_groups - 1)):
                base = key_start(g) // GRID_W
                for qi in range(NA_ROWS_PER_GROUP):
                    r = g * NA_ROWS_PER_GROUP + qi
                    rs = int(np.clip(r - NA_WIN_ROWS // 2, 0, rows - NA_WIN_ROWS))
                    for kj in range(0, NA_KEY_ROWS, 2):
                        pair = [toep_ref[hh, base + kk - r + NA_WIN_ROWS - 1]
                                if rs <= base + kk < rs + NA_WIN_ROWS else outside for kk in (kj, kj + 1)]
                        bias_ref[hh, pat, qi * GRID_W:(qi + 1) * GRID_W, kj * GRID_W:(kj + 2) * GRID_W] = (
                            jnp.concatenate(pair, axis=1))

    def block(hh, g):
        cols = _head_cols(hh)
        qrows = slice(g * NA_Q, (g + 1) * NA_Q)
        krows = slice(key_start(g), key_start(g) + NA_K)
        pat = 0 if g == 0 else (2 if g == n_groups - 1 else 1)

        def scores():
            return _dot_nt(q_ref[qrows, cols], k_ref[krows, cols]) + bias_ref[hh, pat]

        def pv(p):
            return _pv_with_rowsum(p, v_ref[krows, cols])

        def finish(l, acc):
            o = (acc / l) * gate_ref[qrows, cols].astype(jnp.float32)
            o_ref[qrows, cols] = o.astype(o_ref.dtype)

        return _Block(scores, pv, finish, None, None, NA_DEPTH, False)

    _run_pipeline([block(hh, g) for hh in range(q_ref.shape[1] // HEAD_DIM) for g in range(n_groups)])


def _na_toeplitz(rpb):
    c = np.arange(GRID_W)
    cs = np.clip(c - NA_WIN_COLS // 2, 0, GRID_W - NA_WIN_COLS)
    col_ok = (c[None, :] >= cs[:, None]) & (c[None, :] < cs[:, None] + NA_WIN_COLS)
    dc = np.clip(c[None, :] - c[:, None], -(NA_WIN_COLS - 1), NA_WIN_COLS - 1) + NA_WIN_COLS - 1
    onehot = (dc[None] == np.arange(2 * NA_WIN_COLS - 1)[:, None, None]).astype(np.float32)
    toep = jnp.einsum("lhaj,jqk->lhaqk", rpb.astype(jnp.float32), jnp.asarray(onehot),
                      precision=lax.Precision.HIGHEST)
    return jnp.where(col_ok, toep * LOG2E, NEG)


def _na_attention(z, toep, layer, batch, seq):
    n = z.shape[0]
    hc = NA_HPS * HEAD_DIM
    qb, kb, vb, gb = (OFF_NA_Q // hc, OFF_NA_K // hc, OFF_NA_V // hc, OFF_NA_G // hc)
    return pl.pallas_call(
        _na_kernel,
        grid=(N_HEADS_NA // NA_HPS, batch),
        in_specs=[
            pl.BlockSpec((seq, hc), lambda h, b: (b, qb + h)),
            pl.BlockSpec((seq, hc), lambda h, b: (b, kb + h)),
            pl.BlockSpec((seq, hc), lambda h, b: (b, vb + h)),
            pl.BlockSpec((seq, hc), lambda h, b: (b, gb + h)),
            pl.BlockSpec((None, NA_HPS) + toep.shape[2:], lambda h, b: (layer, h, 0, 0, 0)),
        ],
        out_specs=pl.BlockSpec((seq, hc), lambda h, b: (b, h)),
        out_shape=jax.ShapeDtypeStruct((n, W_NA), jnp.bfloat16),
        scratch_shapes=[pltpu.VMEM((NA_HPS, 3, NA_Q, NA_K), jnp.float32)],
        compiler_params=pltpu.CompilerParams(
            dimension_semantics=("parallel", "arbitrary"), vmem_limit_bytes=V7X_VMEM_LIMIT),
        name="na_attention",
    )(z, z, z, z, toep)


def _dl_mem_kernel(qn_ref, kn_ref, vn_ref, gate_ref, qs_ref, ks_ref, vs_ref, bias_a_ref, bias_b_ref,
                   mq_ref, mk_ref, mv_ref, mgate_ref, o_ref, om_ref, m_scr, l_scr, acc_scr):
    seq = qn_ref.shape[0]
    slen = seq // DL_STREAMS

    def block_m(hh, i):
        cols = _head_cols(hh)
        qrows = slice(i * MEM_Q, (i + 1) * MEM_Q)

        def scores():
            return _dot_nt(mq_ref[qrows, cols], mk_ref[:, cols])

        def pv(p):
            return _pv_with_rowsum(p, mv_ref[:, cols])

        def finish(l, acc):
            o = (acc / l) * mgate_ref[qrows, cols].astype(jnp.float32)
            om_ref[qrows, cols] = o.astype(om_ref.dtype)

        return _Block(scores, pv, finish, None, None, MEM_DEPTH, False)

    def block_b(hh, r):
        cols = _head_cols(hh)
        rows = pl.ds(r, slen, stride=DL_STREAMS)

        def scores():
            return _dot_nt(qs_ref[r, :, cols], ks_ref[r, :, cols]) + bias_b_ref[...]

        def pv(p):
            return _pv_with_rowsum(p, vs_ref[r, :, cols])

        def stats(m):
            m_scr[hh, rows, :] = jnp.broadcast_to(m, (slen, HEAD_DIM))

        def finish(l, acc):
            l_scr[hh, rows, :] = l
            acc_scr[hh, rows, :] = acc

        return _Block(scores, pv, finish, None, stats, DL_DEPTH_B, False)

    def block_a(hh, i):
        cols = _head_cols(hh)
        qrows = slice(i * DL_Q, (i + 1) * DL_Q)
        k0 = int(np.clip(i * DL_Q - DL_HALF, 0, seq - DL_KA))
        krows = slice(k0, k0 + DL_KA)
        pat = (i * DL_Q - k0) // DL_HALF

        def scores():
            return _dot_nt(qn_ref[qrows, cols], kn_ref[krows, cols]) + bias_a_ref[pat]

        def pv(p):
            return _pv_with_rowsum(p, vn_ref[krows, cols])

        def prior():
            return m_scr[hh, qrows, :], l_scr[hh, qrows, :], acc_scr[hh, qrows, :]

        def finish(l, acc):
            o = (acc / l) * gate_ref[qrows, cols].astype(jnp.float32)
            o_ref[qrows, cols] = o.astype(o_ref.dtype)

        return _Block(scores, pv, finish, prior, None, DL_DEPTH_A, i == 0)

    heads = range(qn_ref.shape[1] // HEAD_DIM)
    mem_heads = range(mq_ref.shape[1] // HEAD_DIM)
    _run_pipeline([block_b(hh, r) for hh in heads for r in range(DL_STREAMS)]
                  + [block_m(hh, i) for hh in mem_heads for i in range(seq // MEM_Q)]
                  + [block_a(hh, i) for hh in heads for i in range(seq // DL_Q)])


def _dl_bias_tables(seq):
    (w1, d1), (w4, d4), (w16, d16) = DIL_CONFIGS
    assert d1 == 1 and d4 == DL_STREAMS and d16 % d4 == 0
    assert (w1 // 2) // d1 == DL_HALF and (w4 // 2) // d4 == DL_HALF and (w16 // 2) // d16 == DL_HALF
    r = np.arange(DL_Q)[:, None]
    j = np.arange(DL_KA)[None, :]
    bias_a = np.stack([np.where(np.abs(j - off - r) <= DL_HALF, 0.0, NEG) for off in (0, DL_HALF, 2 * DL_HALF)])
    slen = seq // DL_STREAMS
    delta = np.arange(slen)[None, :] - np.arange(slen)[:, None]
    step = d16 // d4
    mult = (np.abs(delta) <= DL_HALF).astype(np.float64) + ((delta % step == 0) & (np.abs(delta) <= DL_HALF * step))
    bias_b = np.where(mult > 0, np.log2(np.maximum(mult, 1.0)), NEG)
    return jnp.asarray(bias_a, jnp.float32), jnp.asarray(bias_b, jnp.float32)


def _dl_mem_attention(z, zs, mkv, bias_a, bias_b, batch, seq):
    n = z.shape[0]
    hc = DL_HPS * HEAD_DIM
    qb, kb, vb, gb = (OFF_DL_Q // hc, OFF_DL_K // hc, OFF_DL_V // hc, OFF_DL_G // hc)
    slen = seq // DL_STREAMS
    steps = N_HEADS_DIL // DL_HPS
    stream_spec = lambda c0: pl.BlockSpec((None, DL_STREAMS, slen, hc), lambda b, h: (b, 0, 0, c0 + h))
    mc = MEM_HPS * HEAD_DIM
    mqb, mgb, mvb = OFF_M_Q // mc, OFF_M_G // mc, W_MEM // mc
    return pl.pallas_call(
        _dl_mem_kernel,
        grid=(batch, steps),
        in_specs=[
            pl.BlockSpec((seq, hc), lambda b, h: (b, qb + h)),
            pl.BlockSpec((seq, hc), lambda b, h: (b, kb + h)),
            pl.BlockSpec((seq, hc), lambda b, h: (b, vb + h)),
            pl.BlockSpec((seq, hc), lambda b, h: (b, gb + h)),
            stream_spec(0), stream_spec(steps), stream_spec(2 * steps),
            pl.BlockSpec(bias_a.shape, lambda b, h: (0, 0, 0)),
            pl.BlockSpec(bias_b.shape, lambda b, h: (0, 0)),
            pl.BlockSpec((seq, mc), lambda b, h: (b, mqb + h)),
            pl.BlockSpec((N_MEM, mc), lambda b, h: (b, h)),
            pl.BlockSpec((N_MEM, mc), lambda b, h: (b, mvb + h)),
            pl.BlockSpec((seq, mc), lambda b, h: (b, mgb + h)),
        ],
        out_specs=[pl.BlockSpec((seq, hc), lambda b, h: (b, h)),
                   pl.BlockSpec((seq, mc), lambda b, h: (b, h))],
        out_shape=[jax.ShapeDtypeStruct((n, W_DIL), jnp.bfloat16),
                   jax.ShapeDtypeStruct((n, W_MEM), jnp.bfloat16)],
        scratch_shapes=[pltpu.VMEM((DL_HPS, seq, HEAD_DIM), jnp.float32)] * 3,
        compiler_params=pltpu.CompilerParams(
            dimension_semantics=("parallel", "parallel"), vmem_limit_bytes=V7X_VMEM_LIMIT),
        name="dl_mem_attention",
    )(z, z, z, z, zs, zs, zs, bias_a, bias_b, z, mkv, mkv, z)


def _rope_tables(seq):
    half = HEAD_DIM // 2
    inv = ROPE_THETA ** (-jnp.arange(half, dtype=jnp.float32) / half)
    ang = jnp.arange(seq).astype(jnp.float32)[:, None] * inv[None, :]
    cos, sin = jnp.cos(ang), jnp.sin(ang)
    return jnp.concatenate([cos, cos], axis=-1), jnp.concatenate([-sin, sin], axis=-1)


def kernel(x, mem, norm_g, w_in, na_rpb, mem_norm_g, w_mem_kv, w_out, final_g):
    batch, seq, d = x.shape
    depth = w_in.shape[0]
    assert d == D_MODEL and w_in.shape[2] == IN_COLS and seq % GRID_W == 0
    assert seq % PROJ_TM == 0 and seq % OUT_TM == 0 and seq % NA_Q == 0 and seq % MEM_Q == 0
    assert seq % (DL_STREAMS * DL_Q) == 0 and PROJ_TM % DL_STREAMS == 0
    assert all(off % (NA_HPS * HEAD_DIM) == 0 for off in (OFF_NA_Q, OFF_NA_K, OFF_NA_V, OFF_NA_G))
    assert all(off % (DL_HPS * HEAD_DIM) == 0 for off in (OFF_DL_Q, OFF_DL_K, OFF_DL_V, OFF_DL_G))
    assert all(off % (MEM_HPS * HEAD_DIM) == 0 for off in (OFF_M_Q, OFF_M_G, W_MEM))
    assert N_HEADS_NA % NA_HPS == 0 and N_HEADS_DIL % DL_HPS == 0
    assert N_HEADS_MEM * DL_HPS % N_HEADS_DIL == 0

    cos2, sin2 = _rope_tables(seq)
    dl_bias_a, dl_bias_b = _dl_bias_tables(seq)
    na_toep = _na_toeplitz(na_rpb)
    w_mem_b = w_mem_kv.astype(jnp.bfloat16)
    mem_f = mem.reshape(batch * N_MEM, d)

    xf = x.reshape(batch * seq, d)
    w_in_b = None
    for l in range(depth):
        casts = [(w_out, l)] + ([(w_in, 0)] if l == 0 else [])
        mkv, w_out_b, *first = _mem_proj(mem_f, mem_norm_g, w_mem_b, l, casts)
        if first:
            w_in_b = first[0][None]
        z, zs = _in_proj(xf, norm_g[l], w_in_b, 0, cos2, sin2, IN_SEGMENTS, batch, seq)
        y_na = _na_attention(z, na_toep, l, batch, seq)
        y_dl, y_m = _dl_mem_attention(z, zs, mkv, dl_bias_a, dl_bias_b, batch, seq)
        if l + 1 < depth:
            xf, w_in_b = _out_proj(xf, y_na, y_dl, y_m, w_out_b, final_g, False, w_in_f32=w_in, next_layer=l + 1)
        else:
            xf = _out_proj(xf, y_na, y_dl, y_m, w_out_b, final_g, True)
    return xf.reshape(batch, seq, d)
```

```python
import collections
import functools

import numpy as np
import jax
import jax.numpy as jnp
from jax import lax
from jax.experimental import pallas as pl
from jax.experimental.pallas import tpu as pltpu

D_MODEL = 2048
HEAD_DIM = 128
N_HEADS_NA = 6
N_HEADS_DIL = 6
N_HEADS_MEM = 4
W_NA = N_HEADS_NA * HEAD_DIM
W_DIL = N_HEADS_DIL * HEAD_DIM
W_MEM = N_HEADS_MEM * HEAD_DIM
IN_COLS = 4 * W_NA + 4 * W_DIL + 2 * W_MEM
N_MEM = 256
GRID_W = 64
NA_WIN_ROWS = 8
NA_WIN_COLS = 16
DIL_CONFIGS = ((128, 1), (512, 4), (2048, 16))
ROPE_THETA = 10000.0
EPS = 1e-6
NEG = -1e30
SCALE = HEAD_DIM ** -0.5
LOG2E = float(np.log2(np.e))
Q_SCALE = SCALE * LOG2E

OFF_NA_Q, OFF_NA_K, OFF_NA_V, OFF_NA_G = 0, W_NA, 2 * W_NA, 3 * W_NA
OFF_DL_Q = 4 * W_NA
OFF_DL_K, OFF_DL_V, OFF_DL_G = OFF_DL_Q + W_DIL, OFF_DL_Q + 2 * W_DIL, OFF_DL_Q + 3 * W_DIL
OFF_M_Q = OFF_DL_Q + 4 * W_DIL
OFF_M_G = OFF_M_Q + W_MEM

IN_SEGMENTS = (
    (OFF_NA_Q, W_NA, "scale", None), (OFF_NA_K, W_NA, "plain", None),
    (OFF_NA_V, W_NA, "plain", None), (OFF_NA_G, W_NA, "silu", None),
    (OFF_DL_Q, W_DIL, "rope_scale", 0), (OFF_DL_K, W_DIL, "rope", W_DIL),
    (OFF_DL_V, W_DIL, "plain", 2 * W_DIL), (OFF_DL_G, W_DIL, "silu", None),
    (OFF_M_Q, W_MEM, "scale", None), (OFF_M_G, W_MEM, "silu", None),
)

V7X_VMEM_LIMIT = 62 * 1024 * 1024

PROJ_TM = 512
OUT_TM = 512
MEM_PROJ_TM = 512
NA_HPS, DL_HPS = 6, 3
MEM_HPS = N_HEADS_MEM * DL_HPS // N_HEADS_DIL
NA_ROWS_PER_GROUP = 4
NA_Q = NA_ROWS_PER_GROUP * GRID_W
NA_KEY_ROWS = NA_ROWS_PER_GROUP + NA_WIN_ROWS
NA_K = NA_KEY_ROWS * GRID_W
NA_KEY_STEP = 4
NA_DEPTH = 2
MEM_Q = 512
MEM_DEPTH = 1

DL_STREAMS = 4
DL_HALF = 64
DL_Q = 128
DL_KA = DL_Q + 2 * DL_HALF
DL_DEPTH_A = 4
DL_DEPTH_B = 2


def _rms_norm_f32(x, g):
    return x * lax.rsqrt(jnp.mean(x * x, axis=-1, keepdims=True) + EPS) * g


def _dot_nt(a, b):
    return lax.dot_general(a, b, (((1,), (1,)), ((), ())), preferred_element_type=jnp.float32)


def _pv_with_rowsum(p, v):
    ones = jnp.ones((v.shape[0], HEAD_DIM), v.dtype)
    r = jnp.dot(p, jnp.concatenate([v, ones], axis=1), preferred_element_type=jnp.float32)
    return r[:, :HEAD_DIM], r[:, HEAD_DIM:]


def _head_cols(hh):
    return slice(hh * HEAD_DIM, (hh + 1) * HEAD_DIM)


def _in_proj_kernel(x_ref, g_ref, w_ref, cos_ref, sin_ref, o_ref, *rest, segments):
    if rest:
        s_ref, scr = rest
    tm = x_ref.shape[0]
    h = _rms_norm_f32(x_ref[...], g_ref[...]).astype(jnp.bfloat16)

    def emit(a, c0, width, s0):
        o_ref[:, c0:c0 + width] = a.astype(o_ref.dtype)
        if s0 is not None:
            for j in range(0, width, HEAD_DIM):
                scr[j // HEAD_DIM] = a[:, j:j + HEAD_DIM]
                for r in range(DL_STREAMS):
                    rows = scr[j // HEAD_DIM, pl.ds(r, tm // DL_STREAMS, stride=DL_STREAMS), :]
                    s_ref[r, :, s0 + j:s0 + j + HEAD_DIM] = rows.astype(s_ref.dtype)

    for off, width, kind, s_off in segments:
        a = jnp.dot(h, w_ref[:, off:off + width], preferred_element_type=jnp.float32)
        if kind in ("rope", "rope_scale"):
            for j in range(0, width, HEAD_DIM):
                aj = a[:, j:j + HEAD_DIM]
                aj = aj * cos_ref[...] + pltpu.roll(aj, HEAD_DIM // 2, axis=1) * sin_ref[...]
                if kind == "rope_scale":
                    aj = aj * Q_SCALE
                emit(aj, off + j, HEAD_DIM, None if s_off is None else s_off + j)
        else:
            if kind == "scale":
                a = a * Q_SCALE
            elif kind == "silu":
                a = a * (1.0 / (1.0 + jnp.exp(-a)))
            emit(a, off, width, s_off)


def _in_proj(x, g, w_bf16, layer, cos2, sin2, segments, batch, seq):
    n, d = x.shape
    cols = w_bf16.shape[2]
    tm = PROJ_TM
    bps = seq // tm
    streamed = any(s[3] is not None for s in segments)
    out_specs = [pl.BlockSpec((tm, cols), lambda i: (i, 0))]
    out_shape = [jax.ShapeDtypeStruct((n, cols), jnp.bfloat16)]
    scratch = []
    if streamed:
        out_specs.append(pl.BlockSpec((None, DL_STREAMS, tm // DL_STREAMS, 3 * W_DIL),
                                      lambda i: (i // bps, 0, i % bps, 0)))
        out_shape.append(jax.ShapeDtypeStruct((batch, DL_STREAMS, seq // DL_STREAMS, 3 * W_DIL), jnp.bfloat16))
        scratch.append(pltpu.VMEM((W_DIL // HEAD_DIM, tm, HEAD_DIM), jnp.float32))
    res = pl.pallas_call(
        functools.partial(_in_proj_kernel, segments=segments),
        grid=(n // tm,),
        in_specs=[
            pl.BlockSpec((tm, d), lambda i: (i, 0)),
            pl.BlockSpec((1, d), lambda i: (0, 0)),
            pl.BlockSpec((None, d, cols), lambda i: (layer, 0, 0), pipeline_mode=pl.Buffered(1)),
            pl.BlockSpec((tm, HEAD_DIM), lambda i: (i % bps, 0)),
            pl.BlockSpec((tm, HEAD_DIM), lambda i: (i % bps, 0)),
        ],
        out_specs=out_specs,
        out_shape=out_shape,
        scratch_shapes=scratch,
        compiler_params=pltpu.CompilerParams(
            dimension_semantics=("parallel",), vmem_limit_bytes=V7X_VMEM_LIMIT),
        name="in_proj",
    )(x, g.reshape(1, d), w_bf16, cos2, sin2)
    return res if streamed else res[0]


def _mem_proj_kernel(x_ref, g_ref, w_ref, *rest):
    n_casts = (len(rest) - 1) // 2
    o_ref = rest[n_casts]
    for src_ref, dst_ref in zip(rest[:n_casts], rest[n_casts + 1:]):
        dst_ref[...] = src_ref[...].astype(dst_ref.dtype)
    h = _rms_norm_f32(x_ref[...], g_ref[...]).astype(jnp.bfloat16)
    o_ref[...] = jnp.dot(h, w_ref[...], preferred_element_type=jnp.float32).astype(o_ref.dtype)


def _mem_proj(mem, g, w_bf16, layer, casts):
    n, d = mem.shape
    cols = w_bf16.shape[2]
    tm = MEM_PROJ_TM
    steps = n // tm
    in_specs = [
        pl.BlockSpec((tm, d), lambda i: (i, 0)),
        pl.BlockSpec((1, d), lambda i: (0, 0)),
        pl.BlockSpec((None, d, cols), lambda i: (layer, 0, 0), pipeline_mode=pl.Buffered(1)),
    ]
    out_specs = [pl.BlockSpec((tm, cols), lambda i: (i, 0))]
    out_shape = [jax.ShapeDtypeStruct((n, cols), jnp.bfloat16)]
    for w, idx in casts:
        _, r, c = w.shape
        assert r % steps == 0
        in_specs.append(pl.BlockSpec((None, r // steps, c), lambda i, idx=idx: (idx, i, 0)))
        out_specs.append(pl.BlockSpec((r // steps, c), lambda i: (i, 0)))
        out_shape.append(jax.ShapeDtypeStruct((r, c), jnp.bfloat16))
    return pl.pallas_call(
        _mem_proj_kernel,
        grid=(steps,),
        in_specs=in_specs,
        out_specs=out_specs,
        out_shape=out_shape,
        compiler_params=pltpu.CompilerParams(
            dimension_semantics=("parallel",), vmem_limit_bytes=V7X_VMEM_LIMIT),
        name="mem_proj",
    )(mem, g.reshape(1, d), w_bf16, *[w for w, _ in casts])


def _out_proj_kernel(x_ref, yna_ref, ydl_ref, ym_ref, w_ref, g_ref, *rest, final_norm):
    if len(rest) == 3:
        wnext_ref, o_ref, wnext_bf16_ref = rest
        wnext_bf16_ref[...] = wnext_ref[...].astype(wnext_bf16_ref.dtype)
    else:
        (o_ref,) = rest
    acc = jnp.dot(yna_ref[...], w_ref[0:W_NA, :], preferred_element_type=jnp.float32)
    acc += jnp.dot(ydl_ref[...], w_ref[W_NA:W_NA + W_DIL, :], preferred_element_type=jnp.float32)
    acc += jnp.dot(ym_ref[...], w_ref[W_NA + W_DIL:, :], preferred_element_type=jnp.float32)
    x = x_ref[...] + acc
    if final_norm:
        x = _rms_norm_f32(x, g_ref[...])
    o_ref[...] = x


def _out_proj(x, y_na, y_dl, y_m, w_bf16, final_g, final_norm, w_in_f32=None, next_layer=None):
    n, d = x.shape
    tm = OUT_TM
    steps = n // tm
    in_specs = [
        pl.BlockSpec((tm, d), lambda i: (i, 0)),
        pl.BlockSpec((tm, W_NA), lambda i: (i, 0)),
        pl.BlockSpec((tm, W_DIL), lambda i: (i, 0)),
        pl.BlockSpec((tm, W_MEM), lambda i: (i, 0)),
        pl.BlockSpec((d, d), lambda i: (0, 0), pipeline_mode=pl.Buffered(1)),
        pl.BlockSpec((1, d), lambda i: (0, 0)),
    ]
    out_specs = [pl.BlockSpec((tm, d), lambda i: (i, 0))]
    out_shape = [jax.ShapeDtypeStruct((n, d), jnp.float32)]
    args = [x, y_na, y_dl, y_m, w_bf16, final_g.reshape(1, d)]
    if w_in_f32 is not None:
        cols = w_in_f32.shape[2]
        assert d % steps == 0
        slab = d // steps
        in_specs.append(pl.BlockSpec((None, slab, cols), lambda i: (next_layer, i, 0)))
        out_specs.append(pl.BlockSpec((None, slab, cols), lambda i: (0, i, 0)))
        out_shape.append(jax.ShapeDtypeStruct((1, d, cols), jnp.bfloat16))
        args.append(w_in_f32)
    res = pl.pallas_call(
        functools.partial(_out_proj_kernel, final_norm=final_norm),
        grid=(steps,),
        in_specs=in_specs,
        out_specs=out_specs,
        out_shape=out_shape,
        compiler_params=pltpu.CompilerParams(
            dimension_semantics=("parallel",), vmem_limit_bytes=V7X_VMEM_LIMIT),
        name="out_proj",
    )(*args)
    return res if w_in_f32 is not None else res[0]


_Block = collections.namedtuple("_Block", "scores pv finish prior stats depth flush")


def _run_pipeline(blocks):
    issued = []
    pending = None
    for i, blk in enumerate(blocks):
        while len(issued) < min(i + 1 + blk.depth, len(blocks)):
            issued.append(blocks[len(issued)].scores())
        s = issued[i]
        issued[i] = None
        if blk.flush and pending is not None:
            pending[0].finish(*pending[1:])
            pending = None
        m = jnp.max(s, axis=-1, keepdims=True)
        if blk.stats is not None:
            blk.stats(m)
        if blk.prior is None:
            p = jnp.exp2(s - m)
            acc, l = blk.pv(p.astype(jnp.bfloat16))
        else:
            m_b, l_b, acc_b = blk.prior()
            m = jnp.maximum(m, m_b)
            p = jnp.exp2(s - jnp.concatenate([m] * (s.shape[1] // m.shape[1]), axis=1))
            alpha = jnp.exp2(m_b - m)
            acc, l = blk.pv(p.astype(jnp.bfloat16))
            l = alpha * l_b + l
            acc = alpha * acc_b + acc
        if pending is not None:
            pending[0].finish(*pending[1:])
        pending = (blk, l, acc)
    pending[0].finish(*pending[1:])


def _na_kernel(q_ref, k_ref, v_ref, gate_ref, toep_ref, o_ref, bias_ref):
    n_groups = q_ref.shape[0] // NA_Q
    rows = q_ref.shape[0] // GRID_W

    def window_start(r):
        return int(np.clip(r - NA_WIN_ROWS // 2, 0, rows - NA_WIN_ROWS))

    def key_rows(g):
        lo = window_start(g * NA_ROWS_PER_GROUP)
        hi = window_start((g + 1) * NA_ROWS_PER_GROUP - 1) + NA_WIN_ROWS
        n = -(-(hi - lo) // NA_KEY_STEP) * NA_KEY_STEP
        return min(lo, rows - n), n

    @pl.when(pl.program_id(1) == 0)
    def _():
        outside = jnp.full((GRID_W, GRID_W), NEG, jnp.float32)
        for hh in range(q_ref.shape[1] // HEAD_DIM):
            for pat, g in enumerate((0, 1, n_groups - 1)):
                base, n_rows = key_rows(g)
                for qi in range(NA_ROWS_PER_GROUP):
                    r = g * NA_ROWS_PER_GROUP + qi
                    rs = window_start(r)
                    for kj in range(0, n_rows, 2):
                        pair = [toep_ref[hh, base + kk - r + NA_WIN_ROWS - 1]
                                if rs <= base + kk < rs + NA_WIN_ROWS else outside for kk in (kj, kj + 1)]
                        bias_ref[hh, pat, qi * GRID_W:(qi + 1) * GRID_W, kj * GRID_W:(kj + 2) * GRID_W] = (
                            jnp.concatenate(pair, axis=1))

    def block(hh, g):
        cols = _head_cols(hh)
        qrows = slice(g * NA_Q, (g + 1) * NA_Q)
        base, n_rows = key_rows(g)
        krows = slice(base * GRID_W, (base + n_rows) * GRID_W)
        pat = 0 if g == 0 else (2 if g == n_groups - 1 else 1)

        def scores():
            return _dot_nt(q_ref[qrows, cols], k_ref[krows, cols]) + bias_ref[hh, pat, :, 0:n_rows * GRID_W]

        def pv(p):
            return _pv_with_rowsum(p, v_ref[krows, cols])

        def finish(l, acc):
            o = (acc / l) * gate_ref[qrows, cols].astype(jnp.float32)
            o_ref[qrows, cols] = o.astype(o_ref.dtype)

        return _Block(scores, pv, finish, None, None, NA_DEPTH, False)

    _run_pipeline([block(hh, g) for hh in range(q_ref.shape[1] // HEAD_DIM) for g in range(n_groups)])


def _na_toeplitz(rpb):
    c = np.arange(GRID_W)
    cs = np.clip(c - NA_WIN_COLS // 2, 0, GRID_W - NA_WIN_COLS)
    col_ok = (c[None, :] >= cs[:, None]) & (c[None, :] < cs[:, None] + NA_WIN_COLS)
    dc = np.clip(c[None, :] - c[:, None], -(NA_WIN_COLS - 1), NA_WIN_COLS - 1) + NA_WIN_COLS - 1
    onehot = (dc[None] == np.arange(2 * NA_WIN_COLS - 1)[:, None, None]).astype(np.float32)
    toep = jnp.einsum("lhaj,jqk->lhaqk", rpb.astype(jnp.float32), jnp.asarray(onehot),
                      precision=lax.Precision.HIGHEST)
    return jnp.where(col_ok, toep * LOG2E, NEG)


def _na_attention(z, toep, layer, batch, seq):
    n = z.shape[0]
    hc = NA_HPS * HEAD_DIM
    qb, kb, vb, gb = (OFF_NA_Q // hc, OFF_NA_K // hc, OFF_NA_V // hc, OFF_NA_G // hc)
    return pl.pallas_call(
        _na_kernel,
        grid=(N_HEADS_NA // NA_HPS, batch),
        in_specs=[
            pl.BlockSpec((seq, hc), lambda h, b: (b, qb + h)),
            pl.BlockSpec((seq, hc), lambda h, b: (b, kb + h)),
            pl.BlockSpec((seq, hc), lambda h, b: (b, vb + h)),
            pl.BlockSpec((seq, hc), lambda h, b: (b, gb + h)),
            pl.BlockSpec((None, NA_HPS) + toep.shape[2:], lambda h, b: (layer, h, 0, 0, 0)),
        ],
        out_specs=pl.BlockSpec((seq, hc), lambda h, b: (b, h)),
        out_shape=jax.ShapeDtypeStruct((n, W_NA), jnp.bfloat16),
        scratch_shapes=[pltpu.VMEM((NA_HPS, 3, NA_Q, NA_K), jnp.float32)],
        compiler_params=pltpu.CompilerParams(
            dimension_semantics=("parallel", "arbitrary"), vmem_limit_bytes=V7X_VMEM_LIMIT),
        name="na_attention",
    )(z, z, z, z, toep)


def _dl_mem_kernel(qn_ref, kn_ref, vn_ref, gate_ref, qs_ref, ks_ref, vs_ref, bias_a_ref, bias_b_ref,
                   mq_ref, mk_ref, mv_ref, mgate_ref, o_ref, om_ref, m_scr, l_scr, acc_scr):
    seq = qn_ref.shape[0]
    slen = seq // DL_STREAMS

    def block_m(hh, i):
        cols = _head_cols(hh)
        qrows = slice(i * MEM_Q, (i + 1) * MEM_Q)

        def scores():
            return _dot_nt(mq_ref[qrows, cols], mk_ref[:, cols])

        def pv(p):
            return _pv_with_rowsum(p, mv_ref[:, cols])

        def finish(l, acc):
            o = (acc / l) * mgate_ref[qrows, cols].astype(jnp.float32)
            om_ref[qrows, cols] = o.astype(om_ref.dtype)

        return _Block(scores, pv, finish, None, None, MEM_DEPTH, False)

    def block_b(hh, r):
        cols = _head_cols(hh)
        rows = pl.ds(r, slen, stride=DL_STREAMS)

        def scores():
            return _dot_nt(qs_ref[r, :, cols], ks_ref[r, :, cols]) + bias_b_ref[...]

        def pv(p):
            return _pv_with_rowsum(p, vs_ref[r, :, cols])

        def stats(m):
            m_scr[hh, rows, :] = jnp.broadcast_to(m, (slen, HEAD_DIM))

        def finish(l, acc):
            l_scr[hh, rows, :] = l
            acc_scr[hh, rows, :] = acc

        return _Block(scores, pv, finish, None, stats, DL_DEPTH_B, False)

    def block_a(hh, i):
        cols = _head_cols(hh)
        qrows = slice(i * DL_Q, (i + 1) * DL_Q)
        k0 = int(np.clip(i * DL_Q - DL_HALF, 0, seq - DL_KA))
        krows = slice(k0, k0 + DL_KA)
        pat = (i * DL_Q - k0) // DL_HALF

        def scores():
            return _dot_nt(qn_ref[qrows, cols], kn_ref[krows, cols]) + bias_a_ref[pat]

        def pv(p):
            return _pv_with_rowsum(p, vn_ref[krows, cols])

        def prior():
            return m_scr[hh, qrows, :], l_scr[hh, qrows, :], acc_scr[hh, qrows, :]

        def finish(l, acc):
            o = (acc / l) * gate_ref[qrows, cols].astype(jnp.float32)
            o_ref[qrows, cols] = o.astype(o_ref.dtype)

        return _Block(scores, pv, finish, prior, None, DL_DEPTH_A, i == 0)

    heads = range(qn_ref.shape[1] // HEAD_DIM)
    mem_heads = range(mq_ref.shape[1] // HEAD_DIM)
    _run_pipeline([block_b(hh, r) for hh in heads for r in range(DL_STREAMS)]
                  + [block_m(hh, i) for hh in mem_heads for i in range(seq // MEM_Q)]
                  + [block_a(hh, i) for hh in heads for i in range(seq // DL_Q)])


def _dl_bias_tables(seq):
    (w1, d1), (w4, d4), (w16, d16) = DIL_CONFIGS
    assert d1 == 1 and d4 == DL_STREAMS and d16 % d4 == 0
    assert (w1 // 2) // d1 == DL_HALF and (w4 // 2) // d4 == DL_HALF and (w16 // 2) // d16 == DL_HALF
    r = np.arange(DL_Q)[:, None]
    j = np.arange(DL_KA)[None, :]
    bias_a = np.stack([np.where(np.abs(j - off - r) <= DL_HALF, 0.0, NEG) for off in (0, DL_HALF, 2 * DL_HALF)])
    slen = seq // DL_STREAMS
    delta = np.arange(slen)[None, :] - np.arange(slen)[:, None]
    step = d16 // d4
    mult = (np.abs(delta) <= DL_HALF).astype(np.float64) + ((delta % step == 0) & (np.abs(delta) <= DL_HALF * step))
    bias_b = np.where(mult > 0, np.log2(np.maximum(mult, 1.0)), NEG)
    return jnp.asarray(bias_a, jnp.float32), jnp.asarray(bias_b, jnp.float32)


def _dl_mem_attention(z, zs, mkv, bias_a, bias_b, batch, seq):
    n = z.shape[0]
    hc = DL_HPS * HEAD_DIM
    qb, kb, vb, gb = (OFF_DL_Q // hc, OFF_DL_K // hc, OFF_DL_V // hc, OFF_DL_G // hc)
    slen = seq // DL_STREAMS
    steps = N_HEADS_DIL // DL_HPS
    stream_spec = lambda c0: pl.BlockSpec((None, DL_STREAMS, slen, hc), lambda b, h: (b, 0, 0, c0 + h))
    mc = MEM_HPS * HEAD_DIM
    mqb, mgb, mvb = OFF_M_Q // mc, OFF_M_G // mc, W_MEM // mc
    return pl.pallas_call(
        _dl_mem_kernel,
        grid=(batch, steps),
        in_specs=[
            pl.BlockSpec((seq, hc), lambda b, h: (b, qb + h)),
            pl.BlockSpec((seq, hc), lambda b, h: (b, kb + h)),
            pl.BlockSpec((seq, hc), lambda b, h: (b, vb + h)),
            pl.BlockSpec((seq, hc), lambda b, h: (b, gb + h)),
            stream_spec(0), stream_spec(steps), stream_spec(2 * steps),
            pl.BlockSpec(bias_a.shape, lambda b, h: (0, 0, 0)),
            pl.BlockSpec(bias_b.shape, lambda b, h: (0, 0)),
            pl.BlockSpec((seq, mc), lambda b, h: (b, mqb + h)),
            pl.BlockSpec((N_MEM, mc), lambda b, h: (b, h)),
            pl.BlockSpec((N_MEM, mc), lambda b, h: (b, mvb + h)),
            pl.BlockSpec((seq, mc), lambda b, h: (b, mgb + h)),
        ],
        out_specs=[pl.BlockSpec((seq, hc), lambda b, h: (b, h)),
                   pl.BlockSpec((seq, mc), lambda b, h: (b, h))],
        out_shape=[jax.ShapeDtypeStruct((n, W_DIL), jnp.bfloat16),
                   jax.ShapeDtypeStruct((n, W_MEM), jnp.bfloat16)],
        scratch_shapes=[pltpu.VMEM((DL_HPS, seq, HEAD_DIM), jnp.float32)] * 3,
        compiler_params=pltpu.CompilerParams(
            dimension_semantics=("parallel", "parallel"), vmem_limit_bytes=V7X_VMEM_LIMIT),
        name="dl_mem_attention",
    )(z, z, z, z, zs, zs, zs, bias_a, bias_b, z, mkv, mkv, z)


def _rope_tables(seq):
    half = HEAD_DIM // 2
    inv = ROPE_THETA ** (-jnp.arange(half, dtype=jnp.float32) / half)
    ang = jnp.arange(seq).astype(jnp.float32)[:, None] * inv[None, :]
    cos, sin = jnp.cos(ang), jnp.sin(ang)
    return jnp.concatenate([cos, cos], axis=-1), jnp.concatenate([-sin, sin], axis=-1)


def kernel(x, mem, norm_g, w_in, na_rpb, mem_norm_g, w_mem_kv, w_out, final_g):
    batch, seq, d = x.shape
    depth = w_in.shape[0]
    assert d == D_MODEL and w_in.shape[2] == IN_COLS and seq % GRID_W == 0
    assert seq % PROJ_TM == 0 and seq % OUT_TM == 0 and seq % NA_Q == 0 and seq % MEM_Q == 0
    assert seq % (DL_STREAMS * DL_Q) == 0 and PROJ_TM % DL_STREAMS == 0
    assert all(off % (NA_HPS * HEAD_DIM) == 0 for off in (OFF_NA_Q, OFF_NA_K, OFF_NA_V, OFF_NA_G))
    assert all(off % (DL_HPS * HEAD_DIM) == 0 for off in (OFF_DL_Q, OFF_DL_K, OFF_DL_V, OFF_DL_G))
    assert all(off % (MEM_HPS * HEAD_DIM) == 0 for off in (OFF_M_Q, OFF_M_G, W_MEM))
    assert N_HEADS_NA % NA_HPS == 0 and N_HEADS_DIL % DL_HPS == 0
    assert N_HEADS_MEM * DL_HPS % N_HEADS_DIL == 0

    cos2, sin2 = _rope_tables(seq)
    dl_bias_a, dl_bias_b = _dl_bias_tables(seq)
    na_toep = _na_toeplitz(na_rpb)
    w_mem_b = w_mem_kv.astype(jnp.bfloat16)
    mem_f = mem.reshape(batch * N_MEM, d)

    xf = x.reshape(batch * seq, d)
    w_in_b = None
    for l in range(depth):
        casts = [(w_out, l)] + ([(w_in, 0)] if l == 0 else [])
        mkv, w_out_b, *first = _mem_proj(mem_f, mem_norm_g, w_mem_b, l, casts)
        if first:
            w_in_b = first[0][None]
        z, zs = _in_proj(xf, norm_g[l], w_in_b, 0, cos2, sin2, IN_SEGMENTS, batch, seq)
        y_na = _na_attention(z, na_toep, l, batch, seq)
        y_dl, y_m = _dl_mem_attention(z, zs, mkv, dl_bias_a, dl_bias_b, batch, seq)
        if l + 1 < depth:
            xf, w_in_b = _out_proj(xf, y_na, y_dl, y_m, w_out_b, final_g, False, w_in_f32=w_in, next_layer=l + 1)
        else:
            xf = _out_proj(xf, y_na, y_dl, y_m, w_out_b, final_g, True)
    return xf.reshape(batch, seq, d)
```

```python
import collections
import functools

import numpy as np
import jax
import jax.numpy as jnp
from jax import lax
from jax.experimental import pallas as pl
from jax.experimental.pallas import tpu as pltpu

D_MODEL = 2048
HEAD_DIM = 128
N_HEADS_NA = 6
N_HEADS_DIL = 6
N_HEADS_MEM = 4
W_NA = N_HEADS_NA * HEAD_DIM
W_DIL = N_HEADS_DIL * HEAD_DIM
W_MEM = N_HEADS_MEM * HEAD_DIM
IN_COLS = 4 * W_NA + 4 * W_DIL + 2 * W_MEM
N_MEM = 256
GRID_W = 64
NA_WIN_ROWS = 8
NA_WIN_COLS = 16
DIL_CONFIGS = ((128, 1), (512, 4), (2048, 16))
ROPE_THETA = 10000.0
EPS = 1e-6
NEG = -1e30
SCALE = HEAD_DIM ** -0.5
LOG2E = float(np.log2(np.e))
Q_SCALE = SCALE * LOG2E

OFF_NA_Q, OFF_NA_K, OFF_NA_V, OFF_NA_G = 0, W_NA, 2 * W_NA, 3 * W_NA
OFF_DL_Q = 4 * W_NA
OFF_DL_K, OFF_DL_V, OFF_DL_G = OFF_DL_Q + W_DIL, OFF_DL_Q + 2 * W_DIL, OFF_DL_Q + 3 * W_DIL
OFF_M_Q = OFF_DL_Q + 4 * W_DIL
OFF_M_G = OFF_M_Q + W_MEM

IN_SEGMENTS = (
    (OFF_NA_Q, W_NA, "scale", None), (OFF_NA_K, W_NA, "plain", None),
    (OFF_NA_V, W_NA, "plain", None), (OFF_NA_G, W_NA, "silu", None),
    (OFF_DL_Q, W_DIL, "rope_scale", 0), (OFF_DL_K, W_DIL, "rope", W_DIL),
    (OFF_DL_V, W_DIL, "plain", 2 * W_DIL), (OFF_DL_G, W_DIL, "silu", None),
    (OFF_M_Q, W_MEM, "scale", None), (OFF_M_G, W_MEM, "silu", None),
)

V7X_VMEM_LIMIT = 62 * 1024 * 1024

PROJ_TM = 512
OUT_TM = 1024
MEM_PROJ_TM = 512
NA_HPS, DL_HPS = 6, 3
MEM_HPS = N_HEADS_MEM * DL_HPS // N_HEADS_DIL
NA_ROWS_PER_GROUP = 4
NA_Q = NA_ROWS_PER_GROUP * GRID_W
NA_KEY_ROWS = NA_ROWS_PER_GROUP + NA_WIN_ROWS
NA_K = NA_KEY_ROWS * GRID_W
NA_KEY_STEP = 4
NA_DEPTH = 2
MEM_Q = 512
MEM_DEPTH = 1

DL_STREAMS = 4
DL_HALF = 64
DL_Q = 128
DL_KA = DL_Q + 2 * DL_HALF
DL_DEPTH_A = 4
DL_DEPTH_B = 2


def _rms_norm_f32(x, g):
    return x * lax.rsqrt(jnp.mean(x * x, axis=-1, keepdims=True) + EPS) * g


def _dot_nt(a, b):
    return lax.dot_general(a, b, (((1,), (1,)), ((), ())), preferred_element_type=jnp.float32)


def _pv_with_rowsum(p, v):
    ones = jnp.ones((v.shape[0], HEAD_DIM), v.dtype)
    r = jnp.dot(p, jnp.concatenate([v, ones], axis=1), preferred_element_type=jnp.float32)
    return r[:, :HEAD_DIM], r[:, HEAD_DIM:]


def _head_cols(hh):
    return slice(hh * HEAD_DIM, (hh + 1) * HEAD_DIM)


def _in_proj_kernel(x_ref, g_ref, w_ref, cos_ref, sin_ref, o_ref, *rest, segments):
    if rest:
        s_ref, scr = rest
    tm = x_ref.shape[0]
    h = _rms_norm_f32(x_ref[...], g_ref[...]).astype(jnp.bfloat16)

    def emit(a, c0, width, s0):
        o_ref[:, c0:c0 + width] = a.astype(o_ref.dtype)
        if s0 is not None:
            for j in range(0, width, HEAD_DIM):
                scr[j // HEAD_DIM] = a[:, j:j + HEAD_DIM]
                for r in range(DL_STREAMS):
                    rows = scr[j // HEAD_DIM, pl.ds(r, tm // DL_STREAMS, stride=DL_STREAMS), :]
                    s_ref[r, :, s0 + j:s0 + j + HEAD_DIM] = rows.astype(s_ref.dtype)

    for off, width, kind, s_off in segments:
        a = jnp.dot(h, w_ref[:, off:off + width], preferred_element_type=jnp.float32)
        if kind in ("rope", "rope_scale"):
            for j in range(0, width, HEAD_DIM):
                aj = a[:, j:j + HEAD_DIM]
                aj = aj * cos_ref[...] + pltpu.roll(aj, HEAD_DIM // 2, axis=1) * sin_ref[...]
                if kind == "rope_scale":
                    aj = aj * Q_SCALE
                emit(aj, off + j, HEAD_DIM, None if s_off is None else s_off + j)
        else:
            if kind == "scale":
                a = a * Q_SCALE
            elif kind == "silu":
                a = a * (1.0 / (1.0 + jnp.exp(-a)))
            emit(a, off, width, s_off)


def _in_proj(x, g, w_bf16, layer, cos2, sin2, segments, batch, seq):
    n, d = x.shape
    cols = w_bf16.shape[2]
    tm = PROJ_TM
    bps = seq // tm
    streamed = any(s[3] is not None for s in segments)
    out_specs = [pl.BlockSpec((tm, cols), lambda i: (i, 0))]
    out_shape = [jax.ShapeDtypeStruct((n, cols), jnp.bfloat16)]
    scratch = []
    if streamed:
        out_specs.append(pl.BlockSpec((None, DL_STREAMS, tm // DL_STREAMS, 3 * W_DIL),
                                      lambda i: (i // bps, 0, i % bps, 0)))
        out_shape.append(jax.ShapeDtypeStruct((batch, DL_STREAMS, seq // DL_STREAMS, 3 * W_DIL), jnp.bfloat16))
        scratch.append(pltpu.VMEM((W_DIL // HEAD_DIM, tm, HEAD_DIM), jnp.float32))
    res = pl.pallas_call(
        functools.partial(_in_proj_kernel, segments=segments),
        grid=(n // tm,),
        in_specs=[
            pl.BlockSpec((tm, d), lambda i: (i, 0)),
            pl.BlockSpec((1, d), lambda i: (0, 0)),
            pl.BlockSpec((None, d, cols), lambda i: (layer, 0, 0), pipeline_mode=pl.Buffered(1)),
            pl.BlockSpec((tm, HEAD_DIM), lambda i: (i % bps, 0)),
            pl.BlockSpec((tm, HEAD_DIM), lambda i: (i % bps, 0)),
        ],
        out_specs=out_specs,
        out_shape=out_shape,
        scratch_shapes=scratch,
        compiler_params=pltpu.CompilerParams(
            dimension_semantics=("parallel",), vmem_limit_bytes=V7X_VMEM_LIMIT),
        name="in_proj",
    )(x, g.reshape(1, d), w_bf16, cos2, sin2)
    return res if streamed else res[0]


def _mem_proj_kernel(x_ref, g_ref, w_ref, *rest):
    n_casts = (len(rest) - 1) // 2
    o_ref = rest[n_casts]
    for src_ref, dst_ref in zip(rest[:n_casts], rest[n_casts + 1:]):
        dst_ref[...] = src_ref[...].astype(dst_ref.dtype)
    h = _rms_norm_f32(x_ref[...], g_ref[...]).astype(jnp.bfloat16)
    o_ref[...] = jnp.dot(h, w_ref[...], preferred_element_type=jnp.float32).astype(o_ref.dtype)


def _mem_proj(mem, g, w_bf16, layer, casts):
    n, d = mem.shape
    cols = w_bf16.shape[2]
    tm = MEM_PROJ_TM
    steps = n // tm
    in_specs = [
        pl.BlockSpec((tm, d), lambda i: (i, 0)),
        pl.BlockSpec((1, d), lambda i: (0, 0)),
        pl.BlockSpec((None, d, cols), lambda i: (layer, 0, 0), pipeline_mode=pl.Buffered(1)),
    ]
    out_specs = [pl.BlockSpec((tm, cols), lambda i: (i, 0))]
    out_shape = [jax.ShapeDtypeStruct((n, cols), jnp.bfloat16)]
    for w, idx in casts:
        _, r, c = w.shape
        assert r % steps == 0
        in_specs.append(pl.BlockSpec((None, r // steps, c), lambda i, idx=idx: (idx, i, 0)))
        out_specs.append(pl.BlockSpec((r // steps, c), lambda i: (i, 0)))
        out_shape.append(jax.ShapeDtypeStruct((r, c), jnp.bfloat16))
    return pl.pallas_call(
        _mem_proj_kernel,
        grid=(steps,),
        in_specs=in_specs,
        out_specs=out_specs,
        out_shape=out_shape,
        compiler_params=pltpu.CompilerParams(
            dimension_semantics=("parallel",), vmem_limit_bytes=V7X_VMEM_LIMIT),
        name="mem_proj",
    )(mem, g.reshape(1, d), w_bf16, *[w for w, _ in casts])


def _out_proj_kernel(x_ref, yna_ref, ydl_ref, ym_ref, w_ref, g_ref, *rest, final_norm):
    if len(rest) == 3:
        wnext_ref, o_ref, wnext_bf16_ref = rest
        wnext_bf16_ref[...] = wnext_ref[...].astype(wnext_bf16_ref.dtype)
    else:
        (o_ref,) = rest
    acc = jnp.dot(yna_ref[...], w_ref[0:W_NA, :], preferred_element_type=jnp.float32)
    acc += jnp.dot(ydl_ref[...], w_ref[W_NA:W_NA + W_DIL, :], preferred_element_type=jnp.float32)
    acc += jnp.dot(ym_ref[...], w_ref[W_NA + W_DIL:, :], preferred_element_type=jnp.float32)
    x = x_ref[...] + acc
    if final_norm:
        x = _rms_norm_f32(x, g_ref[...])
    o_ref[...] = x


def _out_proj(x, y_na, y_dl, y_m, w_bf16, final_g, final_norm, w_in_f32=None, next_layer=None):
    n, d = x.shape
    tm = OUT_TM
    steps = n // tm
    in_specs = [
        pl.BlockSpec((tm, d), lambda i: (i, 0)),
        pl.BlockSpec((tm, W_NA), lambda i: (i, 0)),
        pl.BlockSpec((tm, W_DIL), lambda i: (i, 0)),
        pl.BlockSpec((tm, W_MEM), lambda i: (i, 0)),
        pl.BlockSpec((d, d), lambda i: (0, 0), pipeline_mode=pl.Buffered(1)),
        pl.BlockSpec((1, d), lambda i: (0, 0)),
    ]
    out_specs = [pl.BlockSpec((tm, d), lambda i: (i, 0))]
    out_shape = [jax.ShapeDtypeStruct((n, d), jnp.float32)]
    args = [x, y_na, y_dl, y_m, w_bf16, final_g.reshape(1, d)]
    if w_in_f32 is not None:
        cols = w_in_f32.shape[2]
        assert d % steps == 0
        slab = d // steps
        in_specs.append(pl.BlockSpec((None, slab, cols), lambda i: (next_layer, i, 0)))
        out_specs.append(pl.BlockSpec((None, slab, cols), lambda i: (0, i, 0)))
        out_shape.append(jax.ShapeDtypeStruct((1, d, cols), jnp.bfloat16))
        args.append(w_in_f32)
    res = pl.pallas_call(
        functools.partial(_out_proj_kernel, final_norm=final_norm),
        grid=(steps,),
        in_specs=in_specs,
        out_specs=out_specs,
        out_shape=out_shape,
        compiler_params=pltpu.CompilerParams(
            dimension_semantics=("parallel",), vmem_limit_bytes=V7X_VMEM_LIMIT),
        name="out_proj",
    )(*args)
    return res if w_in_f32 is not None else res[0]


_Block = collections.namedtuple("_Block", "scores pv finish prior stats depth flush")


def _run_pipeline(blocks):
    issued = []
    pending = None
    for i, blk in enumerate(blocks):
        while len(issued) < min(i + 1 + blk.depth, len(blocks)):
            issued.append(blocks[len(issued)].scores())
        s = issued[i]
        issued[i] = None
        if blk.flush and pending is not None:
            pending[0].finish(*pending[1:])
            pending = None
        m = jnp.max(s, axis=-1, keepdims=True)
        if blk.stats is not None:
            blk.stats(m)
        if blk.prior is None:
            p = jnp.exp2(s - m)
            acc, l = blk.pv(p.astype(jnp.bfloat16))
        else:
            m_b, l_b, acc_b = blk.prior()
            m = jnp.maximum(m, m_b)
            p = jnp.exp2(s - jnp.concatenate([m] * (s.shape[1] // m.shape[1]), axis=1))
            alpha = jnp.exp2(m_b - m)
            acc, l = blk.pv(p.astype(jnp.bfloat16))
            l = alpha * l_b + l
            acc = alpha * acc_b + acc
        if pending is not None:
            pending[0].finish(*pending[1:])
        pending = (blk, l, acc)
    pending[0].finish(*pending[1:])


def _na_kernel(q_ref, k_ref, v_ref, gate_ref, toep_ref, o_ref, bias_ref):
    n_groups = q_ref.shape[0] // NA_Q
    rows = q_ref.shape[0] // GRID_W

    def window_start(r):
        return int(np.clip(r - NA_WIN_ROWS // 2, 0, rows - NA_WIN_ROWS))

    def key_rows(g):
        lo = window_start(g * NA_ROWS_PER_GROUP)
        hi = window_start((g + 1) * NA_ROWS_PER_GROUP - 1) + NA_WIN_ROWS
        n = -(-(hi - lo) // NA_KEY_STEP) * NA_KEY_STEP
        return min(lo, rows - n), n

    @pl.when(pl.program_id(1) == 0)
    def _():
        outside = jnp.full((GRID_W, GRID_W), NEG, jnp.float32)
        for hh in range(q_ref.shape[1] // HEAD_DIM):
            for pat, g in enumerate((0, 1, n_groups - 1)):
                base, n_rows = key_rows(g)
                for qi in range(NA_ROWS_PER_GROUP):
                    r = g * NA_ROWS_PER_GROUP + qi
                    rs = window_start(r)
                    for kj in range(0, n_rows, 2):
                        pair = [toep_ref[hh, base + kk - r + NA_WIN_ROWS - 1]
                                if rs <= base + kk < rs + NA_WIN_ROWS else outside for kk in (kj, kj + 1)]
                        bias_ref[hh, pat, qi * GRID_W:(qi + 1) * GRID_W, kj * GRID_W:(kj + 2) * GRID_W] = (
                            jnp.concatenate(pair, axis=1))

    def block(hh, g):
        cols = _head_cols(hh)
        qrows = slice(g * NA_Q, (g + 1) * NA_Q)
        base, n_rows = key_rows(g)
        krows = slice(base * GRID_W, (base + n_rows) * GRID_W)
        pat = 0 if g == 0 else (2 if g == n_groups - 1 else 1)

        def scores():
            return _dot_nt(q_ref[qrows, cols], k_ref[krows, cols]) + bias_ref[hh, pat, :, 0:n_rows * GRID_W]

        def pv(p):
            return _pv_with_rowsum(p, v_ref[krows, cols])

        def finish(l, acc):
            o = (acc / l) * gate_ref[qrows, cols].astype(jnp.float32)
            o_ref[qrows, cols] = o.astype(o_ref.dtype)

        return _Block(scores, pv, finish, None, None, NA_DEPTH, False)

    _run_pipeline([block(hh, g) for hh in range(q_ref.shape[1] // HEAD_DIM) for g in range(n_groups)])


def _na_toeplitz(rpb):
    c = np.arange(GRID_W)
    cs = np.clip(c - NA_WIN_COLS // 2, 0, GRID_W - NA_WIN_COLS)
    col_ok = (c[None, :] >= cs[:, None]) & (c[None, :] < cs[:, None] + NA_WIN_COLS)
    dc = np.clip(c[None, :] - c[:, None], -(NA_WIN_COLS - 1), NA_WIN_COLS - 1) + NA_WIN_COLS - 1
    onehot = (dc[None] == np.arange(2 * NA_WIN_COLS - 1)[:, None, None]).astype(np.float32)
    toep = jnp.einsum("lhaj,jqk->lhaqk", rpb.astype(jnp.float32), jnp.asarray(onehot),
                      precision=lax.Precision.HIGHEST)
    return jnp.where(col_ok, toep * LOG2E, NEG)


def _na_attention(z, toep, layer, batch, seq):
    n = z.shape[0]
    hc = NA_HPS * HEAD_DIM
    qb, kb, vb, gb = (OFF_NA_Q // hc, OFF_NA_K // hc, OFF_NA_V // hc, OFF_NA_G // hc)
    return pl.pallas_call(
        _na_kernel,
        grid=(N_HEADS_NA // NA_HPS, batch),
        in_specs=[
            pl.BlockSpec((seq, hc), lambda h, b: (b, qb + h)),
            pl.BlockSpec((seq, hc), lambda h, b: (b, kb + h)),
            pl.BlockSpec((seq, hc), lambda h, b: (b, vb + h)),
            pl.BlockSpec((seq, hc), lambda h, b: (b, gb + h)),
            pl.BlockSpec((None, NA_HPS) + toep.shape[2:], lambda h, b: (layer, h, 0, 0, 0)),
        ],
        out_specs=pl.BlockSpec((seq, hc), lambda h, b: (b, h)),
        out_shape=jax.ShapeDtypeStruct((n, W_NA), jnp.bfloat16),
        scratch_shapes=[pltpu.VMEM((NA_HPS, 3, NA_Q, NA_K), jnp.float32)],
        compiler_params=pltpu.CompilerParams(
            dimension_semantics=("parallel", "arbitrary"), vmem_limit_bytes=V7X_VMEM_LIMIT),
        name="na_attention",
    )(z, z, z, z, toep)


def _dl_mem_kernel(qn_ref, kn_ref, vn_ref, gate_ref, qs_ref, ks_ref, vs_ref, bias_a_ref, bias_b_ref,
                   mq_ref, mk_ref, mv_ref, mgate_ref, o_ref, om_ref, m_scr, l_scr, acc_scr):
    seq = qn_ref.shape[0]
    slen = seq // DL_STREAMS

    def block_m(hh, i):
        cols = _head_cols(hh)
        qrows = slice(i * MEM_Q, (i + 1) * MEM_Q)

        def scores():
            return _dot_nt(mq_ref[qrows, cols], mk_ref[:, cols])

        def pv(p):
            return _pv_with_rowsum(p, mv_ref[:, cols])

        def finish(l, acc):
            o = (acc / l) * mgate_ref[qrows, cols].astype(jnp.float32)
            om_ref[qrows, cols] = o.astype(om_ref.dtype)

        return _Block(scores, pv, finish, None, None, MEM_DEPTH, False)

    def block_b(hh, r):
        cols = _head_cols(hh)
        rows = pl.ds(r, slen, stride=DL_STREAMS)

        def scores():
            return _dot_nt(qs_ref[r, :, cols], ks_ref[r, :, cols]) + bias_b_ref[...]

        def pv(p):
            return _pv_with_rowsum(p, vs_ref[r, :, cols])

        def stats(m):
            m_scr[hh, rows, :] = jnp.broadcast_to(m, (slen, HEAD_DIM))

        def finish(l, acc):
            l_scr[hh, rows, :] = l
            acc_scr[hh, rows, :] = acc

        return _Block(scores, pv, finish, None, stats, DL_DEPTH_B, False)

    def block_a(hh, i):
        cols = _head_cols(hh)
        qrows = slice(i * DL_Q, (i + 1) * DL_Q)
        k0 = int(np.clip(i * DL_Q - DL_HALF, 0, seq - DL_KA))
        krows = slice(k0, k0 + DL_KA)
        pat = (i * DL_Q - k0) // DL_HALF

        def scores():
            return _dot_nt(qn_ref[qrows, cols], kn_ref[krows, cols]) + bias_a_ref[pat]

        def pv(p):
            return _pv_with_rowsum(p, vn_ref[krows, cols])

        def prior():
            return m_scr[hh, qrows, :], l_scr[hh, qrows, :], acc_scr[hh, qrows, :]

        def finish(l, acc):
            o = (acc / l) * gate_ref[qrows, cols].astype(jnp.float32)
            o_ref[qrows, cols] = o.astype(o_ref.dtype)

        return _Block(scores, pv, finish, prior, None, DL_DEPTH_A, i == 0)

    heads = range(qn_ref.shape[1] // HEAD_DIM)
    mem_heads = range(mq_ref.shape[1] // HEAD_DIM)
    _run_pipeline([block_b(hh, r) for hh in heads for r in range(DL_STREAMS)]
                  + [block_m(hh, i) for hh in mem_heads for i in range(seq // MEM_Q)]
                  + [block_a(hh, i) for hh in heads for i in range(seq // DL_Q)])


def _dl_bias_tables(seq):
    (w1, d1), (w4, d4), (w16, d16) = DIL_CONFIGS
    assert d1 == 1 and d4 == DL_STREAMS and d16 % d4 == 0
    assert (w1 // 2) // d1 == DL_HALF and (w4 // 2) // d4 == DL_HALF and (w16 // 2) // d16 == DL_HALF
    r = np.arange(DL_Q)[:, None]
    j = np.arange(DL_KA)[None, :]
    bias_a = np.stack([np.where(np.abs(j - off - r) <= DL_HALF, 0.0, NEG) for off in (0, DL_HALF, 2 * DL_HALF)])
    slen = seq // DL_STREAMS
    delta = np.arange(slen)[None, :] - np.arange(slen)[:, None]
    step = d16 // d4
    mult = (np.abs(delta) <= DL_HALF).astype(np.float64) + ((delta % step == 0) & (np.abs(delta) <= DL_HALF * step))
    bias_b = np.where(mult > 0, np.log2(np.maximum(mult, 1.0)), NEG)
    return jnp.asarray(bias_a, jnp.float32), jnp.asarray(bias_b, jnp.float32)


def _dl_mem_attention(z, zs, mkv, bias_a, bias_b, batch, seq):
    n = z.shape[0]
    hc = DL_HPS * HEAD_DIM
    qb, kb, vb, gb = (OFF_DL_Q // hc, OFF_DL_K // hc, OFF_DL_V // hc, OFF_DL_G // hc)
    slen = seq // DL_STREAMS
    steps = N_HEADS_DIL // DL_HPS
    stream_spec = lambda c0: pl.BlockSpec((None, DL_STREAMS, slen, hc), lambda b, h: (b, 0, 0, c0 + h))
    mc = MEM_HPS * HEAD_DIM
    mqb, mgb, mvb = OFF_M_Q // mc, OFF_M_G // mc, W_MEM // mc
    return pl.pallas_call(
        _dl_mem_kernel,
        grid=(batch, steps),
        in_specs=[
            pl.BlockSpec((seq, hc), lambda b, h: (b, qb + h)),
            pl.BlockSpec((seq, hc), lambda b, h: (b, kb + h)),
            pl.BlockSpec((seq, hc), lambda b, h: (b, vb + h)),
            pl.BlockSpec((seq, hc), lambda b, h: (b, gb + h)),
            stream_spec(0), stream_spec(steps), stream_spec(2 * steps),
            pl.BlockSpec(bias_a.shape, lambda b, h: (0, 0, 0)),
            pl.BlockSpec(bias_b.shape, lambda b, h: (0, 0)),
            pl.BlockSpec((seq, mc), lambda b, h: (b, mqb + h)),
            pl.BlockSpec((N_MEM, mc), lambda b, h: (b, h)),
            pl.BlockSpec((N_MEM, mc), lambda b, h: (b, mvb + h)),
            pl.BlockSpec((seq, mc), lambda b, h: (b, mgb + h)),
        ],
        out_specs=[pl.BlockSpec((seq, hc), lambda b, h: (b, h)),
                   pl.BlockSpec((seq, mc), lambda b, h: (b, h))],
        out_shape=[jax.ShapeDtypeStruct((n, W_DIL), jnp.bfloat16),
                   jax.ShapeDtypeStruct((n, W_MEM), jnp.bfloat16)],
        scratch_shapes=[pltpu.VMEM((DL_HPS, seq, HEAD_DIM), jnp.float32)] * 3,
        compiler_params=pltpu.CompilerParams(
            dimension_semantics=("parallel", "parallel"), vmem_limit_bytes=V7X_VMEM_LIMIT),
        name="dl_mem_attention",
    )(z, z, z, z, zs, zs, zs, bias_a, bias_b, z, mkv, mkv, z)


def _rope_tables(seq):
    half = HEAD_DIM // 2
    inv = ROPE_THETA ** (-jnp.arange(half, dtype=jnp.float32) / half)
    ang = jnp.arange(seq).astype(jnp.float32)[:, None] * inv[None, :]
    cos, sin = jnp.cos(ang), jnp.sin(ang)
    return jnp.concatenate([cos, cos], axis=-1), jnp.concatenate([-sin, sin], axis=-1)


def kernel(x, mem, norm_g, w_in, na_rpb, mem_norm_g, w_mem_kv, w_out, final_g):
    batch, seq, d = x.shape
    depth = w_in.shape[0]
    assert d == D_MODEL and w_in.shape[2] == IN_COLS and seq % GRID_W == 0
    assert seq % PROJ_TM == 0 and seq % OUT_TM == 0 and seq % NA_Q == 0 and seq % MEM_Q == 0
    assert seq % (DL_STREAMS * DL_Q) == 0 and PROJ_TM % DL_STREAMS == 0
    assert all(off % (NA_HPS * HEAD_DIM) == 0 for off in (OFF_NA_Q, OFF_NA_K, OFF_NA_V, OFF_NA_G))
    assert all(off % (DL_HPS * HEAD_DIM) == 0 for off in (OFF_DL_Q, OFF_DL_K, OFF_DL_V, OFF_DL_G))
    assert all(off % (MEM_HPS * HEAD_DIM) == 0 for off in (OFF_M_Q, OFF_M_G, W_MEM))
    assert N_HEADS_NA % NA_HPS == 0 and N_HEADS_DIL % DL_HPS == 0
    assert N_HEADS_MEM * DL_HPS % N_HEADS_DIL == 0

    cos2, sin2 = _rope_tables(seq)
    dl_bias_a, dl_bias_b = _dl_bias_tables(seq)
    na_toep = _na_toeplitz(na_rpb)
    w_mem_b = w_mem_kv.astype(jnp.bfloat16)
    mem_f = mem.reshape(batch * N_MEM, d)

    xf = x.reshape(batch * seq, d)
    w_in_b = None
    for l in range(depth):
        casts = [(w_out, l)] + ([(w_in, 0)] if l == 0 else [])
        mkv, w_out_b, *first = _mem_proj(mem_f, mem_norm_g, w_mem_b, l, casts)
        if first:
            w_in_b = first[0][None]
        z, zs = _in_proj(xf, norm_g[l], w_in_b, 0, cos2, sin2, IN_SEGMENTS, batch, seq)
        y_na = _na_attention(z, na_toep, l, batch, seq)
        y_dl, y_m = _dl_mem_attention(z, zs, mkv, dl_bias_a, dl_bias_b, batch, seq)
        if l + 1 < depth:
            xf, w_in_b = _out_proj(xf, y_na, y_dl, y_m, w_out_b, final_g, False, w_in_f32=w_in, next_layer=l + 1)
        else:
            xf = _out_proj(xf, y_na, y_dl, y_m, w_out_b, final_g, True)
    return xf.reshape(batch, seq, d)
```

```python
import collections
import functools

import numpy as np
import jax
import jax.numpy as jnp
from jax import lax
from jax.experimental import pallas as pl
from jax.experimental.pallas import tpu as pltpu

D_MODEL = 2048
HEAD_DIM = 128
N_HEADS_NA = 6
N_HEADS_DIL = 6
N_HEADS_MEM = 4
W_NA = N_HEADS_NA * HEAD_DIM
W_DIL = N_HEADS_DIL * HEAD_DIM
W_MEM = N_HEADS_MEM * HEAD_DIM
IN_COLS = 4 * W_NA + 4 * W_DIL + 2 * W_MEM
N_MEM = 256
GRID_W = 64
NA_WIN_ROWS = 8
NA_WIN_COLS = 16
DIL_CONFIGS = ((128, 1), (512, 4), (2048, 16))
ROPE_THETA = 10000.0
EPS = 1e-6
NEG = -1e30
SCALE = HEAD_DIM ** -0.5
LOG2E = float(np.log2(np.e))
Q_SCALE = SCALE * LOG2E

OFF_NA_Q, OFF_NA_K, OFF_NA_V, OFF_NA_G = 0, W_NA, 2 * W_NA, 3 * W_NA
OFF_DL_Q = 4 * W_NA
OFF_DL_K, OFF_DL_V, OFF_DL_G = OFF_DL_Q + W_DIL, OFF_DL_Q + 2 * W_DIL, OFF_DL_Q + 3 * W_DIL
OFF_M_Q = OFF_DL_Q + 4 * W_DIL
OFF_M_G = OFF_M_Q + W_MEM

IN_SEGMENTS = (
    (OFF_NA_Q, W_NA, "scale", None), (OFF_NA_K, W_NA, "plain", None),
    (OFF_NA_V, W_NA, "plain", None), (OFF_NA_G, W_NA, "silu", None),
    (OFF_DL_Q, W_DIL, "rope_scale", 0), (OFF_DL_K, W_DIL, "rope", W_DIL),
    (OFF_DL_V, W_DIL, "plain", 2 * W_DIL), (OFF_DL_G, W_DIL, "silu", None),
    (OFF_M_Q, W_MEM, "scale", None), (OFF_M_G, W_MEM, "silu", None),
)

V7X_VMEM_LIMIT = 62 * 1024 * 1024

PROJ_TM = 512
OUT_TM = 1024
MEM_PROJ_TM = 512
NA_HPS, DL_HPS = 6, 3
MEM_HPS = N_HEADS_MEM * DL_HPS // N_HEADS_DIL
NA_ROWS_PER_GROUP = 4
NA_Q = NA_ROWS_PER_GROUP * GRID_W
NA_KEY_ROWS = NA_ROWS_PER_GROUP + NA_WIN_ROWS
NA_K = NA_KEY_ROWS * GRID_W
NA_KEY_STEP = 4
NA_DEPTH = 2
MEM_Q = 512
MEM_DEPTH = 1

DL_STREAMS = 4
DL_HALF = 64
DL_Q = 128
DL_KA = DL_Q + 2 * DL_HALF
DL_DEPTH_A = 4
DL_DEPTH_B = 2


def _rms_norm_f32(x, g):
    return x * lax.rsqrt(jnp.mean(x * x, axis=-1, keepdims=True) + EPS) * g


def _dot_nt(a, b):
    return lax.dot_general(a, b, (((1,), (1,)), ((), ())), preferred_element_type=jnp.float32)


def _pv_with_rowsum(p, v):
    ones = jnp.ones((v.shape[0], HEAD_DIM), v.dtype)
    r = jnp.dot(p, jnp.concatenate([v, ones], axis=1), preferred_element_type=jnp.float32)
    return r[:, :HEAD_DIM], r[:, HEAD_DIM:]


def _head_cols(hh):
    return slice(hh * HEAD_DIM, (hh + 1) * HEAD_DIM)


def _in_proj_kernel(x_ref, g_ref, w_ref, cos_ref, sin_ref, o_ref, *rest, segments):
    if rest:
        s_ref, scr = rest
    tm = x_ref.shape[0]
    h = _rms_norm_f32(x_ref[...], g_ref[...]).astype(jnp.bfloat16)

    def emit(a, c0, width, s0):
        o_ref[:, c0:c0 + width] = a.astype(o_ref.dtype)
        if s0 is not None:
            for j in range(0, width, HEAD_DIM):
                scr[j // HEAD_DIM] = a[:, j:j + HEAD_DIM]
                for r in range(DL_STREAMS):
                    rows = scr[j // HEAD_DIM, pl.ds(r, tm // DL_STREAMS, stride=DL_STREAMS), :]
                    s_ref[r, :, s0 + j:s0 + j + HEAD_DIM] = rows.astype(s_ref.dtype)

    for off, width, kind, s_off in segments:
        a = jnp.dot(h, w_ref[:, off:off + width], preferred_element_type=jnp.float32)
        if kind in ("rope", "rope_scale"):
            for j in range(0, width, HEAD_DIM):
                aj = a[:, j:j + HEAD_DIM]
                aj = aj * cos_ref[...] + pltpu.roll(aj, HEAD_DIM // 2, axis=1) * sin_ref[...]
                if kind == "rope_scale":
                    aj = aj * Q_SCALE
                emit(aj, off + j, HEAD_DIM, None if s_off is None else s_off + j)
        else:
            if kind == "scale":
                a = a * Q_SCALE
            elif kind == "silu":
                a = a * (1.0 / (1.0 + jnp.exp(-a)))
            emit(a, off, width, s_off)


def _in_proj(x, g, w_bf16, layer, cos2, sin2, segments, batch, seq):
    n, d = x.shape
    cols = w_bf16.shape[2]
    tm = PROJ_TM
    bps = seq // tm
    streamed = any(s[3] is not None for s in segments)
    out_specs = [pl.BlockSpec((tm, cols), lambda i: (i, 0))]
    out_shape = [jax.ShapeDtypeStruct((n, cols), jnp.bfloat16)]
    scratch = []
    if streamed:
        out_specs.append(pl.BlockSpec((None, DL_STREAMS, tm // DL_STREAMS, 3 * W_DIL),
                                      lambda i: (i // bps, 0, i % bps, 0)))
        out_shape.append(jax.ShapeDtypeStruct((batch, DL_STREAMS, seq // DL_STREAMS, 3 * W_DIL), jnp.bfloat16))
        scratch.append(pltpu.VMEM((W_DIL // HEAD_DIM, tm, HEAD_DIM), jnp.float32))
    res = pl.pallas_call(
        functools.partial(_in_proj_kernel, segments=segments),
        grid=(n // tm,),
        in_specs=[
            pl.BlockSpec((tm, d), lambda i: (i, 0)),
            pl.BlockSpec((1, d), lambda i: (0, 0)),
            pl.BlockSpec((None, d, cols), lambda i: (layer, 0, 0), pipeline_mode=pl.Buffered(1)),
            pl.BlockSpec((tm, HEAD_DIM), lambda i: (i % bps, 0)),
            pl.BlockSpec((tm, HEAD_DIM), lambda i: (i % bps, 0)),
        ],
        out_specs=out_specs,
        out_shape=out_shape,
        scratch_shapes=scratch,
        compiler_params=pltpu.CompilerParams(
            dimension_semantics=("parallel",), vmem_limit_bytes=V7X_VMEM_LIMIT),
        name="in_proj",
    )(x, g.reshape(1, d), w_bf16, cos2, sin2)
    return res if streamed else res[0]


def _mem_proj_kernel(x_ref, g_ref, w_ref, *rest):
    n_casts = (len(rest) - 1) // 2
    o_ref = rest[n_casts]
    for src_ref, dst_ref in zip(rest[:n_casts], rest[n_casts + 1:]):
        dst_ref[...] = src_ref[...].astype(dst_ref.dtype)
    h = _rms_norm_f32(x_ref[...], g_ref[...]).astype(jnp.bfloat16)
    o_ref[...] = jnp.dot(h, w_ref[...], preferred_element_type=jnp.float32).astype(o_ref.dtype)


def _mem_proj(mem, g, w_bf16, layer, casts):
    n, d = mem.shape
    cols = w_bf16.shape[2]
    tm = MEM_PROJ_TM
    steps = n // tm
    in_specs = [
        pl.BlockSpec((tm, d), lambda i: (i, 0)),
        pl.BlockSpec((1, d), lambda i: (0, 0)),
        pl.BlockSpec((None, d, cols), lambda i: (layer, 0, 0), pipeline_mode=pl.Buffered(1)),
    ]
    out_specs = [pl.BlockSpec((tm, cols), lambda i: (i, 0))]
    out_shape = [jax.ShapeDtypeStruct((n, cols), jnp.bfloat16)]
    for w, idx in casts:
        _, r, c = w.shape
        assert r % steps == 0
        in_specs.append(pl.BlockSpec((None, r // steps, c), lambda i, idx=idx: (idx, i, 0)))
        out_specs.append(pl.BlockSpec((r // steps, c), lambda i: (i, 0)))
        out_shape.append(jax.ShapeDtypeStruct((r, c), jnp.bfloat16))
    return pl.pallas_call(
        _mem_proj_kernel,
        grid=(steps,),
        in_specs=in_specs,
        out_specs=out_specs,
        out_shape=out_shape,
        compiler_params=pltpu.CompilerParams(
            dimension_semantics=("parallel",), vmem_limit_bytes=V7X_VMEM_LIMIT),
        name="mem_proj",
    )(mem, g.reshape(1, d), w_bf16, *[w for w, _ in casts])


def _out_proj_kernel(x_ref, yna_ref, ydl_ref, ym_ref, w_ref, g_ref, *rest, final_norm):
    if len(rest) == 3:
        wnext_ref, o_ref, wnext_bf16_ref = rest
        wnext_bf16_ref[...] = wnext_ref[...].astype(wnext_bf16_ref.dtype)
    else:
        (o_ref,) = rest
    acc = jnp.dot(yna_ref[...], w_ref[0:W_NA, :], preferred_element_type=jnp.float32)
    acc += jnp.dot(ydl_ref[...], w_ref[W_NA:W_NA + W_DIL, :], preferred_element_type=jnp.float32)
    acc += jnp.dot(ym_ref[...], w_ref[W_NA + W_DIL:, :], preferred_element_type=jnp.float32)
    x = x_ref[...] + acc
    if final_norm:
        x = _rms_norm_f32(x, g_ref[...])
    o_ref[...] = x


def _out_proj(x, y_na, y_dl, y_m, w_bf16, final_g, final_norm, w_in_f32=None, next_layer=None):
    n, d = x.shape
    tm = OUT_TM
    steps = n // tm
    in_specs = [
        pl.BlockSpec((tm, d), lambda i: (i, 0)),
        pl.BlockSpec((tm, W_NA), lambda i: (i, 0)),
        pl.BlockSpec((tm, W_DIL), lambda i: (i, 0)),
        pl.BlockSpec((tm, W_MEM), lambda i: (i, 0)),
        pl.BlockSpec((d, d), lambda i: (0, 0), pipeline_mode=pl.Buffered(1)),
        pl.BlockSpec((1, d), lambda i: (0, 0)),
    ]
    out_specs = [pl.BlockSpec((tm, d), lambda i: (i, 0))]
    out_shape = [jax.ShapeDtypeStruct((n, d), jnp.float32)]
    args = [x, y_na, y_dl, y_m, w_bf16, final_g.reshape(1, d)]
    if w_in_f32 is not None:
        cols = w_in_f32.shape[2]
        assert d % steps == 0
        slab = d // steps
        in_specs.append(pl.BlockSpec((None, slab, cols), lambda i: (next_layer, i, 0)))
        out_specs.append(pl.BlockSpec((None, slab, cols), lambda i: (0, i, 0)))
        out_shape.append(jax.ShapeDtypeStruct((1, d, cols), jnp.bfloat16))
        args.append(w_in_f32)
    res = pl.pallas_call(
        functools.partial(_out_proj_kernel, final_norm=final_norm),
        grid=(steps,),
        in_specs=in_specs,
        out_specs=out_specs,
        out_shape=out_shape,
        compiler_params=pltpu.CompilerParams(
            dimension_semantics=("parallel",), vmem_limit_bytes=V7X_VMEM_LIMIT),
        name="out_proj",
    )(*args)
    return res if w_in_f32 is not None else res[0]


_Block = collections.namedtuple("_Block", "scores pv finish prior stats depth flush")


def _run_pipeline(blocks):
    issued = []
    pending = None
    for i, blk in enumerate(blocks):
        while len(issued) < min(i + 1 + blk.depth, len(blocks)):
            issued.append(blocks[len(issued)].scores())
        s = issued[i]
        issued[i] = None
        if blk.flush and pending is not None:
            pending[0].finish(*pending[1:])
            pending = None
        m = jnp.max(s, axis=-1, keepdims=True)
        if blk.stats is not None:
            blk.stats(m)
        if blk.prior is None:
            p = jnp.exp2(s - m)
            acc, l = blk.pv(p.astype(jnp.bfloat16))
        else:
            m_b, l_b, acc_b = blk.prior()
            m = jnp.maximum(m, m_b)
            p = jnp.exp2(s - jnp.concatenate([m] * (s.shape[1] // m.shape[1]), axis=1))
            alpha = jnp.exp2(m_b - m)
            acc, l = blk.pv(p.astype(jnp.bfloat16))
            l = alpha * l_b + l
            acc = alpha * acc_b + acc
        if pending is not None:
            pending[0].finish(*pending[1:])
        pending = (blk, l, acc)
    pending[0].finish(*pending[1:])


def _na_kernel(q_ref, k_ref, v_ref, gate_ref, toep_ref, o_ref, bias_ref):
    n_groups = q_ref.shape[0] // NA_Q
    rows = q_ref.shape[0] // GRID_W

    def window_start(r):
        return int(np.clip(r - NA_WIN_ROWS // 2, 0, rows - NA_WIN_ROWS))

    def key_rows(g):
        lo = window_start(g * NA_ROWS_PER_GROUP)
        hi = window_start((g + 1) * NA_ROWS_PER_GROUP - 1) + NA_WIN_ROWS
        n = -(-(hi - lo) // NA_KEY_STEP) * NA_KEY_STEP
        return min(lo, rows - n), n

    @pl.when(pl.program_id(1) == 0)
    def _():
        outside = jnp.full((GRID_W, GRID_W), NEG, jnp.float32)
        for hh in range(q_ref.shape[1] // HEAD_DIM):
            for pat, g in enumerate((0, 1, n_groups - 1)):
                base, n_rows = key_rows(g)
                for qi in range(NA_ROWS_PER_GROUP):
                    r = g * NA_ROWS_PER_GROUP + qi
                    rs = window_start(r)
                    for kj in range(0, n_rows, 2):
                        pair = [toep_ref[hh, base + kk - r + NA_WIN_ROWS - 1]
                                if rs <= base + kk < rs + NA_WIN_ROWS else outside for kk in (kj, kj + 1)]
                        bias_ref[hh, pat, qi * GRID_W:(qi + 1) * GRID_W, kj * GRID_W:(kj + 2) * GRID_W] = (
                            jnp.concatenate(pair, axis=1))

    def block(hh, g):
        cols = _head_cols(hh)
        qrows = slice(g * NA_Q, (g + 1) * NA_Q)
        base, n_rows = key_rows(g)
        krows = slice(base * GRID_W, (base + n_rows) * GRID_W)
        pat = 0 if g == 0 else (2 if g == n_groups - 1 else 1)

        def scores():
            return _dot_nt(q_ref[qrows, cols], k_ref[krows, cols]) + bias_ref[hh, pat, :, 0:n_rows * GRID_W]

        def pv(p):
            return _pv_with_rowsum(p, v_ref[krows, cols])

        def finish(l, acc):
            o = (acc / l) * gate_ref[qrows, cols].astype(jnp.float32)
            o_ref[qrows, cols] = o.astype(o_ref.dtype)

        return _Block(scores, pv, finish, None, None, NA_DEPTH, False)

    _run_pipeline([block(hh, g) for hh in range(q_ref.shape[1] // HEAD_DIM) for g in range(n_groups)])


def _na_toeplitz(rpb):
    c = np.arange(GRID_W)
    cs = np.clip(c - NA_WIN_COLS // 2, 0, GRID_W - NA_WIN_COLS)
    col_ok = (c[None, :] >= cs[:, None]) & (c[None, :] < cs[:, None] + NA_WIN_COLS)
    dc = np.clip(c[None, :] - c[:, None], -(NA_WIN_COLS - 1), NA_WIN_COLS - 1) + NA_WIN_COLS - 1
    onehot = (dc[None] == np.arange(2 * NA_WIN_COLS - 1)[:, None, None]).astype(np.float32)
    toep = jnp.einsum("lhaj,jqk->lhaqk", rpb.astype(jnp.float32), jnp.asarray(onehot),
                      precision=lax.Precision.HIGHEST)
    return jnp.where(col_ok, toep * LOG2E, NEG)


def _na_attention(z, toep, layer, batch, seq):
    n = z.shape[0]
    hc = NA_HPS * HEAD_DIM
    qb, kb, vb, gb = (OFF_NA_Q // hc, OFF_NA_K // hc, OFF_NA_V // hc, OFF_NA_G // hc)
    return pl.pallas_call(
        _na_kernel,
        grid=(N_HEADS_NA // NA_HPS, batch),
        in_specs=[
            pl.BlockSpec((seq, hc), lambda h, b: (b, qb + h)),
            pl.BlockSpec((seq, hc), lambda h, b: (b, kb + h)),
            pl.BlockSpec((seq, hc), lambda h, b: (b, vb + h)),
            pl.BlockSpec((seq, hc), lambda h, b: (b, gb + h)),
            pl.BlockSpec((None, NA_HPS) + toep.shape[2:], lambda h, b: (layer, h, 0, 0, 0)),
        ],
        out_specs=pl.BlockSpec((seq, hc), lambda h, b: (b, h)),
        out_shape=jax.ShapeDtypeStruct((n, W_NA), jnp.bfloat16),
        scratch_shapes=[pltpu.VMEM((NA_HPS, 3, NA_Q, NA_K), jnp.float32)],
        compiler_params=pltpu.CompilerParams(
            dimension_semantics=("parallel", "arbitrary"), vmem_limit_bytes=V7X_VMEM_LIMIT),
        name="na_attention",
    )(z, z, z, z, toep)


def _dl_mem_kernel(qn_ref, kn_ref, vn_ref, gate_ref, qs_ref, ks_ref, vs_ref, bias_a_ref, bias_b_ref,
                   mq_ref, mk_ref, mv_ref, mgate_ref, o_ref, om_ref, m_scr, l_scr, acc_scr):
    seq = qn_ref.shape[0]
    slen = seq // DL_STREAMS

    def block_m(hh, i):
        cols = _head_cols(hh)
        qrows = slice(i * MEM_Q, (i + 1) * MEM_Q)

        def scores():
            return _dot_nt(mq_ref[qrows, cols], mk_ref[:, cols])

        def pv(p):
            return _pv_with_rowsum(p, mv_ref[:, cols])

        def finish(l, acc):
            o = (acc / l) * mgate_ref[qrows, cols].astype(jnp.float32)
            om_ref[qrows, cols] = o.astype(om_ref.dtype)

        return _Block(scores, pv, finish, None, None, MEM_DEPTH, False)

    def block_b(hh, r):
        cols = _head_cols(hh)
        rows = pl.ds(r, slen, stride=DL_STREAMS)

        def scores():
            return _dot_nt(qs_ref[r, :, cols], ks_ref[r, :, cols]) + bias_b_ref[...]

        def pv(p):
            return _pv_with_rowsum(p, vs_ref[r, :, cols])

        def stats(m):
            m_scr[hh, rows, :] = jnp.broadcast_to(m, (slen, HEAD_DIM))

        def finish(l, acc):
            l_scr[hh, rows, :] = l
            acc_scr[hh, rows, :] = acc

        return _Block(scores, pv, finish, None, stats, DL_DEPTH_B, False)

    def block_a(hh, i):
        cols = _head_cols(hh)
        qrows = slice(i * DL_Q, (i + 1) * DL_Q)
        k0 = int(np.clip(i * DL_Q - DL_HALF, 0, seq - DL_KA))
        krows = slice(k0, k0 + DL_KA)
        pat = (i * DL_Q - k0) // DL_HALF

        def scores():
            return _dot_nt(qn_ref[qrows, cols], kn_ref[krows, cols]) + bias_a_ref[pat]

        def pv(p):
            return _pv_with_rowsum(p, vn_ref[krows, cols])

        def prior():
            return m_scr[hh, qrows, :], l_scr[hh, qrows, :], acc_scr[hh, qrows, :]

        def finish(l, acc):
            o = (acc / l) * gate_ref[qrows, cols].astype(jnp.float32)
            o_ref[qrows, cols] = o.astype(o_ref.dtype)

        return _Block(scores, pv, finish, prior, None, DL_DEPTH_A, i == 0)

    heads = range(qn_ref.shape[1] // HEAD_DIM)
    mem_heads = range(mq_ref.shape[1] // HEAD_DIM)
    _run_pipeline([block_m(hh, i) for hh in mem_heads for i in range(seq // MEM_Q)]
                  + [block_b(hh, r) for hh in heads for r in range(DL_STREAMS)]
                  + [block_a(hh, i) for hh in heads for i in range(seq // DL_Q)])


def _dl_bias_tables(seq):
    (w1, d1), (w4, d4), (w16, d16) = DIL_CONFIGS
    assert d1 == 1 and d4 == DL_STREAMS and d16 % d4 == 0
    assert (w1 // 2) // d1 == DL_HALF and (w4 // 2) // d4 == DL_HALF and (w16 // 2) // d16 == DL_HALF
    r = np.arange(DL_Q)[:, None]
    j = np.arange(DL_KA)[None, :]
    bias_a = np.stack([np.where(np.abs(j - off - r) <= DL_HALF, 0.0, NEG) for off in (0, DL_HALF, 2 * DL_HALF)])
    slen = seq // DL_STREAMS
    delta = np.arange(slen)[None, :] - np.arange(slen)[:, None]
    step = d16 // d4
    mult = (np.abs(delta) <= DL_HALF).astype(np.float64) + ((delta % step == 0) & (np.abs(delta) <= DL_HALF * step))
    bias_b = np.where(mult > 0, np.log2(np.maximum(mult, 1.0)), NEG)
    return jnp.asarray(bias_a, jnp.float32), jnp.asarray(bias_b, jnp.float32)


def _dl_mem_attention(z, zs, mkv, bias_a, bias_b, batch, seq):
    n = z.shape[0]
    hc = DL_HPS * HEAD_DIM
    qb, kb, vb, gb = (OFF_DL_Q // hc, OFF_DL_K // hc, OFF_DL_V // hc, OFF_DL_G // hc)
    slen = seq // DL_STREAMS
    steps = N_HEADS_DIL // DL_HPS
    stream_spec = lambda c0: pl.BlockSpec((None, DL_STREAMS, slen, hc), lambda b, h: (b, 0, 0, c0 + h))
    mc = MEM_HPS * HEAD_DIM
    mqb, mgb, mvb = OFF_M_Q // mc, OFF_M_G // mc, W_MEM // mc
    return pl.pallas_call(
        _dl_mem_kernel,
        grid=(batch, steps),
        in_specs=[
            pl.BlockSpec((seq, hc), lambda b, h: (b, qb + h)),
            pl.BlockSpec((seq, hc), lambda b, h: (b, kb + h)),
            pl.BlockSpec((seq, hc), lambda b, h: (b, vb + h)),
            pl.BlockSpec((seq, hc), lambda b, h: (b, gb + h)),
            stream_spec(0), stream_spec(steps), stream_spec(2 * steps),
            pl.BlockSpec(bias_a.shape, lambda b, h: (0, 0, 0)),
            pl.BlockSpec(bias_b.shape, lambda b, h: (0, 0)),
            pl.BlockSpec((seq, mc), lambda b, h: (b, mqb + h)),
            pl.BlockSpec((N_MEM, mc), lambda b, h: (b, h)),
            pl.BlockSpec((N_MEM, mc), lambda b, h: (b, mvb + h)),
            pl.BlockSpec((seq, mc), lambda b, h: (b, mgb + h)),
        ],
        out_specs=[pl.BlockSpec((seq, hc), lambda b, h: (b, h)),
                   pl.BlockSpec((seq, mc), lambda b, h: (b, h))],
        out_shape=[jax.ShapeDtypeStruct((n, W_DIL), jnp.bfloat16),
                   jax.ShapeDtypeStruct((n, W_MEM), jnp.bfloat16)],
        scratch_shapes=[pltpu.VMEM((DL_HPS, seq, HEAD_DIM), jnp.float32)] * 3,
        compiler_params=pltpu.CompilerParams(
            dimension_semantics=("parallel", "parallel"), vmem_limit_bytes=V7X_VMEM_LIMIT),
        name="dl_mem_attention",
    )(z, z, z, z, zs, zs, zs, bias_a, bias_b, z, mkv, mkv, z)


def _rope_tables(seq):
    half = HEAD_DIM // 2
    inv = ROPE_THETA ** (-jnp.arange(half, dtype=jnp.float32) / half)
    ang = jnp.arange(seq).astype(jnp.float32)[:, None] * inv[None, :]
    cos, sin = jnp.cos(ang), jnp.sin(ang)
    return jnp.concatenate([cos, cos], axis=-1), jnp.concatenate([-sin, sin], axis=-1)


def kernel(x, mem, norm_g, w_in, na_rpb, mem_norm_g, w_mem_kv, w_out, final_g):
    batch, seq, d = x.shape
    depth = w_in.shape[0]
    assert d == D_MODEL and w_in.shape[2] == IN_COLS and seq % GRID_W == 0
    assert seq % PROJ_TM == 0 and seq % OUT_TM == 0 and seq % NA_Q == 0 and seq % MEM_Q == 0
    assert seq % (DL_STREAMS * DL_Q) == 0 and PROJ_TM % DL_STREAMS == 0
    assert all(off % (NA_HPS * HEAD_DIM) == 0 for off in (OFF_NA_Q, OFF_NA_K, OFF_NA_V, OFF_NA_G))
    assert all(off % (DL_HPS * HEAD_DIM) == 0 for off in (OFF_DL_Q, OFF_DL_K, OFF_DL_V, OFF_DL_G))
    assert all(off % (MEM_HPS * HEAD_DIM) == 0 for off in (OFF_M_Q, OFF_M_G, W_MEM))
    assert N_HEADS_NA % NA_HPS == 0 and N_HEADS_DIL % DL_HPS == 0
    assert N_HEADS_MEM * DL_HPS % N_HEADS_DIL == 0

    cos2, sin2 = _rope_tables(seq)
    dl_bias_a, dl_bias_b = _dl_bias_tables(seq)
    na_toep = _na_toeplitz(na_rpb)
    w_mem_b = w_mem_kv.astype(jnp.bfloat16)
    mem_f = mem.reshape(batch * N_MEM, d)

    xf = x.reshape(batch * seq, d)
    w_in_b = None
    for l in range(depth):
        casts = [(w_out, l)] + ([(w_in, 0)] if l == 0 else [])
        mkv, w_out_b, *first = _mem_proj(mem_f, mem_norm_g, w_mem_b, l, casts)
        if first:
            w_in_b = first[0][None]
        z, zs = _in_proj(xf, norm_g[l], w_in_b, 0, cos2, sin2, IN_SEGMENTS, batch, seq)
        y_na = _na_attention(z, na_toep, l, batch, seq)
        y_dl, y_m = _dl_mem_attention(z, zs, mkv, dl_bias_a, dl_bias_b, batch, seq)
        if l + 1 < depth:
            xf, w_in_b = _out_proj(xf, y_na, y_dl, y_m, w_out_b, final_g, False, w_in_f32=w_in, next_layer=l + 1)
        else:
            xf = _out_proj(xf, y_na, y_dl, y_m, w_out_b, final_g, True)
    return xf.reshape(batch, seq, d)
```

```python
import collections
import functools

import numpy as np
import jax
import jax.numpy as jnp
from jax import lax
from jax.experimental import pallas as pl
from jax.experimental.pallas import tpu as pltpu

D_MODEL = 2048
HEAD_DIM = 128
N_HEADS_NA = 6
N_HEADS_DIL = 6
N_HEADS_MEM = 4
W_NA = N_HEADS_NA * HEAD_DIM
W_DIL = N_HEADS_DIL * HEAD_DIM
W_MEM = N_HEADS_MEM * HEAD_DIM
IN_COLS = 4 * W_NA + 4 * W_DIL + 2 * W_MEM
N_MEM = 256
GRID_W = 64
NA_WIN_ROWS = 8
NA_WIN_COLS = 16
DIL_CONFIGS = ((128, 1), (512, 4), (2048, 16))
ROPE_THETA = 10000.0
EPS = 1e-6
NEG = -1e30
SCALE = HEAD_DIM ** -0.5
LOG2E = float(np.log2(np.e))
Q_SCALE = SCALE * LOG2E

OFF_NA_Q, OFF_NA_K, OFF_NA_V, OFF_NA_G = 0, W_NA, 2 * W_NA, 3 * W_NA
OFF_DL_Q = 4 * W_NA
OFF_DL_K, OFF_DL_V, OFF_DL_G = OFF_DL_Q + W_DIL, OFF_DL_Q + 2 * W_DIL, OFF_DL_Q + 3 * W_DIL
OFF_M_Q = OFF_DL_Q + 4 * W_DIL
OFF_M_G = OFF_M_Q + W_MEM

IN_SEGMENTS = (
    (OFF_NA_Q, W_NA, "scale", None), (OFF_NA_K, W_NA, "plain", None),
    (OFF_NA_V, W_NA, "plain", None), (OFF_NA_G, W_NA, "silu", None),
    (OFF_DL_Q, W_DIL, "rope_scale", 0), (OFF_DL_K, W_DIL, "rope", W_DIL),
    (OFF_DL_V, W_DIL, "plain", 2 * W_DIL), (OFF_DL_G, W_DIL, "silu", None),
    (OFF_M_Q, W_MEM, "scale", None), (OFF_M_G, W_MEM, "silu", None),
)

V7X_VMEM_LIMIT = 62 * 1024 * 1024

PROJ_TM = 512
OUT_TM = 1024
OUT_CHUNKS = 4
MEM_PROJ_TM = 512
NA_HPS, DL_HPS = 6, 3
MEM_HPS = N_HEADS_MEM * DL_HPS // N_HEADS_DIL
NA_ROWS_PER_GROUP = 4
NA_Q = NA_ROWS_PER_GROUP * GRID_W
NA_KEY_ROWS = NA_ROWS_PER_GROUP + NA_WIN_ROWS
NA_K = NA_KEY_ROWS * GRID_W
NA_KEY_STEP = 4
NA_DEPTH = 2
MEM_Q = 512
MEM_DEPTH = 1

DL_STREAMS = 4
DL_HALF = 64
DL_Q = 128
DL_KA = DL_Q + 2 * DL_HALF
DL_DEPTH_A = 4
DL_DEPTH_B = 2


def _rms_norm_f32(x, g):
    return x * lax.rsqrt(jnp.mean(x * x, axis=-1, keepdims=True) + EPS) * g


def _dot_nt(a, b):
    return lax.dot_general(a, b, (((1,), (1,)), ((), ())), preferred_element_type=jnp.float32)


def _pv_with_rowsum(p, v):
    ones = jnp.ones((v.shape[0], HEAD_DIM), v.dtype)
    r = jnp.dot(p, jnp.concatenate([v, ones], axis=1), preferred_element_type=jnp.float32)
    return r[:, :HEAD_DIM], r[:, HEAD_DIM:]


def _head_cols(hh):
    return slice(hh * HEAD_DIM, (hh + 1) * HEAD_DIM)


def _in_proj_kernel(x_ref, g_ref, w_ref, cos_ref, sin_ref, o_ref, *rest, segments):
    if rest:
        s_ref, scr = rest
    tm = x_ref.shape[0]
    h = _rms_norm_f32(x_ref[...], g_ref[...]).astype(jnp.bfloat16)

    def emit(a, c0, width, s0):
        o_ref[:, c0:c0 + width] = a.astype(o_ref.dtype)
        if s0 is not None:
            for j in range(0, width, HEAD_DIM):
                scr[j // HEAD_DIM] = a[:, j:j + HEAD_DIM]
                for r in range(DL_STREAMS):
                    rows = scr[j // HEAD_DIM, pl.ds(r, tm // DL_STREAMS, stride=DL_STREAMS), :]
                    s_ref[r, :, s0 + j:s0 + j + HEAD_DIM] = rows.astype(s_ref.dtype)

    for off, width, kind, s_off in segments:
        a = jnp.dot(h, w_ref[:, off:off + width], preferred_element_type=jnp.float32)
        if kind in ("rope", "rope_scale"):
            for j in range(0, width, HEAD_DIM):
                aj = a[:, j:j + HEAD_DIM]
                aj = aj * cos_ref[...] + pltpu.roll(aj, HEAD_DIM // 2, axis=1) * sin_ref[...]
                if kind == "rope_scale":
                    aj = aj * Q_SCALE
                emit(aj, off + j, HEAD_DIM, None if s_off is None else s_off + j)
        else:
            if kind == "scale":
                a = a * Q_SCALE
            elif kind == "silu":
                a = a * (1.0 / (1.0 + jnp.exp(-a)))
            emit(a, off, width, s_off)


def _in_proj(x, g, w_bf16, layer, cos2, sin2, segments, batch, seq):
    n, d = x.shape
    cols = w_bf16.shape[2]
    tm = PROJ_TM
    bps = seq // tm
    streamed = any(s[3] is not None for s in segments)
    out_specs = [pl.BlockSpec((tm, cols), lambda i: (i, 0))]
    out_shape = [jax.ShapeDtypeStruct((n, cols), jnp.bfloat16)]
    scratch = []
    if streamed:
        out_specs.append(pl.BlockSpec((None, DL_STREAMS, tm // DL_STREAMS, 3 * W_DIL),
                                      lambda i: (i // bps, 0, i % bps, 0)))
        out_shape.append(jax.ShapeDtypeStruct((batch, DL_STREAMS, seq // DL_STREAMS, 3 * W_DIL), jnp.bfloat16))
        scratch.append(pltpu.VMEM((W_DIL // HEAD_DIM, tm, HEAD_DIM), jnp.float32))
    res = pl.pallas_call(
        functools.partial(_in_proj_kernel, segments=segments),
        grid=(n // tm,),
        in_specs=[
            pl.BlockSpec((tm, d), lambda i: (i, 0)),
            pl.BlockSpec((1, d), lambda i: (0, 0)),
            pl.BlockSpec((None, d, cols), lambda i: (layer, 0, 0), pipeline_mode=pl.Buffered(1)),
            pl.BlockSpec((tm, HEAD_DIM), lambda i: (i % bps, 0)),
            pl.BlockSpec((tm, HEAD_DIM), lambda i: (i % bps, 0)),
        ],
        out_specs=out_specs,
        out_shape=out_shape,
        scratch_shapes=scratch,
        compiler_params=pltpu.CompilerParams(
            dimension_semantics=("parallel",), vmem_limit_bytes=V7X_VMEM_LIMIT),
        name="in_proj",
    )(x, g.reshape(1, d), w_bf16, cos2, sin2)
    return res if streamed else res[0]


def _mem_proj_kernel(x_ref, g_ref, w_ref, *rest):
    n_casts = (len(rest) - 1) // 2
    o_ref = rest[n_casts]
    for src_ref, dst_ref in zip(rest[:n_casts], rest[n_casts + 1:]):
        dst_ref[...] = src_ref[...].astype(dst_ref.dtype)
    h = _rms_norm_f32(x_ref[...], g_ref[...]).astype(jnp.bfloat16)
    o_ref[...] = jnp.dot(h, w_ref[...], preferred_element_type=jnp.float32).astype(o_ref.dtype)


def _mem_proj(mem, g, w_bf16, layer, casts):
    n, d = mem.shape
    cols = w_bf16.shape[2]
    tm = MEM_PROJ_TM
    steps = n // tm
    in_specs = [
        pl.BlockSpec((tm, d), lambda i: (i, 0)),
        pl.BlockSpec((1, d), lambda i: (0, 0)),
        pl.BlockSpec((None, d, cols), lambda i: (layer, 0, 0), pipeline_mode=pl.Buffered(1)),
    ]
    out_specs = [pl.BlockSpec((tm, cols), lambda i: (i, 0))]
    out_shape = [jax.ShapeDtypeStruct((n, cols), jnp.bfloat16)]
    for w, idx in casts:
        _, r, c = w.shape
        assert r % steps == 0
        in_specs.append(pl.BlockSpec((None, r // steps, c), lambda i, idx=idx: (idx, i, 0)))
        out_specs.append(pl.BlockSpec((r // steps, c), lambda i: (i, 0)))
        out_shape.append(jax.ShapeDtypeStruct((r, c), jnp.bfloat16))
    return pl.pallas_call(
        _mem_proj_kernel,
        grid=(steps,),
        in_specs=in_specs,
        out_specs=out_specs,
        out_shape=out_shape,
        compiler_params=pltpu.CompilerParams(
            dimension_semantics=("parallel",), vmem_limit_bytes=V7X_VMEM_LIMIT),
        name="mem_proj",
    )(mem, g.reshape(1, d), w_bf16, *[w for w, _ in casts])


def _out_proj_kernel(x_ref, yna_ref, ydl_ref, ym_ref, w_ref, g_ref, *rest, final_norm):
    if len(rest) == 3:
        wnext_ref, o_ref, wnext_bf16_ref = rest
        wnext_bf16_ref[...] = wnext_ref[...].astype(wnext_bf16_ref.dtype)
    else:
        (o_ref,) = rest
    chunk = x_ref.shape[0] // OUT_CHUNKS
    for c in range(OUT_CHUNKS):
        rows = slice(c * chunk, (c + 1) * chunk)
        acc = jnp.dot(yna_ref[rows, :], w_ref[0:W_NA, :], preferred_element_type=jnp.float32)
        acc += jnp.dot(ydl_ref[rows, :], w_ref[W_NA:W_NA + W_DIL, :], preferred_element_type=jnp.float32)
        acc += jnp.dot(ym_ref[rows, :], w_ref[W_NA + W_DIL:, :], preferred_element_type=jnp.float32)
        x = x_ref[rows, :] + acc
        if final_norm:
            x = _rms_norm_f32(x, g_ref[...])
        o_ref[rows, :] = x


def _out_proj(x, y_na, y_dl, y_m, w_bf16, final_g, final_norm, w_in_f32=None, next_layer=None):
    n, d = x.shape
    tm = OUT_TM
    steps = n // tm
    in_specs = [
        pl.BlockSpec((tm, d), lambda i: (i, 0)),
        pl.BlockSpec((tm, W_NA), lambda i: (i, 0)),
        pl.BlockSpec((tm, W_DIL), lambda i: (i, 0)),
        pl.BlockSpec((tm, W_MEM), lambda i: (i, 0)),
        pl.BlockSpec((d, d), lambda i: (0, 0), pipeline_mode=pl.Buffered(1)),
        pl.BlockSpec((1, d), lambda i: (0, 0)),
    ]
    out_specs = [pl.BlockSpec((tm, d), lambda i: (i, 0))]
    out_shape = [jax.ShapeDtypeStruct((n, d), jnp.float32)]
    args = [x, y_na, y_dl, y_m, w_bf16, final_g.reshape(1, d)]
    if w_in_f32 is not None:
        cols = w_in_f32.shape[2]
        assert d % steps == 0
        slab = d // steps
        in_specs.append(pl.BlockSpec((None, slab, cols), lambda i: (next_layer, i, 0)))
        out_specs.append(pl.BlockSpec((None, slab, cols), lambda i: (0, i, 0)))
        out_shape.append(jax.ShapeDtypeStruct((1, d, cols), jnp.bfloat16))
        args.append(w_in_f32)
    res = pl.pallas_call(
        functools.partial(_out_proj_kernel, final_norm=final_norm),
        grid=(steps,),
        in_specs=in_specs,
        out_specs=out_specs,
        out_shape=out_shape,
        compiler_params=pltpu.CompilerParams(
            dimension_semantics=("parallel",), vmem_limit_bytes=V7X_VMEM_LIMIT),
        name="out_proj",
    )(*args)
    return res if w_in_f32 is not None else res[0]


_Block = collections.namedtuple("_Block", "scores pv finish prior stats depth flush")


def _run_pipeline(blocks):
    issued = []
    pending = None
    for i, blk in enumerate(blocks):
        while len(issued) < min(i + 1 + blk.depth, len(blocks)):
            issued.append(blocks[len(issued)].scores())
        s = issued[i]
        issued[i] = None
        if blk.flush and pending is not None:
            pending[0].finish(*pending[1:])
            pending = None
        m = jnp.max(s, axis=-1, keepdims=True)
        if blk.stats is not None:
            blk.stats(m)
        if blk.prior is None:
            p = jnp.exp2(s - m)
            acc, l = blk.pv(p.astype(jnp.bfloat16))
        else:
            m_b, l_b, acc_b = blk.prior()
            m = jnp.maximum(m, m_b)
            p = jnp.exp2(s - jnp.concatenate([m] * (s.shape[1] // m.shape[1]), axis=1))
            alpha = jnp.exp2(m_b - m)
            acc, l = blk.pv(p.astype(jnp.bfloat16))
            l = alpha * l_b + l
            acc = alpha * acc_b + acc
        if pending is not None:
            pending[0].finish(*pending[1:])
        pending = (blk, l, acc)
    pending[0].finish(*pending[1:])


def _na_kernel(q_ref, k_ref, v_ref, gate_ref, toep_ref, o_ref, bias_ref):
    n_groups = q_ref.shape[0] // NA_Q
    rows = q_ref.shape[0] // GRID_W

    def window_start(r):
        return int(np.clip(r - NA_WIN_ROWS // 2, 0, rows - NA_WIN_ROWS))

    def key_rows(g):
        lo = window_start(g * NA_ROWS_PER_GROUP)
        hi = window_start((g + 1) * NA_ROWS_PER_GROUP - 1) + NA_WIN_ROWS
        n = -(-(hi - lo) // NA_KEY_STEP) * NA_KEY_STEP
        return min(lo, rows - n), n

    @pl.when(pl.program_id(1) == 0)
    def _():
        outside = jnp.full((GRID_W, GRID_W), NEG, jnp.float32)
        for hh in range(q_ref.shape[1] // HEAD_DIM):
            for pat, g in enumerate((0, 1, n_groups - 1)):
                base, n_rows = key_rows(g)
                for qi in range(NA_ROWS_PER_GROUP):
                    r = g * NA_ROWS_PER_GROUP + qi
                    rs = window_start(r)
                    for kj in range(0, n_rows, 2):
                        pair = [toep_ref[hh, base + kk - r + NA_WIN_ROWS - 1]
                                if rs <= base + kk < rs + NA_WIN_ROWS else outside for kk in (kj, kj + 1)]
                        bias_ref[hh, pat, qi * GRID_W:(qi + 1) * GRID_W, kj * GRID_W:(kj + 2) * GRID_W] = (
                            jnp.concatenate(pair, axis=1))

    def block(hh, g):
        cols = _head_cols(hh)
        qrows = slice(g * NA_Q, (g + 1) * NA_Q)
        base, n_rows = key_rows(g)
        krows = slice(base * GRID_W, (base + n_rows) * GRID_W)
        pat = 0 if g == 0 else (2 if g == n_groups - 1 else 1)

        def scores():
            return _dot_nt(q_ref[qrows, cols], k_ref[krows, cols]) + bias_ref[hh, pat, :, 0:n_rows * GRID_W]

        def pv(p):
            return _pv_with_rowsum(p, v_ref[krows, cols])

        def finish(l, acc):
            o = (acc / l) * gate_ref[qrows, cols].astype(jnp.float32)
            o_ref[qrows, cols] = o.astype(o_ref.dtype)

        return _Block(scores, pv, finish, None, None, NA_DEPTH, False)

    _run_pipeline([block(hh, g) for hh in range(q_ref.shape[1] // HEAD_DIM) for g in range(n_groups)])


def _na_toeplitz(rpb):
    c = np.arange(GRID_W)
    cs = np.clip(c - NA_WIN_COLS // 2, 0, GRID_W - NA_WIN_COLS)
    col_ok = (c[None, :] >= cs[:, None]) & (c[None, :] < cs[:, None] + NA_WIN_COLS)
    dc = np.clip(c[None, :] - c[:, None], -(NA_WIN_COLS - 1), NA_WIN_COLS - 1) + NA_WIN_COLS - 1
    onehot = (dc[None] == np.arange(2 * NA_WIN_COLS - 1)[:, None, None]).astype(np.float32)
    toep = jnp.einsum("lhaj,jqk->lhaqk", rpb.astype(jnp.float32), jnp.asarray(onehot),
                      precision=lax.Precision.HIGHEST)
    return jnp.where(col_ok, toep * LOG2E, NEG)


def _na_attention(z, toep, layer, batch, seq):
    n = z.shape[0]
    hc = NA_HPS * HEAD_DIM
    qb, kb, vb, gb = (OFF_NA_Q // hc, OFF_NA_K // hc, OFF_NA_V // hc, OFF_NA_G // hc)
    return pl.pallas_call(
        _na_kernel,
        grid=(N_HEADS_NA // NA_HPS, batch),
        in_specs=[
            pl.BlockSpec((seq, hc), lambda h, b: (b, qb + h)),
            pl.BlockSpec((seq, hc), lambda h, b: (b, kb + h)),
            pl.BlockSpec((seq, hc), lambda h, b: (b, vb + h)),
            pl.BlockSpec((seq, hc), lambda h, b: (b, gb + h)),
            pl.BlockSpec((None, NA_HPS) + toep.shape[2:], lambda h, b: (layer, h, 0, 0, 0)),
        ],
        out_specs=pl.BlockSpec((seq, hc), lambda h, b: (b, h)),
        out_shape=jax.ShapeDtypeStruct((n, W_NA), jnp.bfloat16),
        scratch_shapes=[pltpu.VMEM((NA_HPS, 3, NA_Q, NA_K), jnp.float32)],
        compiler_params=pltpu.CompilerParams(
            dimension_semantics=("parallel", "arbitrary"), vmem_limit_bytes=V7X_VMEM_LIMIT),
        name="na_attention",
    )(z, z, z, z, toep)


def _dl_mem_kernel(qn_ref, kn_ref, vn_ref, gate_ref, qs_ref, ks_ref, vs_ref, bias_a_ref, bias_b_ref,
                   mq_ref, mk_ref, mv_ref, mgate_ref, o_ref, om_ref, m_scr, l_scr, acc_scr):
    seq = qn_ref.shape[0]
    slen = seq // DL_STREAMS

    def block_m(hh, i):
        cols = _head_cols(hh)
        qrows = slice(i * MEM_Q, (i + 1) * MEM_Q)

        def scores():
            return _dot_nt(mq_ref[qrows, cols], mk_ref[:, cols])

        def pv(p):
            return _pv_with_rowsum(p, mv_ref[:, cols])

        def finish(l, acc):
            o = (acc / l) * mgate_ref[qrows, cols].astype(jnp.float32)
            om_ref[qrows, cols] = o.astype(om_ref.dtype)

        return _Block(scores, pv, finish, None, None, MEM_DEPTH, False)

    def block_b(hh, r):
        cols = _head_cols(hh)
        rows = pl.ds(r, slen, stride=DL_STREAMS)

        def scores():
            return _dot_nt(qs_ref[r, :, cols], ks_ref[r, :, cols]) + bias_b_ref[...]

        def pv(p):
            return _pv_with_rowsum(p, vs_ref[r, :, cols])

        def stats(m):
            m_scr[hh, rows, :] = jnp.broadcast_to(m, (slen, HEAD_DIM))

        def finish(l, acc):
            l_scr[hh, rows, :] = l
            acc_scr[hh, rows, :] = acc

        return _Block(scores, pv, finish, None, stats, DL_DEPTH_B, False)

    def block_a(hh, i):
        cols = _head_cols(hh)
        qrows = slice(i * DL_Q, (i + 1) * DL_Q)
        k0 = int(np.clip(i * DL_Q - DL_HALF, 0, seq - DL_KA))
        krows = slice(k0, k0 + DL_KA)
        pat = (i * DL_Q - k0) // DL_HALF

        def scores():
            return _dot_nt(qn_ref[qrows, cols], kn_ref[krows, cols]) + bias_a_ref[pat]

        def pv(p):
            return _pv_with_rowsum(p, vn_ref[krows, cols])

        def prior():
            return m_scr[hh, qrows, :], l_scr[hh, qrows, :], acc_scr[hh, qrows, :]

        def finish(l, acc):
            o = (acc / l) * gate_ref[qrows, cols].astype(jnp.float32)
            o_ref[qrows, cols] = o.astype(o_ref.dtype)

        return _Block(scores, pv, finish, prior, None, DL_DEPTH_A, i == 0)

    heads = range(qn_ref.shape[1] // HEAD_DIM)
    mem_heads = range(mq_ref.shape[1] // HEAD_DIM)
    _run_pipeline([block_m(hh, i) for hh in mem_heads for i in range(seq // MEM_Q)]
                  + [block_b(hh, r) for hh in heads for r in range(DL_STREAMS)]
                  + [block_a(hh, i) for hh in heads for i in range(seq // DL_Q)])


def _dl_bias_tables(seq):
    (w1, d1), (w4, d4), (w16, d16) = DIL_CONFIGS
    assert d1 == 1 and d4 == DL_STREAMS and d16 % d4 == 0
    assert (w1 // 2) // d1 == DL_HALF and (w4 // 2) // d4 == DL_HALF and (w16 // 2) // d16 == DL_HALF
    r = np.arange(DL_Q)[:, None]
    j = np.arange(DL_KA)[None, :]
    bias_a = np.stack([np.where(np.abs(j - off - r) <= DL_HALF, 0.0, NEG) for off in (0, DL_HALF, 2 * DL_HALF)])
    slen = seq // DL_STREAMS
    delta = np.arange(slen)[None, :] - np.arange(slen)[:, None]
    step = d16 // d4
    mult = (np.abs(delta) <= DL_HALF).astype(np.float64) + ((delta % step == 0) & (np.abs(delta) <= DL_HALF * step))
    bias_b = np.where(mult > 0, np.log2(np.maximum(mult, 1.0)), NEG)
    return jnp.asarray(bias_a, jnp.float32), jnp.asarray(bias_b, jnp.float32)


def _dl_mem_attention(z, zs, mkv, bias_a, bias_b, batch, seq):
    n = z.shape[0]
    hc = DL_HPS * HEAD_DIM
    qb, kb, vb, gb = (OFF_DL_Q // hc, OFF_DL_K // hc, OFF_DL_V // hc, OFF_DL_G // hc)
    slen = seq // DL_STREAMS
    steps = N_HEADS_DIL // DL_HPS
    stream_spec = lambda c0: pl.BlockSpec((None, DL_STREAMS, slen, hc), lambda b, h: (b, 0, 0, c0 + h))
    mc = MEM_HPS * HEAD_DIM
    mqb, mgb, mvb = OFF_M_Q // mc, OFF_M_G // mc, W_MEM // mc
    return pl.pallas_call(
        _dl_mem_kernel,
        grid=(batch, steps),
        in_specs=[
            pl.BlockSpec((seq, hc), lambda b, h: (b, qb + h)),
            pl.BlockSpec((seq, hc), lambda b, h: (b, kb + h)),
            pl.BlockSpec((seq, hc), lambda b, h: (b, vb + h)),
            pl.BlockSpec((seq, hc), lambda b, h: (b, gb + h)),
            stream_spec(0), stream_spec(steps), stream_spec(2 * steps),
            pl.BlockSpec(bias_a.shape, lambda b, h: (0, 0, 0)),
            pl.BlockSpec(bias_b.shape, lambda b, h: (0, 0)),
            pl.BlockSpec((seq, mc), lambda b, h: (b, mqb + h)),
            pl.BlockSpec((N_MEM, mc), lambda b, h: (b, h)),
            pl.BlockSpec((N_MEM, mc), lambda b, h: (b, mvb + h)),
            pl.BlockSpec((seq, mc), lambda b, h: (b, mgb + h)),
        ],
        out_specs=[pl.BlockSpec((seq, hc), lambda b, h: (b, h)),
                   pl.BlockSpec((seq, mc), lambda b, h: (b, h))],
        out_shape=[jax.ShapeDtypeStruct((n, W_DIL), jnp.bfloat16),
                   jax.ShapeDtypeStruct((n, W_MEM), jnp.bfloat16)],
        scratch_shapes=[pltpu.VMEM((DL_HPS, seq, HEAD_DIM), jnp.float32)] * 3,
        compiler_params=pltpu.CompilerParams(
            dimension_semantics=("parallel", "parallel"), vmem_limit_bytes=V7X_VMEM_LIMIT),
        name="dl_mem_attention",
    )(z, z, z, z, zs, zs, zs, bias_a, bias_b, z, mkv, mkv, z)


def _rope_tables(seq):
    half = HEAD_DIM // 2
    inv = ROPE_THETA ** (-jnp.arange(half, dtype=jnp.float32) / half)
    ang = jnp.arange(seq).astype(jnp.float32)[:, None] * inv[None, :]
    cos, sin = jnp.cos(ang), jnp.sin(ang)
    return jnp.concatenate([cos, cos], axis=-1), jnp.concatenate([-sin, sin], axis=-1)


def kernel(x, mem, norm_g, w_in, na_rpb, mem_norm_g, w_mem_kv, w_out, final_g):
    batch, seq, d = x.shape
    depth = w_in.shape[0]
    assert d == D_MODEL and w_in.shape[2] == IN_COLS and seq % GRID_W == 0
    assert seq % PROJ_TM == 0 and seq % OUT_TM == 0 and seq % NA_Q == 0 and seq % MEM_Q == 0
    assert seq % (DL_STREAMS * DL_Q) == 0 and PROJ_TM % DL_STREAMS == 0
    assert all(off % (NA_HPS * HEAD_DIM) == 0 for off in (OFF_NA_Q, OFF_NA_K, OFF_NA_V, OFF_NA_G))
    assert all(off % (DL_HPS * HEAD_DIM) == 0 for off in (OFF_DL_Q, OFF_DL_K, OFF_DL_V, OFF_DL_G))
    assert all(off % (MEM_HPS * HEAD_DIM) == 0 for off in (OFF_M_Q, OFF_M_G, W_MEM))
    assert N_HEADS_NA % NA_HPS == 0 and N_HEADS_DIL % DL_HPS == 0
    assert N_HEADS_MEM * DL_HPS % N_HEADS_DIL == 0

    cos2, sin2 = _rope_tables(seq)
    dl_bias_a, dl_bias_b = _dl_bias_tables(seq)
    na_toep = _na_toeplitz(na_rpb)
    w_mem_b = w_mem_kv.astype(jnp.bfloat16)
    mem_f = mem.reshape(batch * N_MEM, d)

    xf = x.reshape(batch * seq, d)
    w_in_b = None
    for l in range(depth):
        casts = [(w_out, l)] + ([(w_in, 0)] if l == 0 else [])
        mkv, w_out_b, *first = _mem_proj(mem_f, mem_norm_g, w_mem_b, l, casts)
        if first:
            w_in_b = first[0][None]
        z, zs = _in_proj(xf, norm_g[l], w_in_b, 0, cos2, sin2, IN_SEGMENTS, batch, seq)
        y_na = _na_attention(z, na_toep, l, batch, seq)
        y_dl, y_m = _dl_mem_attention(z, zs, mkv, dl_bias_a, dl_bias_b, batch, seq)
        if l + 1 < depth:
            xf, w_in_b = _out_proj(xf, y_na, y_dl, y_m, w_out_b, final_g, False, w_in_f32=w_in, next_layer=l + 1)
        else:
            xf = _out_proj(xf, y_na, y_dl, y_m, w_out_b, final_g, True)
    return xf.reshape(batch, seq, d)
```
